```python
import jax, jax.numpy as jnp
from jax import lax
import numpy as np

D_MODEL = 2048
BATCH = 4
SEQ = 4096
DEPTH = 2

N_MIXERS = 2
FOX_HEADS = 16
FOX_HEAD_DIM = D_MODEL // FOX_HEADS
RET_HEADS = 8
RET_QK_DIM = D_MODEL // RET_HEADS
RET_V_DIM = 2 * RET_QK_DIM
D_FF = 4 * D_MODEL
Q_BLOCK = 128
RET_CHUNK = 128
RMS_EPS = 1e-6
ROPE_BASE = 10000.0
N_FOX = (DEPTH + 1) // 2
N_RET = DEPTH // 2

kernel_name = "fox_retnet_hybrid_trunk"


def rmsnorm(x, g):
    x32 = x.astype(jnp.float32)
    y = x32 * lax.rsqrt(jnp.mean(jnp.square(x32), axis=-1, keepdims=True) + RMS_EPS)
    return (y * g.astype(jnp.float32)).astype(x.dtype)


def fox_mixer(h, wq, wk, wv, wf, bf, wo):
    B, S, _ = h.shape
    H, dh = FOX_HEADS, FOX_HEAD_DIM

    def heads(w):
        return (h @ w).reshape(B, S, H, dh).transpose(0, 2, 1, 3).astype(jnp.float32)

    q = heads(wq) * (dh ** -0.5)
    k = heads(wk)
    v = heads(wv)
    log_f = jax.nn.log_sigmoid((h @ wf + bf).astype(jnp.float32))
    c = jnp.cumsum(log_f, axis=1).transpose(0, 2, 1)

    nb = S // Q_BLOCK
    q_blocks = q.reshape(B, H, nb, Q_BLOCK, dh).transpose(2, 0, 1, 3, 4)
    c_blocks = c.reshape(B, H, nb, Q_BLOCK).transpose(2, 0, 1, 3)
    k_pos = jnp.arange(S)

    def one_block(args):
        qi, ci, bi = args
        q_pos = bi * Q_BLOCK + jnp.arange(Q_BLOCK)
        logits = jnp.einsum('bhqd,bhkd->bhqk', qi, k) + ci[..., None] - c[:, :, None, :]
        logits = jnp.where(k_pos[None, :] <= q_pos[:, None], logits, -jnp.inf)
        p = jax.nn.softmax(logits, axis=-1)
        return jnp.einsum('bhqk,bhkd->bhqd', p, v)

    o = lax.map(one_block, (q_blocks, c_blocks, jnp.arange(nb)))
    o = o.transpose(1, 0, 3, 2, 4).reshape(B, S, H * dh)
    return o.astype(h.dtype) @ wo


def rotary(x, pos):
    half = x.shape[-1] // 2
    inv = ROPE_BASE ** (-jnp.arange(half, dtype=jnp.float32) / half)
    ang = pos[:, None] * inv[None, :]
    cos, sin = jnp.cos(ang), jnp.sin(ang)
    x1, x2 = x[..., :half], x[..., half:]
    return jnp.concatenate([x1 * cos - x2 * sin, x1 * sin + x2 * cos], axis=-1)


def retention_mixer(h, wq, wk, wv, wg, gn, wo):
    B, S, _ = h.shape
    H, dk, dv, C = RET_HEADS, RET_QK_DIM, RET_V_DIM, RET_CHUNK
    pos = jnp.arange(S, dtype=jnp.float32)
    q = (h @ wq).reshape(B, S, H, dk).transpose(0, 2, 1, 3).astype(jnp.float32)
    k = (h @ wk).reshape(B, S, H, dk).transpose(0, 2, 1, 3).astype(jnp.float32)
    v = (h @ wv).reshape(B, S, H, dv).transpose(0, 2, 1, 3).astype(jnp.float32)
    q = rotary(q, pos)
    k = rotary(k, pos) * (dk ** -0.5)

    log_gamma = jnp.log1p(-jnp.exp2(-5.0 - jnp.arange(H, dtype=jnp.float32)))
    idx = jnp.arange(C, dtype=jnp.float32)
    diff = idx[:, None] - idx[None, :]
    decay_in = jnp.where(diff >= 0,
                         jnp.exp(log_gamma[:, None, None] * jnp.maximum(diff, 0.0)), 0.0)
    q_decay = jnp.exp(log_gamma[:, None] * (idx + 1.0))[:, :, None]
    k_decay = jnp.exp(log_gamma[:, None] * (C - 1.0 - idx))[:, :, None]
    chunk_decay = jnp.exp(log_gamma * C)[:, None, None]

    nc = S // C
    qc = q.reshape(B, H, nc, C, dk).transpose(2, 0, 1, 3, 4)
    kc = k.reshape(B, H, nc, C, dk).transpose(2, 0, 1, 3, 4)
    vc = v.reshape(B, H, nc, C, dv).transpose(2, 0, 1, 3, 4)

    def step(state, xs):
        qi, ki, vi = xs
        inner = jnp.einsum('bhid,bhjd->bhij', qi, ki) * decay_in
        o = (jnp.einsum('bhij,bhje->bhie', inner, vi)
             + jnp.einsum('bhid,bhde->bhie', qi * q_decay, state))
        state = chunk_decay * state + jnp.einsum('bhjd,bhje->bhde', ki * k_decay, vi)
        return state, o

    state0 = jnp.zeros((B, H, dk, dv), jnp.float32)
    _, o = lax.scan(step, state0, (qc, kc, vc))
    o = o.transpose(1, 0, 3, 2, 4).reshape(B, S, H, dv)
    o = o * lax.rsqrt(jnp.mean(jnp.square(o), axis=-1, keepdims=True) + RMS_EPS)
    o = o.reshape(B, S, H * dv) * gn.astype(jnp.float32)
    gate = jax.nn.silu((h @ wg).astype(jnp.float32))
    return (gate * o).astype(h.dtype) @ wo


def sqrelu_mlp(h, w_up, w_down):
    return jnp.square(jax.nn.relu(h @ w_up)) @ w_down


def setup_inputs(seed: int = 0) -> dict:
    key = jax.random.key(seed)
    ks = jax.random.split(key, 24)
    D = D_MODEL

    def w(k, shape, fan_in):
        return jax.random.normal(k, shape, jnp.float32) * (fan_in ** -0.5)

    def gain(k, shape):
        return 1.0 + 0.05 * jax.random.normal(k, shape, jnp.float32)

    fd = FOX_HEADS * FOX_HEAD_DIM
    rk = RET_HEADS * RET_QK_DIM
    rv = RET_HEADS * RET_V_DIM
    return {
        "x": jax.random.normal(ks[0], (BATCH, SEQ, D), jnp.float32),
        "fox_norm": gain(ks[1], (N_FOX, D)),
        "fox_wq": w(ks[2], (N_FOX, D, fd), D),
        "fox_wk": w(ks[3], (N_FOX, D, fd), D),
        "fox_wv": w(ks[4], (N_FOX, D, fd), D),
        "fox_wf": w(ks[5], (N_FOX, D, FOX_HEADS), D),
        "fox_bf": jax.random.uniform(ks[6], (N_FOX, FOX_HEADS), jnp.float32, 1.0, 5.0),
        "fox_wo": w(ks[7], (N_FOX, fd, D), fd),
        "ret_norm": gain(ks[8], (N_RET, D)),
        "ret_wq": w(ks[9], (N_RET, D, rk), D),
        "ret_wk": w(ks[10], (N_RET, D, rk), D),
        "ret_wv": w(ks[11], (N_RET, D, rv), D),
        "ret_wg": w(ks[12], (N_RET, D, rv), D),
        "ret_gn": gain(ks[13], (N_RET, rv)),
        "ret_wo": w(ks[14], (N_RET, rv, D), rv),
        "mlp_norm": gain(ks[15], (DEPTH, D)),
        "mlp_up": w(ks[16], (DEPTH, D, D_FF), D),
        "mlp_down": w(ks[17], (DEPTH, D_FF, D), D_FF),
        "final_norm": gain(ks[18], (D,)),
    }


def reference(x, fox_norm, fox_wq, fox_wk, fox_wv, fox_wf, fox_bf, fox_wo,
              ret_norm, ret_wq, ret_wk, ret_wv, ret_wg, ret_gn, ret_wo,
              mlp_norm, mlp_up, mlp_down, final_norm):
    h = x
    for layer in range(DEPTH):
        j = layer // N_MIXERS
        if layer % N_MIXERS == 0:
            h = h + fox_mixer(rmsnorm(h, fox_norm[j]), fox_wq[j], fox_wk[j], fox_wv[j],
                              fox_wf[j], fox_bf[j], fox_wo[j])
        else:
            h = h + retention_mixer(rmsnorm(h, ret_norm[j]), ret_wq[j], ret_wk[j], ret_wv[j],
                                    ret_wg[j], ret_gn[j], ret_wo[j])
        h = h + sqrelu_mlp(rmsnorm(h, mlp_norm[layer]), mlp_up[layer], mlp_down[layer])
    return rmsnorm(h, final_norm)
```

```python
import functools
import math

import jax
import jax.numpy as jnp
from jax import lax
from jax.experimental import pallas as pl
from jax.experimental.pallas import tpu as pltpu

F32 = jnp.float32
BF16 = jnp.bfloat16

RMS_EPS = 1e-6
ROPE_BASE = 10000.0
FOX_HEAD_DIM = 128
RET_QK_DIM = 256
RET_V_DIM = 512
LANES = 128
NEG_BIG = -1e30
V7X_VMEM_LIMIT = 56 * 1024 * 1024

_NT = (((1,), (1,)), ((), ()))


def _params(sem, vmem=V7X_VMEM_LIMIT):
    return pltpu.CompilerParams(dimension_semantics=sem, vmem_limit_bytes=vmem)


def _tile(n, target, align=LANES):
    if n <= target:
        return n
    t = (target // align) * align
    while n % t:
        t -= align
    return t


def _rms_rows(x, g):
    ms = jnp.mean(x * x, axis=-1, keepdims=True)
    return x * lax.rsqrt(ms + RMS_EPS) * g


def _norm_matmul_kernel(x_ref, g_ref, w_ref, cs_ref, *rest, has_gate, row_chunk):
    if has_gate:
        wz_ref, o_ref, z_ref, hn_ref = rest
    else:
        o_ref, hn_ref = rest
    tm = x_ref.shape[0]

    @pl.when(pl.program_id(1) == 0)
    def _():
        def body(i, carry):
            r0 = pl.multiple_of(i * row_chunk, row_chunk)
            y = _rms_rows(x_ref[pl.ds(r0, row_chunk), :], g_ref[...])
            hn_ref[pl.ds(r0, row_chunk), :] = y.astype(BF16)
            return carry
        lax.fori_loop(0, tm // row_chunk, body, 0)
        if has_gate:
            z_ref[...] = jnp.dot(hn_ref[...], wz_ref[...], preferred_element_type=F32)

    acc = jnp.dot(hn_ref[...], w_ref[...], preferred_element_type=F32)
    o_ref[...] = (acc * cs_ref[...]).astype(o_ref.dtype)


def _norm_matmul(x, g, w, colscale, wz=None, *, tm=1024, tn=1024):
    T, D = x.shape
    N = w.shape[1]
    tm = _tile(T, tm)
    tn = _tile(N, tn)
    has_gate = wz is not None
    in_specs = [
        pl.BlockSpec((tm, D), lambda i, j: (i, 0)),
        pl.BlockSpec((1, D), lambda i, j: (0, 0)),
        pl.BlockSpec((D, tn), lambda i, j: (0, j)),
        pl.BlockSpec((1, tn), lambda i, j: (0, j)),
    ]
    args = [x, g.reshape(1, D), w, colscale.reshape(1, N)]
    out_shape = [jax.ShapeDtypeStruct((T, N), BF16)]
    out_specs = [pl.BlockSpec((tm, tn), lambda i, j: (i, j))]
    if has_gate:
        in_specs.append(pl.BlockSpec((D, LANES), lambda i, j: (0, 0)))
        args.append(wz)
        out_shape.append(jax.ShapeDtypeStruct((T, LANES), F32))
        out_specs.append(pl.BlockSpec((tm, LANES), lambda i, j: (i, 0)))
    res = pl.pallas_call(
        functools.partial(_norm_matmul_kernel, has_gate=has_gate, row_chunk=min(128, tm)),
        grid=(T // tm, N // tn),
        in_specs=in_specs,
        out_specs=out_specs,
        out_shape=out_shape,
        scratch_shapes=[pltpu.VMEM((tm, D), BF16)],
        compiler_params=_params(("parallel", "arbitrary")),
        name="norm_matmul_gate" if has_gate else "norm_matmul",
    )(*args)
    return res if has_gate else res[0]


def _gate_cumsum_kernel(z_ref, b_ref, ccol_ref, crow_ref):
    z = z_ref[...] + b_ref[...]
    c = jnp.minimum(z, 0.0) - jnp.log1p(jnp.exp(-jnp.abs(z)))
    S = c.shape[0]
    row = lax.broadcasted_iota(jnp.int32, c.shape, 0)
    shift = 1
    while shift < S:
        c = c + jnp.where(row >= shift, pltpu.roll(c, shift, axis=0), 0.0)
        shift *= 2
    ccol_ref[...] = c
    crow_ref[...] = c.T


def _gate_cumsum(z, b_pad, batch):
    T = z.shape[0]
    S = T // batch
    return pl.pallas_call(
        _gate_cumsum_kernel,
        grid=(batch,),
        in_specs=[pl.BlockSpec((S, LANES), lambda b: (b, 0)),
                  pl.BlockSpec((1, LANES), lambda b: (0, 0))],
        out_specs=[pl.BlockSpec((S, LANES), lambda b: (b, 0)),
                   pl.BlockSpec((None, LANES, S), lambda b: (b, 0, 0))],
        out_shape=[jax.ShapeDtypeStruct((T, LANES), F32),
                   jax.ShapeDtypeStruct((batch, LANES, S), F32)],
        compiler_params=_params(("parallel",)),
        name="gate_cumsum",
    )(z, b_pad)


def _fox_attn_kernel(q_ref, k_ref, v_ref, cq_ref, ck_ref, o_ref, *, blk):
    h = pl.program_id(1)
    qi = pl.program_id(2)
    q = q_ref[...]
    lane = lax.broadcasted_iota(jnp.int32, cq_ref.shape, 1)
    cq = jnp.sum(jnp.where(lane == h, cq_ref[...], 0.0), axis=-1, keepdims=True)

    def step(j, carry, masked):
        m, l, acc = carry
        start = pl.multiple_of(j * blk, blk)
        k = k_ref[pl.ds(start, blk), :]
        v = v_ref[pl.ds(start, blk), :]
        s = lax.dot_general(q, k, _NT, preferred_element_type=F32)
        s = s + (cq - ck_ref[:, pl.ds(start, blk)])
        if masked:
            r = lax.broadcasted_iota(jnp.int32, s.shape, 0)
            c = lax.broadcasted_iota(jnp.int32, s.shape, 1)
            s = jnp.where(c <= r, s, NEG_BIG)
        m_new = jnp.maximum(m, jnp.max(s, axis=-1, keepdims=True))
        alpha = jnp.exp(m - m_new)
        p = jnp.exp(s - m_new)
        l = alpha * l + jnp.sum(p, axis=-1, keepdims=True)
        acc = alpha * acc + jnp.dot(p.astype(BF16), v, preferred_element_type=F32)
        return m_new, l, acc

    init = (jnp.full((blk, 1), NEG_BIG, F32), jnp.zeros((blk, 1), F32),
            jnp.zeros((blk, q.shape[1]), F32))
    carry = lax.fori_loop(0, qi, functools.partial(step, masked=False), init)
    _, l, acc = step(qi, carry, masked=True)
    o_ref[...] = (acc / l).astype(o_ref.dtype)


def _fox_attention(qkv, c_col, c_row, *, batch, heads, blk=512):
    T = qkv.shape[0]
    S = T // batch
    dh = FOX_HEAD_DIM
    blk = min(blk, S)
    nq = S // blk
    return pl.pallas_call(
        functools.partial(_fox_attn_kernel, blk=blk),
        grid=(batch, heads, nq),
        in_specs=[
            pl.BlockSpec((blk, dh), lambda b, h, i: (b * nq + i, h)),
            pl.BlockSpec((S, dh), lambda b, h, i: (b, heads + h)),
            pl.BlockSpec((S, dh), lambda b, h, i: (b, 2 * heads + h)),
            pl.BlockSpec((blk, LANES), lambda b, h, i: (b * nq + i, 0)),
            pl.BlockSpec((None, None, 1, S), lambda b, h, i: (b, h, 0, 0)),
        ],
        out_specs=pl.BlockSpec((blk, dh), lambda b, h, i: (b * nq + i, h)),
        out_shape=jax.ShapeDtypeStruct((T, heads * dh), BF16),
        compiler_params=_params(("parallel", "parallel", "arbitrary")),
        name="fox_attention",
    )(qkv, qkv, qkv, c_col, c_row)


def _matmul_res_kernel(a_ref, w_ref, r_ref, o_ref):
    o_ref[...] = r_ref[...] + jnp.dot(a_ref[...], w_ref[...], preferred_element_type=F32)


def _matmul_res(a, w, res, *, tm=1024, tn=512):
    T, K = a.shape
    N = w.shape[1]
    tm = _tile(T, tm)
    tn = _tile(N, tn)
    return pl.pallas_call(
        _matmul_res_kernel,
        grid=(T // tm, N // tn),
        in_specs=[pl.BlockSpec((tm, K), lambda i, j: (i, 0)),
                  pl.BlockSpec((K, tn), lambda i, j: (0, j)),
                  pl.BlockSpec((tm, tn), lambda i, j: (i, j))],
        out_specs=pl.BlockSpec((tm, tn), lambda i, j: (i, j)),
        out_shape=jax.ShapeDtypeStruct((T, N), F32),
        compiler_params=_params(("parallel", "arbitrary")),
        name="matmul_res",
    )(a, w, res)


def _mlp_kernel(x_ref, g_ref, wu_ref, wd_ref, *rest, has_final, row_chunk):
    if has_final:
        gf_ref, o_ref, hn_ref = rest
    else:
        o_ref, hn_ref = rest
    tm = x_ref.shape[0]
    f = pl.program_id(1)

    def rows(fn):
        def body(i, carry):
            fn(pl.ds(pl.multiple_of(i * row_chunk, row_chunk), row_chunk))
            return carry
        lax.fori_loop(0, tm // row_chunk, body, 0)

    @pl.when(f == 0)
    def _():
        def init(sl):
            x = x_ref[sl, :]
            hn_ref[sl, :] = _rms_rows(x, g_ref[...]).astype(BF16)
            o_ref[sl, :] = x
        rows(init)

    u = jnp.dot(hn_ref[...], wu_ref[...], preferred_element_type=F32)
    a = jnp.square(jnp.maximum(u, 0.0)).astype(BF16)
    o_ref[...] += jnp.dot(a, wd_ref[...], preferred_element_type=F32)

    if has_final:
        @pl.when(f == pl.num_programs(1) - 1)
        def _():
            def fin(sl):
                o_ref[sl, :] = _rms_rows(o_ref[sl, :], gf_ref[...])
            rows(fin)


def _mlp(x, g, wu, wd, g_final=None, *, tm=1024, tf=512):
    T, D = x.shape
    Fdim = wu.shape[1]
    tm = _tile(T, tm)
    tf = _tile(Fdim, tf)
    has_final = g_final is not None
    in_specs = [pl.BlockSpec((tm, D), lambda i, f: (i, 0)),
                pl.BlockSpec((1, D), lambda i, f: (0, 0)),
                pl.BlockSpec((D, tf), lambda i, f: (0, f)),
                pl.BlockSpec((tf, D), lambda i, f: (f, 0))]
    args = [x, g.reshape(1, D), wu, wd]
    if has_final:
        in_specs.append(pl.BlockSpec((1, D), lambda i, f: (0, 0)))
        args.append(g_final.reshape(1, D))
    return pl.pallas_call(
        functools.partial(_mlp_kernel, has_final=has_final, row_chunk=min(128, tm)),
        grid=(T // tm, Fdim // tf),
        in_specs=in_specs,
        out_specs=pl.BlockSpec((tm, D), lambda i, f: (i, 0)),
        out_shape=jax.ShapeDtypeStruct((T, D), F32),
        scratch_shapes=[pltpu.VMEM((tm, D), BF16)],
        compiler_params=_params(("parallel", "arbitrary")),
        name="mlp_final" if has_final else "mlp",
    )(*args)


def _retention_kernel(q_ref, k_ref, v_ref, g_ref, cos_ref, sin_ref, din_ref, qd_ref, kd_ref,
                      cd_ref, gn_ref, o_ref, state_ref, *, chunk, nchunks):
    @pl.when(pl.program_id(2) == 0)
    def _():
        state_ref[...] = jnp.zeros_like(state_ref)

    half = q_ref.shape[1] // 2
    for ci in range(nchunks):
        sl = slice(ci * chunk, (ci + 1) * chunk)
        cos = cos_ref[sl, :]
        sin = sin_ref[sl, :]

        def rot(x):
            x1, x2 = x[:, :half], x[:, half:]
            return jnp.concatenate([x1 * cos - x2 * sin, x1 * sin + x2 * cos], axis=-1)

        q = rot(q_ref[sl, :].astype(F32))
        k = rot(k_ref[sl, :].astype(F32))
        v = v_ref[sl, :]
        state = state_ref[...]

        inner = lax.dot_general(q.astype(BF16), k.astype(BF16), _NT,
                                preferred_element_type=F32) * din_ref[...]
        o = jnp.dot(inner.astype(BF16), v, preferred_element_type=F32)
        o = o + jnp.dot((q * qd_ref[...]).astype(BF16), state.astype(BF16),
                        preferred_element_type=F32)
        kd_t = (k * kd_ref[...]).T.astype(BF16)
        state_ref[...] = cd_ref[...] * state + jnp.dot(kd_t, v, preferred_element_type=F32)

        ms = jnp.mean(o * o, axis=-1, keepdims=True)
        on = o * lax.rsqrt(ms + RMS_EPS) * gn_ref[...]
        gate = g_ref[sl, :].astype(F32)
        gate = gate * jax.nn.sigmoid(gate)
        o_ref[sl, :] = (gate * on).astype(o_ref.dtype)


def _retention_tables(heads, chunk, seq):
    dk, dv = RET_QK_DIM, RET_V_DIM
    half = dk // 2
    pos = jnp.arange(seq, dtype=F32)
    inv = ROPE_BASE ** (-jnp.arange(half, dtype=F32) / half)
    ang = pos[:, None] * inv[None, :]
    log_gamma = jnp.log1p(-jnp.exp2(-5.0 - jnp.arange(heads, dtype=F32)))
    idx = jnp.arange(chunk, dtype=F32)
    diff = idx[:, None] - idx[None, :]
    decay_in = jnp.where(diff >= 0,
                         jnp.exp(log_gamma[:, None, None] * jnp.maximum(diff, 0.0)), 0.0)
    q_decay = jnp.exp(log_gamma[:, None] * (idx + 1.0))[:, :, None]
    k_decay = jnp.exp(log_gamma[:, None] * (chunk - 1.0 - idx))[:, :, None]
    chunk_decay = jnp.exp(log_gamma * chunk)[:, None, None]
    return (jnp.cos(ang), jnp.sin(ang), decay_in,
            jnp.broadcast_to(q_decay, (heads, chunk, dk)),
            jnp.broadcast_to(k_decay, (heads, chunk, dk)),
            jnp.broadcast_to(chunk_decay, (heads, 1, dv)))


def _retention(proj, gn, *, batch, heads, chunk=256, rows=512):
    T = proj.shape[0]
    S = T // batch
    dk, dv = RET_QK_DIM, RET_V_DIM
    chunk = min(chunk, S)
    rows = min(rows, S)
    ns = S // rows
    cos, sin, din, qd, kd, cd = _retention_tables(heads, chunk, S)
    vblk0 = 2 * heads * dk // dv
    return pl.pallas_call(
        functools.partial(_retention_kernel, chunk=chunk, nchunks=rows // chunk),
        grid=(batch, heads, ns),
        in_specs=[
            pl.BlockSpec((rows, dk), lambda b, h, s: (b * ns + s, h)),
            pl.BlockSpec((rows, dk), lambda b, h, s: (b * ns + s, heads + h)),
            pl.BlockSpec((rows, dv), lambda b, h, s: (b * ns + s, vblk0 + h)),
            pl.BlockSpec((rows, dv), lambda b, h, s: (b * ns + s, vblk0 + heads + h)),
            pl.BlockSpec((rows, dk // 2), lambda b, h, s: (s, 0)),
            pl.BlockSpec((rows, dk // 2), lambda b, h, s: (s, 0)),
            pl.BlockSpec((None, chunk, chunk), lambda b, h, s: (h, 0, 0)),
            pl.BlockSpec((None, chunk, dk), lambda b, h, s: (h, 0, 0)),
            pl.BlockSpec((None, chunk, dk), lambda b, h, s: (h, 0, 0)),
            pl.BlockSpec((None, 1, dv), lambda b, h, s: (h, 0, 0)),
            pl.BlockSpec((1, dv), lambda b, h, s: (0, h)),
        ],
        out_specs=pl.BlockSpec((rows, dv), lambda b, h, s: (b * ns + s, h)),
        out_shape=jax.ShapeDtypeStruct((T, heads * dv), BF16),
        scratch_shapes=[pltpu.VMEM((dk, dv), F32)],
        compiler_params=_params(("parallel", "parallel", "arbitrary")),
        name="retention",
    )(proj, proj, proj, proj, cos, sin, din, qd, kd, cd, gn.reshape(1, heads * dv))


def kernel(x, fox_norm, fox_wq, fox_wk, fox_wv, fox_wf, fox_bf, fox_wo, ret_norm, ret_wq, ret_wk,
           ret_wv, ret_wg, ret_gn, ret_wo, mlp_norm, mlp_up, mlp_down, final_norm):
    B, S, D = x.shape
    T = B * S
    fox_heads = D // FOX_HEAD_DIM
    ret_heads = D // RET_QK_DIM
    h = x.reshape(T, D)

    w_qkv = jnp.concatenate([fox_wq[0], fox_wk[0], fox_wv[0]], axis=1).astype(BF16)
    cs = jnp.concatenate([jnp.full((D,), FOX_HEAD_DIM ** -0.5, F32), jnp.ones((2 * D,), F32)])
    wf_pad = jnp.pad(fox_wf[0], ((0, 0), (0, LANES - fox_heads))).astype(BF16)
    bf_pad = jnp.pad(fox_bf[0], (0, LANES - fox_heads)).reshape(1, LANES)
    qkv, z = _norm_matmul(h, fox_norm[0], w_qkv, cs, wf_pad)
    c_col, c_row = _gate_cumsum(z, bf_pad, B)
    c_row = c_row[:, :fox_heads, :].reshape(B, fox_heads, 1, S)
    attn = _fox_attention(qkv, c_col, c_row, batch=B, heads=fox_heads)
    h = _matmul_res(attn, fox_wo[0].astype(BF16), h)
    h = _mlp(h, mlp_norm[0], mlp_up[0].astype(BF16), mlp_down[0].astype(BF16))

    w_proj = jnp.concatenate([ret_wq[0], ret_wk[0], ret_wv[0], ret_wg[0]], axis=1).astype(BF16)
    cs = jnp.concatenate([jnp.ones((D,), F32), jnp.full((D,), RET_QK_DIM ** -0.5, F32),
                          jnp.ones((2 * ret_heads * RET_V_DIM,), F32)])
    proj = _norm_matmul(h, ret_norm[0], w_proj, cs)
    y = _retention(proj, ret_gn[0], batch=B, heads=ret_heads)
    h = _matmul_res(y, ret_wo[0].astype(BF16), h)
    h = _mlp(h, mlp_norm[1], mlp_up[1].astype(BF16), mlp_down[1].astype(BF16), final_norm)
    return h.reshape(B, S, D)
```

```python
import functools
import math

import jax
import jax.numpy as jnp
from jax import lax
from jax.experimental import pallas as pl
from jax.experimental.pallas import tpu as pltpu

F32 = jnp.float32
BF16 = jnp.bfloat16

RMS_EPS = 1e-6
ROPE_BASE = 10000.0
FOX_HEAD_DIM = 128
RET_QK_DIM = 256
RET_V_DIM = 512
LANES = 128
NEG_BIG = -1e30
LOG2E = math.log2(math.e)
V7X_VMEM_LIMIT = 56 * 1024 * 1024

_NT = (((1,), (1,)), ((), ()))


def _params(sem, vmem=V7X_VMEM_LIMIT):
    return pltpu.CompilerParams(dimension_semantics=sem, vmem_limit_bytes=vmem)


def _tile(n, target, align=LANES):
    if n <= target:
        return n
    t = (target // align) * align
    while n % t:
        t -= align
    return t


def _rms_rows(x, g):
    ms = jnp.mean(x * x, axis=-1, keepdims=True)
    return x * lax.rsqrt(ms + RMS_EPS) * g


def _norm_matmul_kernel(x_ref, g_ref, w_ref, cs_ref, *rest, has_gate, row_chunk):
    if has_gate:
        wz_ref, o_ref, z_ref, hn_ref = rest
    else:
        o_ref, hn_ref = rest
    tm = x_ref.shape[0]

    @pl.when(pl.program_id(1) == 0)
    def _():
        def body(i, carry):
            r0 = pl.multiple_of(i * row_chunk, row_chunk)
            y = _rms_rows(x_ref[pl.ds(r0, row_chunk), :], g_ref[...])
            hn_ref[pl.ds(r0, row_chunk), :] = y.astype(BF16)
            return carry
        lax.fori_loop(0, tm // row_chunk, body, 0)
        if has_gate:
            z_ref[...] = jnp.dot(hn_ref[...], wz_ref[...], preferred_element_type=F32)

    acc = jnp.dot(hn_ref[...], w_ref[...], preferred_element_type=F32)
    o_ref[...] = (acc * cs_ref[...]).astype(o_ref.dtype)


def _norm_matmul(x, g, w, colscale, wz=None, *, tm=1024, tn=1024):
    T, D = x.shape
    N = w.shape[1]
    tm = _tile(T, tm)
    tn = _tile(N, tn)
    has_gate = wz is not None
    in_specs = [
        pl.BlockSpec((tm, D), lambda i, j: (i, 0)),
        pl.BlockSpec((1, D), lambda i, j: (0, 0)),
        pl.BlockSpec((D, tn), lambda i, j: (0, j)),
        pl.BlockSpec((1, tn), lambda i, j: (0, j)),
    ]
    args = [x, g.reshape(1, D), w, colscale.reshape(1, N)]
    out_shape = [jax.ShapeDtypeStruct((T, N), BF16)]
    out_specs = [pl.BlockSpec((tm, tn), lambda i, j: (i, j))]
    if has_gate:
        in_specs.append(pl.BlockSpec((D, LANES), lambda i, j: (0, 0)))
        args.append(wz)
        out_shape.append(jax.ShapeDtypeStruct((T, LANES), F32))
        out_specs.append(pl.BlockSpec((tm, LANES), lambda i, j: (i, 0)))
    res = pl.pallas_call(
        functools.partial(_norm_matmul_kernel, has_gate=has_gate, row_chunk=min(128, tm)),
        grid=(T // tm, N // tn),
        in_specs=in_specs,
        out_specs=out_specs,
        out_shape=out_shape,
        scratch_shapes=[pltpu.VMEM((tm, D), BF16)],
        compiler_params=_params(("parallel", "arbitrary")),
        name="norm_matmul_gate" if has_gate else "norm_matmul",
    )(*args)
    return res if has_gate else res[0]


def _gate_cumsum_kernel(z_ref, b_ref, c3_ref):
    z = z_ref[...] + b_ref[...]
    c = jnp.minimum(z, 0.0) - jnp.log1p(jnp.exp(-jnp.abs(z)))
    S = c.shape[0]
    row = lax.broadcasted_iota(jnp.int32, c.shape, 0)
    shift = 1
    while shift < S:
        c = c + jnp.where(row >= shift, pltpu.roll(c, shift, axis=0), 0.0)
        shift *= 2
    c = c * LOG2E
    hi = c.astype(BF16)
    r1 = c - hi.astype(F32)
    mid = r1.astype(BF16)
    lo = (r1 - mid.astype(F32)).astype(BF16)
    c3_ref[:, 0:LANES] = hi
    c3_ref[:, LANES:2 * LANES] = mid
    c3_ref[:, 2 * LANES:3 * LANES] = lo


def _gate_cumsum(z, b_pad, batch):
    T = z.shape[0]
    S = T // batch
    return pl.pallas_call(
        _gate_cumsum_kernel,
        grid=(batch,),
        in_specs=[pl.BlockSpec((S, LANES), lambda b: (b, 0)),
                  pl.BlockSpec((1, LANES), lambda b: (0, 0))],
        out_specs=pl.BlockSpec((S, 3 * LANES), lambda b: (b, 0)),
        out_shape=jax.ShapeDtypeStruct((T, 3 * LANES), BF16),
        compiler_params=_params(("parallel",)),
        name="gate_cumsum",
    )(z, b_pad)


ONES_ROWS = 16


def _piece_selector(h, sign, lane0):
    r = lax.broadcasted_iota(jnp.int32, (3 * LANES, LANES), 0)
    l = lax.broadcasted_iota(jnp.int32, (3 * LANES, LANES), 1)
    hit = (r == h + LANES * (l - lane0)) & (l >= lane0) & (l < lane0 + 3)
    return jnp.where(hit, sign, 0.0).astype(BF16)


def _fox_attn_kernel(q_ref, k_ref, v_ref, c3_ref, o_ref, kx_ref, vt_ref, acc_ref, st_ref,
                     *, tq, tk, hpg):
    hg = pl.program_id(1)
    qi = pl.program_id(2)
    S = k_ref.shape[0]
    dh = FOX_HEAD_DIM
    lane = lax.broadcasted_iota(jnp.int32, (1, LANES), 1)
    heads = [hg * hpg + g for g in range(hpg)]
    cols = [slice(g * dh, (g + 1) * dh) for g in range(hpg)]

    @pl.when(qi == 0)
    def _():
        ones = jnp.where((lane >= 3) & (lane < 6), 1.0, 0.0)
        for g in range(hpg):
            sel = _piece_selector(heads[g], -1.0, 0)
            vt_ref[g, dh:, :] = jnp.ones((ONES_ROWS, S), BF16)

            def body(i, carry, g=g, sel=sel):
                sl = pl.ds(pl.multiple_of(i * tq, tq), tq)
                kx_ref[g, sl, :dh] = k_ref[sl, cols[g]]
                ek = jnp.dot(c3_ref[sl, :], sel, preferred_element_type=F32) + ones
                kx_ref[g, sl, dh:] = ek.astype(BF16)
                vt_ref[g, :dh, sl] = v_ref[sl, cols[g]].astype(F32).T.astype(BF16)
                return carry
            lax.fori_loop(0, S // tq, body, 0)

    q0 = pl.multiple_of(qi * tq, tq)
    c3q = c3_ref[pl.ds(q0, tq), :]
    qx = []
    for g in range(hpg):
        eq = jnp.dot(c3q, _piece_selector(heads[g], 1.0, 3), preferred_element_type=F32)
        eq = eq + jnp.where(lane < 3, 1.0, 0.0)
        qx.append(jnp.concatenate([q_ref[:, cols[g]], eq.astype(BF16)], axis=1))

    def scores(g, j):
        k0 = pl.multiple_of(j * tk, tk)
        return lax.dot_general(kx_ref[g, pl.ds(k0, tk), :], qx[g], _NT, preferred_element_type=F32)

    def consume(g, j, st, m, diag):
        if diag is not None:
            kk = lax.broadcasted_iota(jnp.int32, st.shape, 0) + diag * tk
            qq = lax.broadcasted_iota(jnp.int32, st.shape, 1)
            st = jnp.where(kk <= qq, st, NEG_BIG)
        m_new = jnp.maximum(m, jnp.max(st, axis=0, keepdims=True))
        alpha = jnp.exp2(m - m_new)
        p = jnp.exp2(st - m_new).astype(BF16)
        k0 = pl.multiple_of(j * tk, tk)
        pv = jnp.dot(vt_ref[g, :, pl.ds(k0, tk)], p, preferred_element_type=F32)
        acc_ref[g] = alpha * acc_ref[g] + pv
        return m_new

    assert tq == 2 * tk
    for g in range(hpg):
        acc_ref[g] = jnp.zeros(acc_ref.shape[1:], F32)
        st_ref[g] = scores(g, 0)

    def pair(i, ms):
        out = []
        for g in range(hpg):
            st_a = st_ref[g]
            st_b = scores(g, 2 * i + 1)
            m = consume(g, 2 * i, st_a, ms[g], None)
            st_ref[g] = scores(g, 2 * i + 2)
            out.append(consume(g, 2 * i + 1, st_b, m, None))
        return tuple(out)

    ms = lax.fori_loop(0, qi, pair, tuple(jnp.full((1, tq), NEG_BIG, F32) for _ in range(hpg)))
    for g in range(hpg):
        st_a = st_ref[g]
        st_b = scores(g, 2 * qi + 1)
        m = consume(g, 2 * qi, st_a, ms[g], 0)
        consume(g, 2 * qi + 1, st_b, m, 1)
        ot = acc_ref[g, :dh, :] / acc_ref[g, dh:dh + 1, :]
        o_ref[:, cols[g]] = ot.T.astype(o_ref.dtype)


def _fox_attention(qkv, c3, *, batch, heads, tq=512, hpg=4):
    T = qkv.shape[0]
    S = T // batch
    dh = FOX_HEAD_DIM
    tq = min(tq, S)
    tk = tq // 2
    nq = S // tq
    ng = heads // hpg
    w = hpg * dh
    return pl.pallas_call(
        functools.partial(_fox_attn_kernel, tq=tq, tk=tk, hpg=hpg),
        grid=(batch, ng, nq),
        in_specs=[
            pl.BlockSpec((tq, w), lambda b, h, i: (b * nq + i, h)),
            pl.BlockSpec((S, w), lambda b, h, i: (b, ng + h)),
            pl.BlockSpec((S, w), lambda b, h, i: (b, 2 * ng + h)),
            pl.BlockSpec((S, 3 * LANES), lambda b, h, i: (b, 0)),
        ],
        out_specs=pl.BlockSpec((tq, w), lambda b, h, i: (b * nq + i, h)),
        out_shape=jax.ShapeDtypeStruct((T, heads * dh), BF16),
        scratch_shapes=[pltpu.VMEM((hpg, S, 2 * dh), BF16),
                        pltpu.VMEM((hpg, dh + ONES_ROWS, S), BF16),
                        pltpu.VMEM((hpg, dh + ONES_ROWS, tq), F32),
                        pltpu.VMEM((hpg, tk, tq), F32)],
        compiler_params=_params(("parallel", "parallel", "arbitrary")),
        name="fox_attention",
    )(qkv, qkv, qkv, c3)


def _matmul_res_kernel(a_ref, w_ref, r_ref, o_ref):
    o_ref[...] = r_ref[...] + jnp.dot(a_ref[...], w_ref[...], preferred_element_type=F32)


def _matmul_res(a, w, res, *, tm=1024, tn=512):
    T, K = a.shape
    N = w.shape[1]
    tm = _tile(T, tm)
    tn = _tile(N, tn)
    return pl.pallas_call(
        _matmul_res_kernel,
        grid=(T // tm, N // tn),
        in_specs=[pl.BlockSpec((tm, K), lambda i, j: (i, 0)),
                  pl.BlockSpec((K, tn), lambda i, j: (0, j)),
                  pl.BlockSpec((tm, tn), lambda i, j: (i, j))],
        out_specs=pl.BlockSpec((tm, tn), lambda i, j: (i, j)),
        out_shape=jax.ShapeDtypeStruct((T, N), F32),
        compiler_params=_params(("parallel", "arbitrary")),
        name="matmul_res",
    )(a, w, res)


def _mlp_kernel(x_ref, g_ref, wu_ref, wd_ref, *rest, has_final, row_chunk):
    if has_final:
        gf_ref, o_ref, hn_ref = rest
    else:
        o_ref, hn_ref = rest
    tm = x_ref.shape[0]
    f = pl.program_id(1)

    def rows(fn):
        def body(i, carry):
            fn(pl.ds(pl.multiple_of(i * row_chunk, row_chunk), row_chunk))
            return carry
        lax.fori_loop(0, tm // row_chunk, body, 0)

    @pl.when(f == 0)
    def _():
        def init(sl):
            x = x_ref[sl, :]
            hn_ref[sl, :] = _rms_rows(x, g_ref[...]).astype(BF16)
            o_ref[sl, :] = x
        rows(init)

    u = jnp.dot(hn_ref[...], wu_ref[...], preferred_element_type=F32)
    a = jnp.square(jnp.maximum(u, 0.0)).astype(BF16)
    o_ref[...] += jnp.dot(a, wd_ref[...], preferred_element_type=F32)

    if has_final:
        @pl.when(f == pl.num_programs(1) - 1)
        def _():
            def fin(sl):
                o_ref[sl, :] = _rms_rows(o_ref[sl, :], gf_ref[...])
            rows(fin)


def _mlp(x, g, wu, wd, g_final=None, *, tm=1024, tf=512):
    T, D = x.shape
    Fdim = wu.shape[1]
    tm = _tile(T, tm)
    tf = _tile(Fdim, tf)
    has_final = g_final is not None
    in_specs = [pl.BlockSpec((tm, D), lambda i, f: (i, 0)),
                pl.BlockSpec((1, D), lambda i, f: (0, 0)),
                pl.BlockSpec((D, tf), lambda i, f: (0, f)),
                pl.BlockSpec((tf, D), lambda i, f: (f, 0))]
    args = [x, g.reshape(1, D), wu, wd]
    if has_final:
        in_specs.append(pl.BlockSpec((1, D), lambda i, f: (0, 0)))
        args.append(g_final.reshape(1, D))
    return pl.pallas_call(
        functools.partial(_mlp_kernel, has_final=has_final, row_chunk=min(128, tm)),
        grid=(T // tm, Fdim // tf),
        in_specs=in_specs,
        out_specs=pl.BlockSpec((tm, D), lambda i, f: (i, 0)),
        out_shape=jax.ShapeDtypeStruct((T, D), F32),
        scratch_shapes=[pltpu.VMEM((tm, D), BF16)],
        compiler_params=_params(("parallel", "arbitrary")),
        name="mlp_final" if has_final else "mlp",
    )(*args)


def _retention_kernel(q_ref, k_ref, v_ref, g_ref, cos_ref, sin_ref, din_ref, qd_ref, kd_ref,
                      cd_ref, gn_ref, o_ref, state_ref, *, chunk, nchunks):
    @pl.when(pl.program_id(2) == 0)
    def _():
        state_ref[...] = jnp.zeros_like(state_ref)

    half = q_ref.shape[1] // 2
    for ci in range(nchunks):
        sl = slice(ci * chunk, (ci + 1) * chunk)
        cos = cos_ref[sl, :]
        sin = sin_ref[sl, :]

        def rot(x):
            x1, x2 = x[:, :half], x[:, half:]
            return jnp.concatenate([x1 * cos - x2 * sin, x1 * sin + x2 * cos], axis=-1)

        q = rot(q_ref[sl, :].astype(F32))
        k = rot(k_ref[sl, :].astype(F32))
        v = v_ref[sl, :]
        state = state_ref[...]

        inner = lax.dot_general(q.astype(BF16), k.astype(BF16), _NT,
                                preferred_element_type=F32) * din_ref[...]
        o = jnp.dot(inner.astype(BF16), v, preferred_element_type=F32)
        o = o + jnp.dot((q * qd_ref[...]).astype(BF16), state.astype(BF16),
                        preferred_element_type=F32)
        kd_t = (k * kd_ref[...]).T.astype(BF16)
        state_ref[...] = cd_ref[...] * state + jnp.dot(kd_t, v, preferred_element_type=F32)

        ms = jnp.mean(o * o, axis=-1, keepdims=True)
        on = o * lax.rsqrt(ms + RMS_EPS) * gn_ref[...]
        gate = g_ref[sl, :].astype(F32)
        gate = gate * jax.nn.sigmoid(gate)
        o_ref[sl, :] = (gate * on).astype(o_ref.dtype)


def _retention_tables(heads, chunk, seq):
    dk, dv = RET_QK_DIM, RET_V_DIM
    half = dk // 2
    pos = jnp.arange(seq, dtype=F32)
    inv = ROPE_BASE ** (-jnp.arange(half, dtype=F32) / half)
    ang = pos[:, None] * inv[None, :]
    log_gamma = jnp.log1p(-jnp.exp2(-5.0 - jnp.arange(heads, dtype=F32)))
    idx = jnp.arange(chunk, dtype=F32)
    diff = idx[:, None] - idx[None, :]
    decay_in = jnp.where(diff >= 0,
                         jnp.exp(log_gamma[:, None, None] * jnp.maximum(diff, 0.0)), 0.0)
    q_decay = jnp.exp(log_gamma[:, None] * (idx + 1.0))[:, :, None]
    k_decay = jnp.exp(log_gamma[:, None] * (chunk - 1.0 - idx))[:, :, None]
    chunk_decay = jnp.exp(log_gamma * chunk)[:, None, None]
    return (jnp.cos(ang), jnp.sin(ang), decay_in,
            jnp.broadcast_to(q_decay, (heads, chunk, dk)),
            jnp.broadcast_to(k_decay, (heads, chunk, dk)),
            jnp.broadcast_to(chunk_decay, (heads, 1, dv)))


def _retention(proj, gn, *, batch, heads, chunk=256, rows=512):
    T = proj.shape[0]
    S = T // batch
    dk, dv = RET_QK_DIM, RET_V_DIM
    chunk = min(chunk, S)
    rows = min(rows, S)
    ns = S // rows
    cos, sin, din, qd, kd, cd = _retention_tables(heads, chunk, S)
    vblk0 = 2 * heads * dk // dv
    return pl.pallas_call(
        functools.partial(_retention_kernel, chunk=chunk, nchunks=rows // chunk),
        grid=(batch, heads, ns),
        in_specs=[
            pl.BlockSpec((rows, dk), lambda b, h, s: (b * ns + s, h)),
            pl.BlockSpec((rows, dk), lambda b, h, s: (b * ns + s, heads + h)),
            pl.BlockSpec((rows, dv), lambda b, h, s: (b * ns + s, vblk0 + h)),
            pl.BlockSpec((rows, dv), lambda b, h, s: (b * ns + s, vblk0 + heads + h)),
            pl.BlockSpec((rows, dk // 2), lambda b, h, s: (s, 0)),
            pl.BlockSpec((rows, dk // 2), lambda b, h, s: (s, 0)),
            pl.BlockSpec((None, chunk, chunk), lambda b, h, s: (h, 0, 0)),
            pl.BlockSpec((None, chunk, dk), lambda b, h, s: (h, 0, 0)),
            pl.BlockSpec((None, chunk, dk), lambda b, h, s: (h, 0, 0)),
            pl.BlockSpec((None, 1, dv), lambda b, h, s: (h, 0, 0)),
            pl.BlockSpec((1, dv), lambda b, h, s: (0, h)),
        ],
        out_specs=pl.BlockSpec((rows, dv), lambda b, h, s: (b * ns + s, h)),
        out_shape=jax.ShapeDtypeStruct((T, heads * dv), BF16),
        scratch_shapes=[pltpu.VMEM((dk, dv), F32)],
        compiler_params=_params(("parallel", "parallel", "arbitrary")),
        name="retention",
    )(proj, proj, proj, proj, cos, sin, din, qd, kd, cd, gn.reshape(1, heads * dv))


def kernel(x, fox_norm, fox_wq, fox_wk, fox_wv, fox_wf, fox_bf, fox_wo, ret_norm, ret_wq, ret_wk,
           ret_wv, ret_wg, ret_gn, ret_wo, mlp_norm, mlp_up, mlp_down, final_norm):
    B, S, D = x.shape
    T = B * S
    fox_heads = D // FOX_HEAD_DIM
    ret_heads = D // RET_QK_DIM
    h = x.reshape(T, D)

    w_qkv = jnp.concatenate([fox_wq[0], fox_wk[0], fox_wv[0]], axis=1).astype(BF16)
    cs = jnp.concatenate([jnp.full((D,), LOG2E * FOX_HEAD_DIM ** -0.5, F32),
                          jnp.ones((2 * D,), F32)])
    wf_pad = jnp.pad(fox_wf[0], ((0, 0), (0, LANES - fox_heads))).astype(BF16)
    bf_pad = jnp.pad(fox_bf[0], (0, LANES - fox_heads)).reshape(1, LANES)
    qkv, z = _norm_matmul(h, fox_norm[0], w_qkv, cs, wf_pad)
    c3 = _gate_cumsum(z, bf_pad, B)
    attn = _fox_attention(qkv, c3, batch=B, heads=fox_heads)
    h = _matmul_res(attn, fox_wo[0].astype(BF16), h)
    h = _mlp(h, mlp_norm[0], mlp_up[0].astype(BF16), mlp_down[0].astype(BF16))

    w_proj = jnp.concatenate([ret_wq[0], ret_wk[0], ret_wv[0], ret_wg[0]], axis=1).astype(BF16)
    cs = jnp.concatenate([jnp.ones((D,), F32), jnp.full((D,), RET_QK_DIM ** -0.5, F32),
                          jnp.ones((2 * ret_heads * RET_V_DIM,), F32)])
    proj = _norm_matmul(h, ret_norm[0], w_proj, cs)
    y = _retention(proj, ret_gn[0], batch=B, heads=ret_heads)
    h = _matmul_res(y, ret_wo[0].astype(BF16), h)
    h = _mlp(h, mlp_norm[1], mlp_up[1].astype(BF16), mlp_down[1].astype(BF16), final_norm)
    return h.reshape(B, S, D)
```

```python
import functools
import math

import numpy as np
import jax
import jax.numpy as jnp
from jax import lax
from jax.experimental import pallas as pl
from jax.experimental.pallas import tpu as pltpu

F32 = jnp.float32
BF16 = jnp.bfloat16

RMS_EPS = 1e-6
ROPE_BASE = 10000.0
FOX_HEAD_DIM = 128
RET_QK_DIM = 256
RET_V_DIM = 512
LANES = 128
NEG_BIG = -1e30
LOG2E = math.log2(math.e)
V7X_VMEM_LIMIT = 56 * 1024 * 1024

_NT = (((1,), (1,)), ((), ()))


def _params(sem, vmem=V7X_VMEM_LIMIT):
    return pltpu.CompilerParams(dimension_semantics=sem, vmem_limit_bytes=vmem)


def _tile(n, target, align=LANES):
    if n <= target:
        return n
    t = (target // align) * align
    while n % t:
        t -= align
    return t


def _rms_rows(x, g):
    ms = jnp.mean(x * x, axis=-1, keepdims=True)
    return x * lax.rsqrt(ms + RMS_EPS) * g


def _cast_concat_kernel(*refs, starts, nblks):
    o_ref = refs[-1]
    p = pl.program_id(0)
    for w_ref, s0, nb in zip(refs[:-1], starts, nblks):
        @pl.when((p >= s0) & (p < s0 + nb))
        def _(w_ref=w_ref):
            o_ref[...] = w_ref[...].astype(o_ref.dtype)


def _cast_concat(ws, layer, *, bw=2048, tr=512):
    R = ws[0].shape[1]
    bw = min(bw, min(w.shape[2] for w in ws))
    tr = _tile(R, tr)
    nr = R // tr
    nblks = [w.shape[2] // bw for w in ws]
    starts = [sum(nblks[:i]) for i in range(len(ws))]

    def in_map(s0, nb):
        def index(p, r):
            local = p - s0
            row = jnp.where(local < 0, 0, jnp.where(local >= nb, nr - 1, r))
            return layer, row, jnp.clip(local, 0, nb - 1)
        return index

    return pl.pallas_call(
        functools.partial(_cast_concat_kernel, starts=starts, nblks=nblks),
        grid=(sum(nblks), nr),
        in_specs=[pl.BlockSpec((None, tr, bw), in_map(s0, nb)) for s0, nb in zip(starts, nblks)],
        out_specs=pl.BlockSpec((tr, bw), lambda p, r: (r, p)),
        out_shape=jax.ShapeDtypeStruct((R, sum(nblks) * bw), BF16),
        compiler_params=_params(("arbitrary", "arbitrary")),
        name="cast_concat",
    )(*ws)


def _norm_matmul_kernel(x_ref, g_ref, w_ref, cs_ref, *rest, has_gate, row_chunk):
    if has_gate:
        wz_ref, o_ref, z_ref, hn_ref = rest
    else:
        o_ref, hn_ref = rest
    tm = x_ref.shape[0]

    @pl.when(pl.program_id(1) == 0)
    def _():
        def body(i, carry):
            r0 = pl.multiple_of(i * row_chunk, row_chunk)
            y = _rms_rows(x_ref[pl.ds(r0, row_chunk), :], g_ref[...])
            hn_ref[pl.ds(r0, row_chunk), :] = y.astype(BF16)
            return carry
        lax.fori_loop(0, tm // row_chunk, body, 0)
        if has_gate:
            z_ref[...] = jnp.dot(hn_ref[...], wz_ref[...], preferred_element_type=F32)

    acc = jnp.dot(hn_ref[...], w_ref[...], preferred_element_type=F32)
    o_ref[...] = (acc * cs_ref[...]).astype(o_ref.dtype)


def _norm_matmul(x, g, w, colscale, wz=None, *, tm=1024, tn=1024):
    T, D = x.shape
    N = w.shape[1]
    tm = _tile(T, tm)
    tn = _tile(N, tn)
    has_gate = wz is not None
    in_specs = [
        pl.BlockSpec((tm, D), lambda i, j: (i, 0)),
        pl.BlockSpec((1, D), lambda i, j: (0, 0)),
        pl.BlockSpec((D, tn), lambda i, j: (0, j)),
        pl.BlockSpec((1, tn), lambda i, j: (0, j)),
    ]
    args = [x, g.reshape(1, D), w, colscale.reshape(1, N)]
    out_shape = [jax.ShapeDtypeStruct((T, N), BF16)]
    out_specs = [pl.BlockSpec((tm, tn), lambda i, j: (i, j))]
    if has_gate:
        in_specs.append(pl.BlockSpec((D, LANES), lambda i, j: (0, 0)))
        args.append(wz)
        out_shape.append(jax.ShapeDtypeStruct((T, LANES), F32))
        out_specs.append(pl.BlockSpec((tm, LANES), lambda i, j: (i, 0)))
    res = pl.pallas_call(
        functools.partial(_norm_matmul_kernel, has_gate=has_gate, row_chunk=min(128, tm)),
        grid=(T // tm, N // tn),
        in_specs=in_specs,
        out_specs=out_specs,
        out_shape=out_shape,
        scratch_shapes=[pltpu.VMEM((tm, D), BF16)],
        compiler_params=_params(("parallel", "arbitrary")),
        name="norm_matmul_gate" if has_gate else "norm_matmul",
    )(*args)
    return res if has_gate else res[0]


def _gate_cumsum_kernel(z_ref, b_ref, cp_ref, *, heads):
    z = z_ref[...] + b_ref[...]
    c = jnp.minimum(z, 0.0) - jnp.log1p(jnp.exp(-jnp.abs(z)))
    S = c.shape[0]
    row = lax.broadcasted_iota(jnp.int32, c.shape, 0)
    shift = 1
    while shift < S:
        c = c + jnp.where(row >= shift, pltpu.roll(c, shift, axis=0), 0.0)
        shift *= 2
    c = c * LOG2E
    hi = c.astype(BF16).astype(F32)
    r1 = c - hi
    mid = r1.astype(BF16).astype(F32)
    lo = r1 - mid
    lane = lax.broadcasted_iota(jnp.int32, c.shape, 1)
    packed = jnp.where(lane < heads, hi,
                       jnp.where(lane < 2 * heads, pltpu.roll(mid, heads, axis=1),
                                 jnp.where(lane < 3 * heads, pltpu.roll(lo, 2 * heads, axis=1), 0.0)))
    cp_ref[...] = packed.astype(BF16)


def _gate_cumsum(z, b_pad, batch, heads):
    T = z.shape[0]
    S = T // batch
    assert 3 * heads <= LANES
    return pl.pallas_call(
        functools.partial(_gate_cumsum_kernel, heads=heads),
        grid=(batch,),
        in_specs=[pl.BlockSpec((S, LANES), lambda b: (b, 0)),
                  pl.BlockSpec((1, LANES), lambda b: (0, 0))],
        out_specs=pl.BlockSpec((S, LANES), lambda b: (b, 0)),
        out_shape=jax.ShapeDtypeStruct((T, LANES), BF16),
        compiler_params=_params(("parallel",)),
        name="gate_cumsum",
    )(z, b_pad)


ONES_ROWS = 16
N_PIECES = 3


def _piece_selector(head0, hpg, heads, sign, lane0):
    r = lax.broadcasted_iota(jnp.int32, (LANES, hpg * LANES), 0)
    col = lax.broadcasted_iota(jnp.int32, (LANES, hpg * LANES), 1)
    g = col // LANES
    p = col % LANES - lane0
    hit = (p >= 0) & (p < N_PIECES) & (r == head0 + g + heads * p)
    return jnp.where(hit, sign, 0.0).astype(BF16)


def _fox_attn_kernel(q_ref, k_ref, v_ref, cp_ref, o_ref, kx_ref, vt_ref, acc_ref, st_ref,
                     *, tq, tk, hpg, heads):
    hg = pl.program_id(1)
    qi = pl.program_id(2)
    S = k_ref.shape[0]
    dh = FOX_HEAD_DIM
    lane = lax.broadcasted_iota(jnp.int32, (1, hpg * LANES), 1) % LANES
    cols = [slice(g * dh, (g + 1) * dh) for g in range(hpg)]

    @pl.when(qi == 0)
    def _():
        sel = _piece_selector(hg * hpg, hpg, heads, -1.0, 0)
        ones = jnp.where((lane >= N_PIECES) & (lane < 2 * N_PIECES), 1.0, 0.0)
        for g in range(hpg):
            vt_ref[g, dh:, :] = jnp.ones((ONES_ROWS, S), BF16)

        def body(i, carry):
            sl = pl.ds(pl.multiple_of(i * tq, tq), tq)
            ek = (jnp.dot(cp_ref[sl, :], sel, preferred_element_type=F32) + ones).astype(BF16)
            for g in range(hpg):
                kx_ref[g, sl, :dh] = k_ref[sl, cols[g]]
                kx_ref[g, sl, dh:] = ek[:, cols[g]]
                vt_ref[g, :dh, sl] = v_ref[sl, cols[g]].astype(F32).T.astype(BF16)
            return carry
        lax.fori_loop(0, S // tq, body, 0)

    q0 = pl.multiple_of(qi * tq, tq)
    sel_q = _piece_selector(hg * hpg, hpg, heads, 1.0, N_PIECES)
    cref = jnp.dot(cp_ref[pl.ds(q0, ONES_ROWS), :], sel_q, preferred_element_type=F32)[0:1, :]
    eq = (cref + jnp.where(lane < N_PIECES, 1.0, 0.0)).astype(BF16)
    qx = [jnp.concatenate([q_ref[:, cols[g]], jnp.broadcast_to(eq[:, cols[g]], (tq, dh))], axis=1)
          for g in range(hpg)]

    def scores(g, j):
        k0 = pl.multiple_of(j * tk, tk)
        return lax.dot_general(kx_ref[g, pl.ds(k0, tk), :], qx[g], _NT, preferred_element_type=F32)

    def consume(g, j, st, m, diag):
        if diag is not None:
            kk = lax.broadcasted_iota(jnp.int32, st.shape, 0) + diag * tk
            qq = lax.broadcasted_iota(jnp.int32, st.shape, 1)
            st = jnp.where(kk <= qq, st, NEG_BIG)
        m_new = jnp.maximum(m, jnp.max(st, axis=0, keepdims=True))
        alpha = jnp.exp2(m - m_new)
        p = jnp.exp2(st - m_new).astype(BF16)
        k0 = pl.multiple_of(j * tk, tk)
        pv = jnp.dot(vt_ref[g, :, pl.ds(k0, tk)], p, preferred_element_type=F32)
        acc_ref[g] = alpha * acc_ref[g] + pv
        return m_new

    assert tq == 2 * tk
    for g in range(hpg):
        acc_ref[g] = jnp.zeros(acc_ref.shape[1:], F32)
        st_ref[g] = scores(g, 0)

    def pair(i, ms):
        out = []
        for g in range(hpg):
            st_a = st_ref[g]
            st_b = scores(g, 2 * i + 1)
            m = consume(g, 2 * i, st_a, ms[g], None)
            st_ref[g] = scores(g, 2 * i + 2)
            out.append(consume(g, 2 * i + 1, st_b, m, None))
        return tuple(out)

    ms = lax.fori_loop(0, qi, pair, tuple(jnp.full((1, tq), NEG_BIG, F32) for _ in range(hpg)))
    for g in range(hpg):
        st_a = st_ref[g]
        st_b = scores(g, 2 * qi + 1)
        m = consume(g, 2 * qi, st_a, ms[g], 0)
        consume(g, 2 * qi + 1, st_b, m, 1)
        ot = acc_ref[g, :dh, :] / acc_ref[g, dh:dh + 1, :]
        o_ref[:, cols[g]] = ot.T.astype(o_ref.dtype)


def _fox_attention(qkv, cp, *, batch, heads, tq=512, hpg=4):
    T = qkv.shape[0]
    S = T // batch
    dh = FOX_HEAD_DIM
    tq = min(tq, S)
    tk = tq // 2
    nq = S // tq
    ng = heads // hpg
    w = hpg * dh
    return pl.pallas_call(
        functools.partial(_fox_attn_kernel, tq=tq, tk=tk, hpg=hpg, heads=heads),
        grid=(batch, ng, nq),
        in_specs=[
            pl.BlockSpec((tq, w), lambda b, h, i: (b * nq + i, h)),
            pl.BlockSpec((S, w), lambda b, h, i: (b, ng + h)),
            pl.BlockSpec((S, w), lambda b, h, i: (b, 2 * ng + h)),
            pl.BlockSpec((S, LANES), lambda b, h, i: (b, 0)),
        ],
        out_specs=pl.BlockSpec((tq, w), lambda b, h, i: (b * nq + i, h)),
        out_shape=jax.ShapeDtypeStruct((T, heads * dh), BF16),
        scratch_shapes=[pltpu.VMEM((hpg, S, 2 * dh), BF16),
                        pltpu.VMEM((hpg, dh + ONES_ROWS, S), BF16),
                        pltpu.VMEM((hpg, dh + ONES_ROWS, tq), F32),
                        pltpu.VMEM((hpg, tk, tq), F32)],
        compiler_params=_params(("parallel", "parallel", "arbitrary")),
        name="fox_attention",
    )(qkv, qkv, qkv, cp)


def _matmul_res_kernel(a_ref, w_ref, r_ref, o_ref):
    o_ref[...] = r_ref[...] + jnp.dot(a_ref[...], w_ref[...], preferred_element_type=F32)


def _matmul_res(a, w, res, *, tm=1024, tn=512):
    T, K = a.shape
    N = w.shape[1]
    tm = _tile(T, tm)
    tn = _tile(N, tn)
    return pl.pallas_call(
        _matmul_res_kernel,
        grid=(T // tm, N // tn),
        in_specs=[pl.BlockSpec((tm, K), lambda i, j: (i, 0)),
                  pl.BlockSpec((K, tn), lambda i, j: (0, j)),
                  pl.BlockSpec((tm, tn), lambda i, j: (i, j))],
        out_specs=pl.BlockSpec((tm, tn), lambda i, j: (i, j)),
        out_shape=jax.ShapeDtypeStruct((T, N), F32),
        compiler_params=_params(("parallel", "arbitrary")),
        name="matmul_res",
    )(a, w, res)


def _mlp_kernel(x_ref, g_ref, wu_ref, wd_ref, *rest, has_final, row_chunk):
    if has_final:
        gf_ref, o_ref, hn_ref = rest
    else:
        o_ref, hn_ref = rest
    tm = x_ref.shape[0]
    f = pl.program_id(1)

    def rows(fn):
        def body(i, carry):
            fn(pl.ds(pl.multiple_of(i * row_chunk, row_chunk), row_chunk))
            return carry
        lax.fori_loop(0, tm // row_chunk, body, 0)

    @pl.when(f == 0)
    def _():
        def init(sl):
            x = x_ref[sl, :]
            hn_ref[sl, :] = _rms_rows(x, g_ref[...]).astype(BF16)
            o_ref[sl, :] = x
        rows(init)

    u = jnp.dot(hn_ref[...], wu_ref[...], preferred_element_type=F32)
    a = jnp.square(jnp.maximum(u, 0.0)).astype(BF16)
    o_ref[...] += jnp.dot(a, wd_ref[...], preferred_element_type=F32)

    if has_final:
        @pl.when(f == pl.num_programs(1) - 1)
        def _():
            def fin(sl):
                o_ref[sl, :] = _rms_rows(o_ref[sl, :], gf_ref[...])
            rows(fin)


def _mlp(x, g, wu, wd, g_final=None, *, tm=1024, tf=512):
    T, D = x.shape
    Fdim = wu.shape[1]
    tm = _tile(T, tm)
    tf = _tile(Fdim, tf)
    has_final = g_final is not None
    in_specs = [pl.BlockSpec((tm, D), lambda i, f: (i, 0)),
                pl.BlockSpec((1, D), lambda i, f: (0, 0)),
                pl.BlockSpec((D, tf), lambda i, f: (0, f)),
                pl.BlockSpec((tf, D), lambda i, f: (f, 0))]
    args = [x, g.reshape(1, D), wu, wd]
    if has_final:
        in_specs.append(pl.BlockSpec((1, D), lambda i, f: (0, 0)))
        args.append(g_final.reshape(1, D))
    return pl.pallas_call(
        functools.partial(_mlp_kernel, has_final=has_final, row_chunk=min(128, tm)),
        grid=(T // tm, Fdim // tf),
        in_specs=in_specs,
        out_specs=pl.BlockSpec((tm, D), lambda i, f: (i, 0)),
        out_shape=jax.ShapeDtypeStruct((T, D), F32),
        scratch_shapes=[pltpu.VMEM((tm, D), BF16)],
        compiler_params=_params(("parallel", "arbitrary")),
        name="mlp_final" if has_final else "mlp",
    )(*args)


def _retention_kernel(q_ref, k_ref, v_ref, g_ref, cos_ref, sin_ref, din_ref, qd_ref, kd_ref,
                      cd_ref, gn_ref, o_ref, state_ref, *, chunk, nchunks):
    @pl.when(pl.program_id(2) == 0)
    def _():
        state_ref[...] = jnp.zeros_like(state_ref)

    half = q_ref.shape[1] // 2
    for ci in range(nchunks):
        sl = slice(ci * chunk, (ci + 1) * chunk)
        cos = cos_ref[sl, :]
        sin = sin_ref[sl, :]

        def rot(x):
            x1, x2 = x[:, :half], x[:, half:]
            return jnp.concatenate([x1 * cos - x2 * sin, x1 * sin + x2 * cos], axis=-1)

        q = rot(q_ref[sl, :].astype(F32))
        k = rot(k_ref[sl, :].astype(F32))
        v = v_ref[sl, :]
        state = state_ref[...]

        inner = lax.dot_general(q.astype(BF16), k.astype(BF16), _NT,
                                preferred_element_type=F32) * din_ref[...]
        o = jnp.dot(inner.astype(BF16), v, preferred_element_type=F32)
        o = o + jnp.dot((q * qd_ref[...]).astype(BF16), state.astype(BF16),
                        preferred_element_type=F32)
        kd_t = (k * kd_ref[...]).T.astype(BF16)
        state_ref[...] = cd_ref[...] * state + jnp.dot(kd_t, v, preferred_element_type=F32)

        ms = jnp.mean(o * o, axis=-1, keepdims=True)
        on = o * lax.rsqrt(ms + RMS_EPS) * gn_ref[...]
        gate = g_ref[sl, :].astype(F32)
        gate = gate * jax.nn.sigmoid(gate)
        o_ref[sl, :] = (gate * on).astype(o_ref.dtype)


def _retention_tables(heads, chunk, seq):
    dk, dv = RET_QK_DIM, RET_V_DIM
    half = dk // 2
    pos = np.arange(seq, dtype=np.float64)
    inv = ROPE_BASE ** (-np.arange(half, dtype=np.float64) / half)
    ang = pos[:, None] * inv[None, :]
    log_gamma = np.log1p(-np.exp2(-5.0 - np.arange(heads, dtype=np.float64)))
    idx = np.arange(chunk, dtype=np.float64)
    diff = idx[:, None] - idx[None, :]
    decay_in = np.where(diff >= 0,
                        np.exp(log_gamma[:, None, None] * np.maximum(diff, 0.0)), 0.0)
    q_decay = np.exp(log_gamma[:, None] * (idx + 1.0))[:, :, None]
    k_decay = np.exp(log_gamma[:, None] * (chunk - 1.0 - idx))[:, :, None]
    chunk_decay = np.exp(log_gamma * chunk)[:, None, None]
    tables = (np.cos(ang), np.sin(ang), decay_in,
              np.broadcast_to(q_decay, (heads, chunk, dk)),
              np.broadcast_to(k_decay, (heads, chunk, dk)),
              np.broadcast_to(chunk_decay, (heads, 1, dv)))
    return tuple(jnp.asarray(t, dtype=F32) for t in tables)


def _retention(proj, gn, *, batch, heads, chunk=256, rows=512):
    T = proj.shape[0]
    S = T // batch
    dk, dv = RET_QK_DIM, RET_V_DIM
    chunk = min(chunk, S)
    rows = min(rows, S)
    ns = S // rows
    cos, sin, din, qd, kd, cd = _retention_tables(heads, chunk, S)
    vblk0 = 2 * heads * dk // dv
    return pl.pallas_call(
        functools.partial(_retention_kernel, chunk=chunk, nchunks=rows // chunk),
        grid=(batch, heads, ns),
        in_specs=[
            pl.BlockSpec((rows, dk), lambda b, h, s: (b * ns + s, h)),
            pl.BlockSpec((rows, dk), lambda b, h, s: (b * ns + s, heads + h)),
            pl.BlockSpec((rows, dv), lambda b, h, s: (b * ns + s, vblk0 + h)),
            pl.BlockSpec((rows, dv), lambda b, h, s: (b * ns + s, vblk0 + heads + h)),
            pl.BlockSpec((rows, dk // 2), lambda b, h, s: (s, 0)),
            pl.BlockSpec((rows, dk // 2), lambda b, h, s: (s, 0)),
            pl.BlockSpec((None, chunk, chunk), lambda b, h, s: (h, 0, 0)),
            pl.BlockSpec((None, chunk, dk), lambda b, h, s: (h, 0, 0)),
            pl.BlockSpec((None, chunk, dk), lambda b, h, s: (h, 0, 0)),
            pl.BlockSpec((None, 1, dv), lambda b, h, s: (h, 0, 0)),
            pl.BlockSpec((1, dv), lambda b, h, s: (0, h)),
        ],
        out_specs=pl.BlockSpec((rows, dv), lambda b, h, s: (b * ns + s, h)),
        out_shape=jax.ShapeDtypeStruct((T, heads * dv), BF16),
        scratch_shapes=[pltpu.VMEM((dk, dv), F32)],
        compiler_params=_params(("parallel", "parallel", "arbitrary")),
        name="retention",
    )(proj, proj, proj, proj, cos, sin, din, qd, kd, cd, gn.reshape(1, heads * dv))


def kernel(x, fox_norm, fox_wq, fox_wk, fox_wv, fox_wf, fox_bf, fox_wo, ret_norm, ret_wq, ret_wk,
           ret_wv, ret_wg, ret_gn, ret_wo, mlp_norm, mlp_up, mlp_down, final_norm):
    B, S, D = x.shape
    T = B * S
    fox_heads = D // FOX_HEAD_DIM
    ret_heads = D // RET_QK_DIM
    h = x.reshape(T, D)

    w_qkv = _cast_concat([fox_wq, fox_wk, fox_wv], 0)
    cs = jnp.concatenate([jnp.full((D,), LOG2E * FOX_HEAD_DIM ** -0.5, F32),
                          jnp.ones((2 * D,), F32)])
    wf_pad = jnp.pad(fox_wf[0], ((0, 0), (0, LANES - fox_heads))).astype(BF16)
    bf_pad = jnp.pad(fox_bf[0], (0, LANES - fox_heads)).reshape(1, LANES)
    qkv, z = _norm_matmul(h, fox_norm[0], w_qkv, cs, wf_pad)
    cp = _gate_cumsum(z, bf_pad, B, fox_heads)
    attn = _fox_attention(qkv, cp, batch=B, heads=fox_heads)
    h = _matmul_res(attn, _cast_concat([fox_wo], 0), h)
    h = _mlp(h, mlp_norm[0], _cast_concat([mlp_up], 0), _cast_concat([mlp_down], 0))

    w_proj = _cast_concat([ret_wq, ret_wk, ret_wv, ret_wg], 0)
    cs = jnp.concatenate([jnp.ones((D,), F32), jnp.full((D,), RET_QK_DIM ** -0.5, F32),
                          jnp.ones((2 * ret_heads * RET_V_DIM,), F32)])
    proj = _norm_matmul(h, ret_norm[0], w_proj, cs)
    y = _retention(proj, ret_gn[0], batch=B, heads=ret_heads)
    h = _matmul_res(y, _cast_concat([ret_wo], 0), h)
    h = _mlp(h, mlp_norm[1], _cast_concat([mlp_up], 1), _cast_concat([mlp_down], 1), final_norm)
    return h.reshape(B, S, D)
```

```python
import functools
import math

import numpy as np
import jax
import jax.numpy as jnp
from jax import lax
from jax.experimental import pallas as pl
from jax.experimental.pallas import tpu as pltpu

F32 = jnp.float32
BF16 = jnp.bfloat16

RMS_EPS = 1e-6
ROPE_BASE = 10000.0
FOX_HEAD_DIM = 128
RET_QK_DIM = 256
RET_V_DIM = 512
LANES = 128
NEG_BIG = -1e30
LOG2E = math.log2(math.e)
V7X_VMEM_LIMIT = 56 * 1024 * 1024

_NT = (((1,), (1,)), ((), ()))


def _params(sem, vmem=V7X_VMEM_LIMIT):
    return pltpu.CompilerParams(dimension_semantics=sem, vmem_limit_bytes=vmem)


def _tile(n, target, align=LANES):
    if n <= target:
        return n
    t = (target // align) * align
    while n % t:
        t -= align
    return t


def _rms_rows(x, g):
    ms = jnp.mean(x * x, axis=-1, keepdims=True)
    return x * lax.rsqrt(ms + RMS_EPS) * g


def _cast_concat_kernel(*refs, starts, nblks):
    o_ref = refs[-1]
    p = pl.program_id(0)
    for w_ref, s0, nb in zip(refs[:-1], starts, nblks):
        @pl.when((p >= s0) & (p < s0 + nb))
        def _(w_ref=w_ref):
            o_ref[...] = w_ref[...].astype(o_ref.dtype)


def _cast_concat(ws, layer, *, bw=2048, tr=512):
    R = ws[0].shape[1]
    bw = min(bw, min(w.shape[2] for w in ws))
    tr = _tile(R, tr)
    nr = R // tr
    nblks = [w.shape[2] // bw for w in ws]
    starts = [sum(nblks[:i]) for i in range(len(ws))]

    def in_map(s0, nb):
        def index(p, r):
            local = p - s0
            row = jnp.where(local < 0, 0, jnp.where(local >= nb, nr - 1, r))
            return layer, row, jnp.clip(local, 0, nb - 1)
        return index

    return pl.pallas_call(
        functools.partial(_cast_concat_kernel, starts=starts, nblks=nblks),
        grid=(sum(nblks), nr),
        in_specs=[pl.BlockSpec((None, tr, bw), in_map(s0, nb)) for s0, nb in zip(starts, nblks)],
        out_specs=pl.BlockSpec((tr, bw), lambda p, r: (r, p)),
        out_shape=jax.ShapeDtypeStruct((R, sum(nblks) * bw), BF16),
        compiler_params=_params(("arbitrary", "arbitrary")),
        name="cast_concat",
    )(*ws)


def _norm_matmul_kernel(x_ref, g_ref, w_ref, cs_ref, *rest, has_gate, row_chunk):
    if has_gate:
        wz_ref, o_ref, z_ref, hn_ref = rest
    else:
        o_ref, hn_ref = rest
    tm = x_ref.shape[0]

    @pl.when(pl.program_id(1) == 0)
    def _():
        def body(i, carry):
            r0 = pl.multiple_of(i * row_chunk, row_chunk)
            y = _rms_rows(x_ref[pl.ds(r0, row_chunk), :], g_ref[...])
            hn_ref[pl.ds(r0, row_chunk), :] = y.astype(BF16)
            return carry
        lax.fori_loop(0, tm // row_chunk, body, 0)
        if has_gate:
            z_ref[...] = jnp.dot(hn_ref[...], wz_ref[...], preferred_element_type=F32)

    acc = jnp.dot(hn_ref[...], w_ref[...], preferred_element_type=F32)
    o_ref[...] = (acc * cs_ref[...]).astype(o_ref.dtype)


def _norm_matmul(x, g, w, colscale, wz=None, *, tm=1024, tn=1024):
    T, D = x.shape
    N = w.shape[1]
    tm = _tile(T, tm)
    tn = _tile(N, tn)
    has_gate = wz is not None
    in_specs = [
        pl.BlockSpec((tm, D), lambda i, j: (i, 0)),
        pl.BlockSpec((1, D), lambda i, j: (0, 0)),
        pl.BlockSpec((D, tn), lambda i, j: (0, j)),
        pl.BlockSpec((1, tn), lambda i, j: (0, j)),
    ]
    args = [x, g.reshape(1, D), w, colscale.reshape(1, N)]
    out_shape = [jax.ShapeDtypeStruct((T, N), BF16)]
    out_specs = [pl.BlockSpec((tm, tn), lambda i, j: (i, j))]
    if has_gate:
        in_specs.append(pl.BlockSpec((D, LANES), lambda i, j: (0, 0)))
        args.append(wz)
        out_shape.append(jax.ShapeDtypeStruct((T, LANES), F32))
        out_specs.append(pl.BlockSpec((tm, LANES), lambda i, j: (i, 0)))
    res = pl.pallas_call(
        functools.partial(_norm_matmul_kernel, has_gate=has_gate, row_chunk=min(128, tm)),
        grid=(T // tm, N // tn),
        in_specs=in_specs,
        out_specs=out_specs,
        out_shape=out_shape,
        scratch_shapes=[pltpu.VMEM((tm, D), BF16)],
        compiler_params=_params(("parallel", "arbitrary")),
        name="norm_matmul_gate" if has_gate else "norm_matmul",
    )(*args)
    return res if has_gate else res[0]


def _gate_cumsum_kernel(z_ref, b_ref, cp_ref, *, heads):
    z = z_ref[...] + b_ref[...]
    c = jnp.minimum(z, 0.0) - jnp.log1p(jnp.exp(-jnp.abs(z)))
    S = c.shape[0]
    row = lax.broadcasted_iota(jnp.int32, c.shape, 0)
    shift = 1
    while shift < S:
        c = c + jnp.where(row >= shift, pltpu.roll(c, shift, axis=0), 0.0)
        shift *= 2
    c = c * LOG2E
    hi = c.astype(BF16).astype(F32)
    r1 = c - hi
    mid = r1.astype(BF16).astype(F32)
    lo = r1 - mid
    lane = lax.broadcasted_iota(jnp.int32, c.shape, 1)
    packed = jnp.where(lane < heads, hi,
                       jnp.where(lane < 2 * heads, pltpu.roll(mid, heads, axis=1),
                                 jnp.where(lane < 3 * heads, pltpu.roll(lo, 2 * heads, axis=1), 0.0)))
    cp_ref[...] = packed.astype(BF16)


def _gate_cumsum(z, b_pad, batch, heads):
    T = z.shape[0]
    S = T // batch
    assert 3 * heads <= LANES
    return pl.pallas_call(
        functools.partial(_gate_cumsum_kernel, heads=heads),
        grid=(batch,),
        in_specs=[pl.BlockSpec((S, LANES), lambda b: (b, 0)),
                  pl.BlockSpec((1, LANES), lambda b: (0, 0))],
        out_specs=pl.BlockSpec((S, LANES), lambda b: (b, 0)),
        out_shape=jax.ShapeDtypeStruct((T, LANES), BF16),
        compiler_params=_params(("parallel",)),
        name="gate_cumsum",
    )(z, b_pad)


ONES_ROWS = 16
N_PIECES = 3


def _piece_selector(head0, hpg, heads, sign, lane0):
    r = lax.broadcasted_iota(jnp.int32, (LANES, hpg * LANES), 0)
    col = lax.broadcasted_iota(jnp.int32, (LANES, hpg * LANES), 1)
    g = col // LANES
    p = col % LANES - lane0
    hit = (p >= 0) & (p < N_PIECES) & (r == head0 + g + heads * p)
    return jnp.where(hit, sign, 0.0).astype(BF16)


def _fox_attn_kernel(q_ref, k_ref, v_ref, cp_ref, o_ref, kx_ref, vt_ref, acc_ref, st_ref, p_ref,
                     *, tq, tk, hpg, heads):
    hg = pl.program_id(1)
    qi = pl.program_id(2)
    S = k_ref.shape[0]
    dh = FOX_HEAD_DIM
    lane = lax.broadcasted_iota(jnp.int32, (1, hpg * LANES), 1) % LANES
    cols = [slice(g * dh, (g + 1) * dh) for g in range(hpg)]

    @pl.when(qi == 0)
    def _():
        sel = _piece_selector(hg * hpg, hpg, heads, -1.0, 0)
        ones = jnp.where((lane >= N_PIECES) & (lane < 2 * N_PIECES), 1.0, 0.0)
        for g in range(hpg):
            vt_ref[g, dh:, :] = jnp.ones((ONES_ROWS, S), BF16)

        def body(i, carry):
            sl = pl.ds(pl.multiple_of(i * tq, tq), tq)
            ek = (jnp.dot(cp_ref[sl, :], sel, preferred_element_type=F32) + ones).astype(BF16)
            for g in range(hpg):
                kx_ref[g, sl, :dh] = k_ref[sl, cols[g]]
                kx_ref[g, sl, dh:] = ek[:, cols[g]]
                vt_ref[g, :dh, sl] = v_ref[sl, cols[g]].astype(F32).T.astype(BF16)
            return carry
        lax.fori_loop(0, S // tq, body, 0)

    q0 = pl.multiple_of(qi * tq, tq)
    sel_q = _piece_selector(hg * hpg, hpg, heads, 1.0, N_PIECES)
    cref = jnp.dot(cp_ref[pl.ds(q0, ONES_ROWS), :], sel_q, preferred_element_type=F32)[0:1, :]
    eq = (cref + jnp.where(lane < N_PIECES, 1.0, 0.0)).astype(BF16)
    qx = [jnp.concatenate([q_ref[:, cols[g]], jnp.broadcast_to(eq[:, cols[g]], (tq, dh))], axis=1)
          for g in range(hpg)]

    def scores(g, j, diag):
        k0 = pl.multiple_of(j * tk, tk)
        st = lax.dot_general(kx_ref[g, pl.ds(k0, tk), :], qx[g], _NT, preferred_element_type=F32)
        if diag is not None:
            kk = lax.broadcasted_iota(jnp.int32, st.shape, 0) + diag * tk
            qq = lax.broadcasted_iota(jnp.int32, st.shape, 1)
            st = jnp.where(kk <= qq, st, NEG_BIG)
        return st, jnp.max(st, axis=0, keepdims=True)

    def pv_update(g, j, slot, alpha):
        k0 = pl.multiple_of(j * tk, tk)
        pv = jnp.dot(vt_ref[g, :, pl.ds(k0, tk)], p_ref[g, slot], preferred_element_type=F32)
        acc_ref[g] = alpha * acc_ref[g] + pv

    def stage(g, slot, carry, prev_blk, next_blk, next_diag=None):
        m, alpha_prev, cmax = carry
        if prev_blk is not None:
            pv_update(g, prev_blk, 1 - slot, alpha_prev)
        st_next, cmax_next = scores(g, next_blk, next_diag)
        st_ref[g, 1 - slot] = st_next
        m_new = jnp.maximum(m, cmax)
        p_ref[g, slot] = jnp.exp2(st_ref[g, slot] - m_new).astype(BF16)
        return m_new, jnp.exp2(m - m_new), cmax_next

    assert tq == 2 * tk
    d0 = 2 * qi
    carries = []
    for g in range(hpg):
        acc_ref[g] = jnp.zeros(acc_ref.shape[1:], F32)
        st, cmax = scores(g, d0, 0)
        st_ref[g, 0] = st
        c = (jnp.full((1, tq), NEG_BIG, F32), jnp.ones((1, tq), F32), cmax)
        c = stage(g, 0, c, None, d0 + 1, 1)
        carries.append(stage(g, 1, c, d0, 0))

    def pair(i, carries):
        out = []
        for g in range(hpg):
            c = stage(g, 0, carries[g], jnp.where(i == 0, d0 + 1, 2 * i - 1), 2 * i + 1)
            out.append(stage(g, 1, c, 2 * i, 2 * i + 2))
        return tuple(out)

    carries = lax.fori_loop(0, qi, pair, tuple(carries))
    last_blk = jnp.where(qi == 0, d0 + 1, d0 - 1)
    for g in range(hpg):
        pv_update(g, last_blk, 1, carries[g][1])
        ot = acc_ref[g, :dh, :] / acc_ref[g, dh:dh + 1, :]
        o_ref[:, cols[g]] = ot.T.astype(o_ref.dtype)


def _fox_attention(qkv, cp, *, batch, heads, tq=512, hpg=4):
    T = qkv.shape[0]
    S = T // batch
    dh = FOX_HEAD_DIM
    tq = min(tq, S)
    tk = tq // 2
    nq = S // tq
    ng = heads // hpg
    w = hpg * dh
    return pl.pallas_call(
        functools.partial(_fox_attn_kernel, tq=tq, tk=tk, hpg=hpg, heads=heads),
        grid=(batch, ng, nq),
        in_specs=[
            pl.BlockSpec((tq, w), lambda b, h, i: (b * nq + i, h)),
            pl.BlockSpec((S, w), lambda b, h, i: (b, ng + h)),
            pl.BlockSpec((S, w), lambda b, h, i: (b, 2 * ng + h)),
            pl.BlockSpec((S, LANES), lambda b, h, i: (b, 0)),
        ],
        out_specs=pl.BlockSpec((tq, w), lambda b, h, i: (b * nq + i, h)),
        out_shape=jax.ShapeDtypeStruct((T, heads * dh), BF16),
        scratch_shapes=[pltpu.VMEM((hpg, S, 2 * dh), BF16),
                        pltpu.VMEM((hpg, dh + ONES_ROWS, S), BF16),
                        pltpu.VMEM((hpg, dh + ONES_ROWS, tq), F32),
                        pltpu.VMEM((hpg, 2, tk, tq), F32),
                        pltpu.VMEM((hpg, 2, tk, tq), BF16)],
        compiler_params=_params(("parallel", "parallel", "arbitrary")),
        name="fox_attention",
    )(qkv, qkv, qkv, cp)


def _matmul_res_kernel(a_ref, w_ref, r_ref, o_ref):
    o_ref[...] = r_ref[...] + jnp.dot(a_ref[...], w_ref[...], preferred_element_type=F32)


def _matmul_res(a, w, res, *, tm=1024, tn=512):
    T, K = a.shape
    N = w.shape[1]
    tm = _tile(T, tm)
    tn = _tile(N, tn)
    return pl.pallas_call(
        _matmul_res_kernel,
        grid=(T // tm, N // tn),
        in_specs=[pl.BlockSpec((tm, K), lambda i, j: (i, 0)),
                  pl.BlockSpec((K, tn), lambda i, j: (0, j)),
                  pl.BlockSpec((tm, tn), lambda i, j: (i, j))],
        out_specs=pl.BlockSpec((tm, tn), lambda i, j: (i, j)),
        out_shape=jax.ShapeDtypeStruct((T, N), F32),
        compiler_params=_params(("parallel", "arbitrary")),
        name="matmul_res",
    )(a, w, res)


def _mlp_kernel(x_ref, g_ref, wu_ref, wd_ref, *rest, has_final, row_chunk):
    if has_final:
        gf_ref, o_ref, hn_ref = rest
    else:
        o_ref, hn_ref = rest
    tm = x_ref.shape[0]
    f = pl.program_id(1)

    def rows(fn):
        def body(i, carry):
            fn(pl.ds(pl.multiple_of(i * row_chunk, row_chunk), row_chunk))
            return carry
        lax.fori_loop(0, tm // row_chunk, body, 0)

    @pl.when(f == 0)
    def _():
        def init(sl):
            x = x_ref[sl, :]
            hn_ref[sl, :] = _rms_rows(x, g_ref[...]).astype(BF16)
            o_ref[sl, :] = x
        rows(init)

    u = jnp.dot(hn_ref[...], wu_ref[...], preferred_element_type=F32)
    a = jnp.square(jnp.maximum(u, 0.0)).astype(BF16)
    o_ref[...] += jnp.dot(a, wd_ref[...], preferred_element_type=F32)

    if has_final:
        @pl.when(f == pl.num_programs(1) - 1)
        def _():
            def fin(sl):
                o_ref[sl, :] = _rms_rows(o_ref[sl, :], gf_ref[...])
            rows(fin)


def _mlp(x, g, wu, wd, g_final=None, *, tm=1024, tf=512):
    T, D = x.shape
    Fdim = wu.shape[1]
    tm = _tile(T, tm)
    tf = _tile(Fdim, tf)
    has_final = g_final is not None
    in_specs = [pl.BlockSpec((tm, D), lambda i, f: (i, 0)),
                pl.BlockSpec((1, D), lambda i, f: (0, 0)),
                pl.BlockSpec((D, tf), lambda i, f: (0, f)),
                pl.BlockSpec((tf, D), lambda i, f: (f, 0))]
    args = [x, g.reshape(1, D), wu, wd]
    if has_final:
        in_specs.append(pl.BlockSpec((1, D), lambda i, f: (0, 0)))
        args.append(g_final.reshape(1, D))
    return pl.pallas_call(
        functools.partial(_mlp_kernel, has_final=has_final, row_chunk=min(128, tm)),
        grid=(T // tm, Fdim // tf),
        in_specs=in_specs,
        out_specs=pl.BlockSpec((tm, D), lambda i, f: (i, 0)),
        out_shape=jax.ShapeDtypeStruct((T, D), F32),
        scratch_shapes=[pltpu.VMEM((tm, D), BF16)],
        compiler_params=_params(("parallel", "arbitrary")),
        name="mlp_final" if has_final else "mlp",
    )(*args)


def _retention_kernel(q_ref, k_ref, v_ref, g_ref, cos_ref, sin_ref, din_ref, qd_ref, kd_ref,
                      cd_ref, gn_ref, o_ref, state_ref, *, chunk, nchunks):
    @pl.when(pl.program_id(2) == 0)
    def _():
        state_ref[...] = jnp.zeros_like(state_ref)

    half = q_ref.shape[1] // 2
    for ci in range(nchunks):
        sl = slice(ci * chunk, (ci + 1) * chunk)
        cos = cos_ref[sl, :]
        sin = sin_ref[sl, :]

        def rot(x):
            x1, x2 = x[:, :half], x[:, half:]
            return jnp.concatenate([x1 * cos - x2 * sin, x1 * sin + x2 * cos], axis=-1)

        q = rot(q_ref[sl, :].astype(F32))
        k = rot(k_ref[sl, :].astype(F32))
        v = v_ref[sl, :]
        state = state_ref[...]

        inner = lax.dot_general(q.astype(BF16), k.astype(BF16), _NT,
                                preferred_element_type=F32) * din_ref[...]
        o = jnp.dot(inner.astype(BF16), v, preferred_element_type=F32)
        o = o + jnp.dot((q * qd_ref[...]).astype(BF16), state.astype(BF16),
                        preferred_element_type=F32)
        kd_t = (k * kd_ref[...]).T.astype(BF16)
        state_ref[...] = cd_ref[...] * state + jnp.dot(kd_t, v, preferred_element_type=F32)

        ms = jnp.mean(o * o, axis=-1, keepdims=True)
        on = o * lax.rsqrt(ms + RMS_EPS) * gn_ref[...]
        gate = g_ref[sl, :].astype(F32)
        gate = gate * jax.nn.sigmoid(gate)
        o_ref[sl, :] = (gate * on).astype(o_ref.dtype)


def _retention_tables(heads, chunk, seq):
    dk, dv = RET_QK_DIM, RET_V_DIM
    half = dk // 2
    pos = np.arange(seq, dtype=np.float64)
    inv = ROPE_BASE ** (-np.arange(half, dtype=np.float64) / half)
    ang = pos[:, None] * inv[None, :]
    log_gamma = np.log1p(-np.exp2(-5.0 - np.arange(heads, dtype=np.float64)))
    idx = np.arange(chunk, dtype=np.float64)
    diff = idx[:, None] - idx[None, :]
    decay_in = np.where(diff >= 0,
                        np.exp(log_gamma[:, None, None] * np.maximum(diff, 0.0)), 0.0)
    q_decay = np.exp(log_gamma[:, None] * (idx + 1.0))[:, :, None]
    k_decay = np.exp(log_gamma[:, None] * (chunk - 1.0 - idx))[:, :, None]
    chunk_decay = np.exp(log_gamma * chunk)[:, None, None]
    tables = (np.cos(ang), np.sin(ang), decay_in,
              np.broadcast_to(q_decay, (heads, chunk, dk)),
              np.broadcast_to(k_decay, (heads, chunk, dk)),
              np.broadcast_to(chunk_decay, (heads, 1, dv)))
    return tuple(jnp.asarray(t, dtype=F32) for t in tables)


def _retention(proj, gn, *, batch, heads, chunk=256, rows=512):
    T = proj.shape[0]
    S = T // batch
    dk, dv = RET_QK_DIM, RET_V_DIM
    chunk = min(chunk, S)
    rows = min(rows, S)
    ns = S // rows
    cos, sin, din, qd, kd, cd = _retention_tables(heads, chunk, S)
    vblk0 = 2 * heads * dk // dv
    return pl.pallas_call(
        functools.partial(_retention_kernel, chunk=chunk, nchunks=rows // chunk),
        grid=(batch, heads, ns),
        in_specs=[
            pl.BlockSpec((rows, dk), lambda b, h, s: (b * ns + s, h)),
            pl.BlockSpec((rows, dk), lambda b, h, s: (b * ns + s, heads + h)),
            pl.BlockSpec((rows, dv), lambda b, h, s: (b * ns + s, vblk0 + h)),
            pl.BlockSpec((rows, dv), lambda b, h, s: (b * ns + s, vblk0 + heads + h)),
            pl.BlockSpec((rows, dk // 2), lambda b, h, s: (s, 0)),
            pl.BlockSpec((rows, dk // 2), lambda b, h, s: (s, 0)),
            pl.BlockSpec((None, chunk, chunk), lambda b, h, s: (h, 0, 0)),
            pl.BlockSpec((None, chunk, dk), lambda b, h, s: (h, 0, 0)),
            pl.BlockSpec((None, chunk, dk), lambda b, h, s: (h, 0, 0)),
            pl.BlockSpec((None, 1, dv), lambda b, h, s: (h, 0, 0)),
            pl.BlockSpec((1, dv), lambda b, h, s: (0, h)),
        ],
        out_specs=pl.BlockSpec((rows, dv), lambda b, h, s: (b * ns + s, h)),
        out_shape=jax.ShapeDtypeStruct((T, heads * dv), BF16),
        scratch_shapes=[pltpu.VMEM((dk, dv), F32)],
        compiler_params=_params(("parallel", "parallel", "arbitrary")),
        name="retention",
    )(proj, proj, proj, proj, cos, sin, din, qd, kd, cd, gn.reshape(1, heads * dv))


def kernel(x, fox_norm, fox_wq, fox_wk, fox_wv, fox_wf, fox_bf, fox_wo, ret_norm, ret_wq, ret_wk,
           ret_wv, ret_wg, ret_gn, ret_wo, mlp_norm, mlp_up, mlp_down, final_norm):
    B, S, D = x.shape
    T = B * S
    fox_heads = D // FOX_HEAD_DIM
    ret_heads = D // RET_QK_DIM
    h = x.reshape(T, D)

    w_qkv = _cast_concat([fox_wq, fox_wk, fox_wv], 0)
    cs = jnp.concatenate([jnp.full((D,), LOG2E * FOX_HEAD_DIM ** -0.5, F32),
                          jnp.ones((2 * D,), F32)])
    wf_pad = jnp.pad(fox_wf[0], ((0, 0), (0, LANES - fox_heads))).astype(BF16)
    bf_pad = jnp.pad(fox_bf[0], (0, LANES - fox_heads)).reshape(1, LANES)
    qkv, z = _norm_matmul(h, fox_norm[0], w_qkv, cs, wf_pad)
    cp = _gate_cumsum(z, bf_pad, B, fox_heads)
    attn = _fox_attention(qkv, cp, batch=B, heads=fox_heads)
    h = _matmul_res(attn, _cast_concat([fox_wo], 0), h)
    h = _mlp(h, mlp_norm[0], _cast_concat([mlp_up], 0), _cast_concat([mlp_down], 0))

    w_proj = _cast_concat([ret_wq, ret_wk, ret_wv, ret_wg], 0)
    cs = jnp.concatenate([jnp.ones((D,), F32), jnp.full((D,), RET_QK_DIM ** -0.5, F32),
                          jnp.ones((2 * ret_heads * RET_V_DIM,), F32)])
    proj = _norm_matmul(h, ret_norm[0], w_proj, cs)
    y = _retention(proj, ret_gn[0], batch=B, heads=ret_heads)
    h = _matmul_res(y, _cast_concat([ret_wo], 0), h)
    h = _mlp(h, mlp_norm[1], _cast_concat([mlp_up], 1), _cast_concat([mlp_down], 1), final_norm)
    return h.reshape(B, S, D)
```

```python
import functools
import math

import numpy as np
import jax
import jax.numpy as jnp
from jax import lax
from jax.experimental import pallas as pl
from jax.experimental.pallas import tpu as pltpu

F32 = jnp.float32
BF16 = jnp.bfloat16

RMS_EPS = 1e-6
ROPE_BASE = 10000.0
FOX_HEAD_DIM = 128
RET_QK_DIM = 256
RET_V_DIM = 512
LANES = 128
BF16_ROWS = 16
NEG_BIG = -1e30
LOG2E = math.log2(math.e)
V7X_VMEM_LIMIT = 56 * 1024 * 1024

_NT = (((1,), (1,)), ((), ()))


def _params(sem, vmem=V7X_VMEM_LIMIT):
    return pltpu.CompilerParams(dimension_semantics=sem, vmem_limit_bytes=vmem)


def _tile(n, target, align=LANES):
    if n <= target:
        return n
    t = (target // align) * align
    while n % t:
        t -= align
    return t


def _rms_rows(x, g):
    ms = jnp.mean(x * x, axis=-1, keepdims=True)
    return x * lax.rsqrt(ms + RMS_EPS) * g


def _cast_concat_kernel(*refs, starts, nblks):
    o_ref = refs[-1]
    p = pl.program_id(0)
    for w_ref, s0, nb in zip(refs[:-1], starts, nblks):
        @pl.when((p >= s0) & (p < s0 + nb))
        def _(w_ref=w_ref):
            o_ref[...] = w_ref[...].astype(o_ref.dtype)


def _cast_concat(ws, layer, *, bw=2048, tr=512):
    R = ws[0].shape[1]
    bw = min(bw, min(w.shape[2] for w in ws))
    tr = _tile(R, tr)
    nr = R // tr
    nblks = [w.shape[2] // bw for w in ws]
    starts = [sum(nblks[:i]) for i in range(len(ws))]

    def in_map(s0, nb):
        def index(p, r):
            local = p - s0
            row = jnp.where(local < 0, 0, jnp.where(local >= nb, nr - 1, r))
            return layer, row, jnp.clip(local, 0, nb - 1)
        return index

    return pl.pallas_call(
        functools.partial(_cast_concat_kernel, starts=starts, nblks=nblks),
        grid=(sum(nblks), nr),
        in_specs=[pl.BlockSpec((None, tr, bw), in_map(s0, nb)) for s0, nb in zip(starts, nblks)],
        out_specs=pl.BlockSpec((tr, bw), lambda p, r: (r, p)),
        out_shape=jax.ShapeDtypeStruct((R, sum(nblks) * bw), BF16),
        compiler_params=_params(("arbitrary", "arbitrary")),
        name="cast_concat",
    )(*ws)


def _side_cast_plan(groups, grid):
    nsteps = math.prod(grid)

    def lin(*ids):
        l = ids[0]
        for n, i in zip(grid[1:], ids[1:]):
            l = l * n + i
        return l

    in_specs, args, out_specs, out_shapes, widths = [], [], [], [], []
    for ws, layer in groups:
        R = ws[0].shape[1]
        rows = BF16_ROWS
        while R // rows > nsteps:
            rows *= 2
        assert R % rows == 0
        nblk = R // rows
        rep = nsteps // nblk

        def blk(*ids, rep=rep, nblk=nblk):
            return jnp.minimum(lin(*ids) // rep, nblk - 1)

        for w in ws:
            in_specs.append(pl.BlockSpec((None, rows, w.shape[2]),
                                         lambda *ids, blk=blk, layer=layer: (layer, blk(*ids), 0)))
            args.append(w)
        cols = sum(w.shape[2] for w in ws)
        out_specs.append(pl.BlockSpec((rows, cols), lambda *ids, blk=blk: (blk(*ids), 0)))
        out_shapes.append(jax.ShapeDtypeStruct((R, cols), BF16))
        widths.append(tuple(w.shape[2] for w in ws))
    return in_specs, args, out_specs, out_shapes, tuple(widths)


def _side_cast_body(in_refs, out_refs, widths):
    k = 0
    for o_ref, ws in zip(out_refs, widths):
        c0 = 0
        for wd in ws:
            o_ref[:, c0:c0 + wd] = in_refs[k][...].astype(BF16)
            k += 1
            c0 += wd


def _n_cast_in(widths):
    return sum(len(ws) for ws in widths)


def _norm_matmul_kernel(x_ref, g_ref, w_ref, cs_ref, *rest, has_gate, row_chunk, cast_widths):
    rest = list(rest)
    wz_ref = rest.pop(0) if has_gate else None
    cast_in = [rest.pop(0) for _ in range(_n_cast_in(cast_widths))]
    o_ref = rest.pop(0)
    z_ref = rest.pop(0) if has_gate else None
    cast_out = [rest.pop(0) for _ in cast_widths]
    hn_ref, = rest
    tm = x_ref.shape[0]
    _side_cast_body(cast_in, cast_out, cast_widths)

    @pl.when(pl.program_id(1) == 0)
    def _():
        def body(i, carry):
            r0 = pl.multiple_of(i * row_chunk, row_chunk)
            y = _rms_rows(x_ref[pl.ds(r0, row_chunk), :], g_ref[...])
            hn_ref[pl.ds(r0, row_chunk), :] = y.astype(BF16)
            return carry
        lax.fori_loop(0, tm // row_chunk, body, 0)
        if has_gate:
            z_ref[...] = jnp.dot(hn_ref[...], wz_ref[...], preferred_element_type=F32)

    acc = jnp.dot(hn_ref[...], w_ref[...], preferred_element_type=F32)
    o_ref[...] = (acc * cs_ref[...]).astype(o_ref.dtype)


def _norm_matmul(x, g, w, colscale, wz=None, *, casts=(), tm=1024, tn=1024):
    T, D = x.shape
    N = w.shape[1]
    tm = _tile(T, tm)
    tn = _tile(N, tn)
    has_gate = wz is not None
    grid = (T // tm, N // tn)
    c_in, c_args, c_out, c_shapes, c_widths = _side_cast_plan(casts, grid)
    in_specs = [
        pl.BlockSpec((tm, D), lambda i, j: (i, 0)),
        pl.BlockSpec((1, D), lambda i, j: (0, 0)),
        pl.BlockSpec((D, tn), lambda i, j: (0, j)),
        pl.BlockSpec((1, tn), lambda i, j: (0, j)),
    ]
    args = [x, g.reshape(1, D), w, colscale.reshape(1, N)]
    out_shape = [jax.ShapeDtypeStruct((T, N), BF16)]
    out_specs = [pl.BlockSpec((tm, tn), lambda i, j: (i, j))]
    if has_gate:
        in_specs.append(pl.BlockSpec((D, LANES), lambda i, j: (0, 0)))
        args.append(wz)
        out_shape.append(jax.ShapeDtypeStruct((T, LANES), F32))
        out_specs.append(pl.BlockSpec((tm, LANES), lambda i, j: (i, 0)))
    res = pl.pallas_call(
        functools.partial(_norm_matmul_kernel, has_gate=has_gate, row_chunk=min(128, tm),
                          cast_widths=c_widths),
        grid=grid,
        in_specs=in_specs + c_in,
        out_specs=out_specs + c_out,
        out_shape=out_shape + c_shapes,
        scratch_shapes=[pltpu.VMEM((tm, D), BF16)],
        compiler_params=_params(("parallel", "arbitrary")),
        name="norm_matmul_gate" if has_gate else "norm_matmul",
    )(*args, *c_args)
    return res


def _gate_cumsum_kernel(z_ref, b_ref, cp_ref, *, heads):
    z = z_ref[...] + b_ref[...]
    c = jnp.minimum(z, 0.0) - jnp.log1p(jnp.exp(-jnp.abs(z)))
    S = c.shape[0]
    row = lax.broadcasted_iota(jnp.int32, c.shape, 0)
    shift = 1
    while shift < S:
        c = c + jnp.where(row >= shift, pltpu.roll(c, shift, axis=0), 0.0)
        shift *= 2
    c = c * LOG2E
    hi = c.astype(BF16).astype(F32)
    r1 = c - hi
    mid = r1.astype(BF16).astype(F32)
    lo = r1 - mid
    lane = lax.broadcasted_iota(jnp.int32, c.shape, 1)
    packed = jnp.where(lane < heads, hi,
                       jnp.where(lane < 2 * heads, pltpu.roll(mid, heads, axis=1),
                                 jnp.where(lane < 3 * heads, pltpu.roll(lo, 2 * heads, axis=1), 0.0)))
    cp_ref[...] = packed.astype(BF16)


def _gate_cumsum(z, b_pad, batch, heads):
    T = z.shape[0]
    S = T // batch
    assert 3 * heads <= LANES
    return pl.pallas_call(
        functools.partial(_gate_cumsum_kernel, heads=heads),
        grid=(batch,),
        in_specs=[pl.BlockSpec((S, LANES), lambda b: (b, 0)),
                  pl.BlockSpec((1, LANES), lambda b: (0, 0))],
        out_specs=pl.BlockSpec((S, LANES), lambda b: (b, 0)),
        out_shape=jax.ShapeDtypeStruct((T, LANES), BF16),
        compiler_params=_params(("parallel",)),
        name="gate_cumsum",
    )(z, b_pad)


ONES_ROWS = 16
N_PIECES = 3


def _piece_selector(head0, hpg, heads, sign, lane0):
    r = lax.broadcasted_iota(jnp.int32, (LANES, hpg * LANES), 0)
    col = lax.broadcasted_iota(jnp.int32, (LANES, hpg * LANES), 1)
    g = col // LANES
    p = col % LANES - lane0
    hit = (p >= 0) & (p < N_PIECES) & (r == head0 + g + heads * p)
    return jnp.where(hit, sign, 0.0).astype(BF16)


def _fox_attn_kernel(q_ref, k_ref, v_ref, cp_ref, *rest, tq, tk, hpg, heads, cast_widths):
    rest = list(rest)
    cast_in = [rest.pop(0) for _ in range(_n_cast_in(cast_widths))]
    o_ref = rest.pop(0)
    cast_out = [rest.pop(0) for _ in cast_widths]
    kx_ref, vt_ref, acc_ref, st_ref, p_ref = rest
    _side_cast_body(cast_in, cast_out, cast_widths)
    hg = pl.program_id(1)
    qi = pl.program_id(2)
    S = k_ref.shape[0]
    dh = FOX_HEAD_DIM
    lane = lax.broadcasted_iota(jnp.int32, (1, hpg * LANES), 1) % LANES
    cols = [slice(g * dh, (g + 1) * dh) for g in range(hpg)]

    @pl.when(qi == 0)
    def _():
        sel = _piece_selector(hg * hpg, hpg, heads, -1.0, 0)
        ones = jnp.where((lane >= N_PIECES) & (lane < 2 * N_PIECES), 1.0, 0.0)
        for g in range(hpg):
            vt_ref[g, dh:, :] = jnp.ones((ONES_ROWS, S), BF16)

        def body(i, carry):
            sl = pl.ds(pl.multiple_of(i * tq, tq), tq)
            ek = (jnp.dot(cp_ref[sl, :], sel, preferred_element_type=F32) + ones).astype(BF16)
            for g in range(hpg):
                kx_ref[g, sl, :dh] = k_ref[sl, cols[g]]
                kx_ref[g, sl, dh:] = ek[:, cols[g]]
                vt_ref[g, :dh, sl] = v_ref[sl, cols[g]].astype(F32).T.astype(BF16)
            return carry
        lax.fori_loop(0, S // tq, body, 0)

    q0 = pl.multiple_of(qi * tq, tq)
    sel_q = _piece_selector(hg * hpg, hpg, heads, 1.0, N_PIECES)
    cref = jnp.dot(cp_ref[pl.ds(q0, ONES_ROWS), :], sel_q, preferred_element_type=F32)[0:1, :]
    eq = (cref + jnp.where(lane < N_PIECES, 1.0, 0.0)).astype(BF16)
    qx = [jnp.concatenate([q_ref[:, cols[g]], jnp.broadcast_to(eq[:, cols[g]], (tq, dh))], axis=1)
          for g in range(hpg)]

    def scores(g, j, diag):
        k0 = pl.multiple_of(j * tk, tk)
        st = lax.dot_general(kx_ref[g, pl.ds(k0, tk), :], qx[g], _NT, preferred_element_type=F32)
        if diag is not None:
            kk = lax.broadcasted_iota(jnp.int32, st.shape, 0) + diag * tk
            qq = lax.broadcasted_iota(jnp.int32, st.shape, 1)
            st = jnp.where(kk <= qq, st, NEG_BIG)
        return st, jnp.max(st, axis=0, keepdims=True)

    def pv_update(g, j, slot, alpha):
        k0 = pl.multiple_of(j * tk, tk)
        pv = jnp.dot(vt_ref[g, :, pl.ds(k0, tk)], p_ref[g, slot], preferred_element_type=F32)
        acc_ref[g] = alpha * acc_ref[g] + pv

    def stage(g, slot, carry, prev_blk, next_blk, next_diag=None):
        m, alpha_prev, cmax = carry
        if prev_blk is not None:
            pv_update(g, prev_blk, 1 - slot, alpha_prev)
        st_next, cmax_next = scores(g, next_blk, next_diag)
        st_ref[g, 1 - slot] = st_next
        m_new = jnp.maximum(m, cmax)
        p_ref[g, slot] = jnp.exp2(st_ref[g, slot] - m_new).astype(BF16)
        return m_new, jnp.exp2(m - m_new), cmax_next

    assert tq == 2 * tk
    d0 = 2 * qi
    carries = []
    for g in range(hpg):
        acc_ref[g] = jnp.zeros(acc_ref.shape[1:], F32)
        st, cmax = scores(g, d0, 0)
        st_ref[g, 0] = st
        c = (jnp.full((1, tq), NEG_BIG, F32), jnp.ones((1, tq), F32), cmax)
        c = stage(g, 0, c, None, d0 + 1, 1)
        carries.append(stage(g, 1, c, d0, 0))

    def pair(i, carries):
        out = []
        for g in range(hpg):
            c = stage(g, 0, carries[g], jnp.where(i == 0, d0 + 1, 2 * i - 1), 2 * i + 1)
            out.append(stage(g, 1, c, 2 * i, 2 * i + 2))
        return tuple(out)

    carries = lax.fori_loop(0, qi, pair, tuple(carries))
    last_blk = jnp.where(qi == 0, d0 + 1, d0 - 1)
    for g in range(hpg):
        pv_update(g, last_blk, 1, carries[g][1])
        ot = acc_ref[g, :dh, :] / acc_ref[g, dh:dh + 1, :]
        o_ref[:, cols[g]] = ot.T.astype(o_ref.dtype)


def _fox_attention(qkv, cp, *, batch, heads, casts=(), tq=512, hpg=4):
    T = qkv.shape[0]
    S = T // batch
    dh = FOX_HEAD_DIM
    tq = min(tq, S)
    tk = tq // 2
    nq = S // tq
    ng = heads // hpg
    w = hpg * dh
    grid = (batch, ng, nq)
    c_in, c_args, c_out, c_shapes, c_widths = _side_cast_plan(casts, grid)
    return pl.pallas_call(
        functools.partial(_fox_attn_kernel, tq=tq, tk=tk, hpg=hpg, heads=heads,
                          cast_widths=c_widths),
        grid=grid,
        in_specs=[
            pl.BlockSpec((tq, w), lambda b, h, i: (b * nq + i, h)),
            pl.BlockSpec((S, w), lambda b, h, i: (b, ng + h)),
            pl.BlockSpec((S, w), lambda b, h, i: (b, 2 * ng + h)),
            pl.BlockSpec((S, LANES), lambda b, h, i: (b, 0)),
        ] + c_in,
        out_specs=[pl.BlockSpec((tq, w), lambda b, h, i: (b * nq + i, h))] + c_out,
        out_shape=[jax.ShapeDtypeStruct((T, heads * dh), BF16)] + c_shapes,
        scratch_shapes=[pltpu.VMEM((hpg, S, 2 * dh), BF16),
                        pltpu.VMEM((hpg, dh + ONES_ROWS, S), BF16),
                        pltpu.VMEM((hpg, dh + ONES_ROWS, tq), F32),
                        pltpu.VMEM((hpg, 2, tk, tq), F32),
                        pltpu.VMEM((hpg, 2, tk, tq), BF16)],
        compiler_params=_params(("parallel", "parallel", "arbitrary")),
        name="fox_attention",
    )(qkv, qkv, qkv, cp, *c_args)


def _matmul_res_kernel(a_ref, w_ref, r_ref, o_ref):
    o_ref[...] = r_ref[...] + jnp.dot(a_ref[...], w_ref[...], preferred_element_type=F32)


def _matmul_res(a, w, res, *, tm=1024, tn=512):
    T, K = a.shape
    N = w.shape[1]
    tm = _tile(T, tm)
    tn = _tile(N, tn)
    return pl.pallas_call(
        _matmul_res_kernel,
        grid=(T // tm, N // tn),
        in_specs=[pl.BlockSpec((tm, K), lambda i, j: (i, 0)),
                  pl.BlockSpec((K, tn), lambda i, j: (0, j)),
                  pl.BlockSpec((tm, tn), lambda i, j: (i, j))],
        out_specs=pl.BlockSpec((tm, tn), lambda i, j: (i, j)),
        out_shape=jax.ShapeDtypeStruct((T, N), F32),
        compiler_params=_params(("parallel", "arbitrary")),
        name="matmul_res",
    )(a, w, res)


def _mlp_kernel(x_ref, g_ref, wu_ref, wd_ref, *rest, has_final, row_chunk, cast_widths):
    rest = list(rest)
    gf_ref = rest.pop(0) if has_final else None
    cast_in = [rest.pop(0) for _ in range(_n_cast_in(cast_widths))]
    o_ref = rest.pop(0)
    cast_out = [rest.pop(0) for _ in cast_widths]
    hn_ref, = rest
    _side_cast_body(cast_in, cast_out, cast_widths)
    tm = x_ref.shape[0]
    f = pl.program_id(1)

    def rows(fn):
        def body(i, carry):
            fn(pl.ds(pl.multiple_of(i * row_chunk, row_chunk), row_chunk))
            return carry
        lax.fori_loop(0, tm // row_chunk, body, 0)

    @pl.when(f == 0)
    def _():
        def init(sl):
            x = x_ref[sl, :]
            hn_ref[sl, :] = _rms_rows(x, g_ref[...]).astype(BF16)
            o_ref[sl, :] = x
        rows(init)

    u = jnp.dot(hn_ref[...], wu_ref[...], preferred_element_type=F32)
    a = jnp.square(jnp.maximum(u, 0.0)).astype(BF16)
    o_ref[...] += jnp.dot(a, wd_ref[...], preferred_element_type=F32)

    if has_final:
        @pl.when(f == pl.num_programs(1) - 1)
        def _():
            def fin(sl):
                o_ref[sl, :] = _rms_rows(o_ref[sl, :], gf_ref[...])
            rows(fin)


def _mlp(x, g, wu, wd, g_final=None, *, casts=(), tm=1024, tf=512):
    T, D = x.shape
    Fdim = wu.shape[1]
    tm = _tile(T, tm)
    tf = _tile(Fdim, tf)
    has_final = g_final is not None
    in_specs = [pl.BlockSpec((tm, D), lambda i, f: (i, 0)),
                pl.BlockSpec((1, D), lambda i, f: (0, 0)),
                pl.BlockSpec((D, tf), lambda i, f: (0, f)),
                pl.BlockSpec((tf, D), lambda i, f: (f, 0))]
    args = [x, g.reshape(1, D), wu, wd]
    if has_final:
        in_specs.append(pl.BlockSpec((1, D), lambda i, f: (0, 0)))
        args.append(g_final.reshape(1, D))
    grid = (T // tm, Fdim // tf)
    c_in, c_args, c_out, c_shapes, c_widths = _side_cast_plan(casts, grid)
    return pl.pallas_call(
        functools.partial(_mlp_kernel, has_final=has_final, row_chunk=min(128, tm),
                          cast_widths=c_widths),
        grid=grid,
        in_specs=in_specs + c_in,
        out_specs=[pl.BlockSpec((tm, D), lambda i, f: (i, 0))] + c_out,
        out_shape=[jax.ShapeDtypeStruct((T, D), F32)] + c_shapes,
        scratch_shapes=[pltpu.VMEM((tm, D), BF16)],
        compiler_params=_params(("parallel", "arbitrary")),
        name="mlp_final" if has_final else "mlp",
    )(*args, *c_args)


def _retention_kernel(q_ref, k_ref, v_ref, g_ref, cos_ref, sin_ref, din_ref, qd_ref, kd_ref,
                      cd_ref, gn_ref, *rest, chunk, nchunks, cast_widths):
    rest = list(rest)
    cast_in = [rest.pop(0) for _ in range(_n_cast_in(cast_widths))]
    o_ref = rest.pop(0)
    cast_out = [rest.pop(0) for _ in cast_widths]
    state_ref, = rest
    _side_cast_body(cast_in, cast_out, cast_widths)

    @pl.when(pl.program_id(2) == 0)
    def _():
        state_ref[...] = jnp.zeros_like(state_ref)

    half = q_ref.shape[1] // 2
    for ci in range(nchunks):
        sl = slice(ci * chunk, (ci + 1) * chunk)
        cos = cos_ref[sl, :]
        sin = sin_ref[sl, :]

        def rot(x):
            x1, x2 = x[:, :half], x[:, half:]
            return jnp.concatenate([x1 * cos - x2 * sin, x1 * sin + x2 * cos], axis=-1)

        q = rot(q_ref[sl, :].astype(F32))
        k = rot(k_ref[sl, :].astype(F32))
        v = v_ref[sl, :]
        state = state_ref[...]

        inner = lax.dot_general(q.astype(BF16), k.astype(BF16), _NT,
                                preferred_element_type=F32) * din_ref[...]
        o = jnp.dot(inner.astype(BF16), v, preferred_element_type=F32)
        o = o + jnp.dot((q * qd_ref[...]).astype(BF16), state.astype(BF16),
                        preferred_element_type=F32)
        kd_t = (k * kd_ref[...]).T.astype(BF16)
        state_ref[...] = cd_ref[...] * state + jnp.dot(kd_t, v, preferred_element_type=F32)

        ms = jnp.mean(o * o, axis=-1, keepdims=True)
        on = o * lax.rsqrt(ms + RMS_EPS) * gn_ref[...]
        gate = g_ref[sl, :].astype(F32)
        gate = gate * jax.nn.sigmoid(gate)
        o_ref[sl, :] = (gate * on).astype(o_ref.dtype)


def _retention_tables(heads, chunk, seq):
    dk, dv = RET_QK_DIM, RET_V_DIM
    half = dk // 2
    pos = np.arange(seq, dtype=np.float64)
    inv = ROPE_BASE ** (-np.arange(half, dtype=np.float64) / half)
    ang = pos[:, None] * inv[None, :]
    log_gamma = np.log1p(-np.exp2(-5.0 - np.arange(heads, dtype=np.float64)))
    idx = np.arange(chunk, dtype=np.float64)
    diff = idx[:, None] - idx[None, :]
    decay_in = np.where(diff >= 0,
                        np.exp(log_gamma[:, None, None] * np.maximum(diff, 0.0)), 0.0)
    q_decay = np.exp(log_gamma[:, None] * (idx + 1.0))[:, :, None]
    k_decay = np.exp(log_gamma[:, None] * (chunk - 1.0 - idx))[:, :, None]
    chunk_decay = np.exp(log_gamma * chunk)[:, None, None]
    tables = (np.cos(ang), np.sin(ang), decay_in,
              np.broadcast_to(q_decay, (heads, chunk, dk)),
              np.broadcast_to(k_decay, (heads, chunk, dk)),
              np.broadcast_to(chunk_decay, (heads, 1, dv)))
    return tuple(jnp.asarray(t, dtype=F32) for t in tables)


def _retention(proj, gn, *, batch, heads, casts=(), chunk=256, rows=512):
    T = proj.shape[0]
    S = T // batch
    dk, dv = RET_QK_DIM, RET_V_DIM
    chunk = min(chunk, S)
    rows = min(rows, S)
    ns = S // rows
    cos, sin, din, qd, kd, cd = _retention_tables(heads, chunk, S)
    vblk0 = 2 * heads * dk // dv
    grid = (batch, heads, ns)
    c_in, c_args, c_out, c_shapes, c_widths = _side_cast_plan(casts, grid)
    return pl.pallas_call(
        functools.partial(_retention_kernel, chunk=chunk, nchunks=rows // chunk,
                          cast_widths=c_widths),
        grid=grid,
        in_specs=[
            pl.BlockSpec((rows, dk), lambda b, h, s: (b * ns + s, h)),
            pl.BlockSpec((rows, dk), lambda b, h, s: (b * ns + s, heads + h)),
            pl.BlockSpec((rows, dv), lambda b, h, s: (b * ns + s, vblk0 + h)),
            pl.BlockSpec((rows, dv), lambda b, h, s: (b * ns + s, vblk0 + heads + h)),
            pl.BlockSpec((rows, dk // 2), lambda b, h, s: (s, 0)),
            pl.BlockSpec((rows, dk // 2), lambda b, h, s: (s, 0)),
            pl.BlockSpec((None, chunk, chunk), lambda b, h, s: (h, 0, 0)),
            pl.BlockSpec((None, chunk, dk), lambda b, h, s: (h, 0, 0)),
            pl.BlockSpec((None, chunk, dk), lambda b, h, s: (h, 0, 0)),
            pl.BlockSpec((None, 1, dv), lambda b, h, s: (h, 0, 0)),
            pl.BlockSpec((1, dv), lambda b, h, s: (0, h)),
        ] + c_in,
        out_specs=[pl.BlockSpec((rows, dv), lambda b, h, s: (b * ns + s, h))] + c_out,
        out_shape=[jax.ShapeDtypeStruct((T, heads * dv), BF16)] + c_shapes,
        scratch_shapes=[pltpu.VMEM((dk, dv), F32)],
        compiler_params=_params(("parallel", "parallel", "arbitrary")),
        name="retention",
    )(proj, proj, proj, proj, cos, sin, din, qd, kd, cd, gn.reshape(1, heads * dv), *c_args)


def kernel(x, fox_norm, fox_wq, fox_wk, fox_wv, fox_wf, fox_bf, fox_wo, ret_norm, ret_wq, ret_wk,
           ret_wv, ret_wg, ret_gn, ret_wo, mlp_norm, mlp_up, mlp_down, final_norm):
    B, S, D = x.shape
    T = B * S
    fox_heads = D // FOX_HEAD_DIM
    ret_heads = D // RET_QK_DIM
    h = x.reshape(T, D)

    w_qkv = _cast_concat([fox_wq, fox_wk, fox_wv], 0)
    cs = jnp.concatenate([jnp.full((D,), LOG2E * FOX_HEAD_DIM ** -0.5, F32),
                          jnp.ones((2 * D,), F32)])
    wf_pad = jnp.pad(fox_wf[0], ((0, 0), (0, LANES - fox_heads))).astype(BF16)
    bf_pad = jnp.pad(fox_bf[0], (0, LANES - fox_heads)).reshape(1, LANES)
    qkv, z = _norm_matmul(h, fox_norm[0], w_qkv, cs, wf_pad)
    cp = _gate_cumsum(z, bf_pad, B, fox_heads)
    attn, wo0, up0, down0 = _fox_attention(
        qkv, cp, batch=B, heads=fox_heads,
        casts=[([fox_wo], 0), ([mlp_up], 0), ([mlp_down], 0)])
    h = _matmul_res(attn, wo0, h)
    h, w_proj = _mlp(h, mlp_norm[0], up0, down0,
                     casts=[([ret_wq, ret_wk, ret_wv, ret_wg], 0)])

    cs = jnp.concatenate([jnp.ones((D,), F32), jnp.full((D,), RET_QK_DIM ** -0.5, F32),
                          jnp.ones((2 * ret_heads * RET_V_DIM,), F32)])
    proj, wo1 = _norm_matmul(h, ret_norm[0], w_proj, cs, casts=[([ret_wo], 0)])
    y, up1, down1 = _retention(proj, ret_gn[0], batch=B, heads=ret_heads,
                               casts=[([mlp_up], 1), ([mlp_down], 1)])
    h = _matmul_res(y, wo1, h)
    h, = _mlp(h, mlp_norm[1], up1, down1, final_norm)
    return h.reshape(B, S, D)
```

```python
import functools
import math

import numpy as np
import jax
import jax.numpy as jnp
from jax import lax
from jax.experimental import pallas as pl
from jax.experimental.pallas import tpu as pltpu

F32 = jnp.float32
BF16 = jnp.bfloat16

RMS_EPS = 1e-6
ROPE_BASE = 10000.0
FOX_HEAD_DIM = 128
RET_QK_DIM = 256
RET_V_DIM = 512
LANES = 128
BF16_ROWS = 16
NEG_BIG = -1e30
LOG2E = math.log2(math.e)
V7X_VMEM_LIMIT = 56 * 1024 * 1024

_NT = (((1,), (1,)), ((), ()))


def _params(sem, vmem=V7X_VMEM_LIMIT):
    return pltpu.CompilerParams(dimension_semantics=sem, vmem_limit_bytes=vmem)


def _tile(n, target, align=LANES):
    if n <= target:
        return n
    t = (target // align) * align
    while n % t:
        t -= align
    return t


def _rms_rows(x, g):
    ms = jnp.mean(x * x, axis=-1, keepdims=True)
    return x * lax.rsqrt(ms + RMS_EPS) * g


def _cast_concat_kernel(*refs, starts, nblks):
    o_ref = refs[-1]
    p = pl.program_id(0)
    for w_ref, s0, nb in zip(refs[:-1], starts, nblks):
        @pl.when((p >= s0) & (p < s0 + nb))
        def _(w_ref=w_ref):
            o_ref[...] = w_ref[...].astype(o_ref.dtype)


def _cast_concat(ws, layer, *, bw=2048, tr=512):
    R = ws[0].shape[1]
    bw = min(bw, min(w.shape[2] for w in ws))
    tr = _tile(R, tr)
    nr = R // tr
    nblks = [w.shape[2] // bw for w in ws]
    starts = [sum(nblks[:i]) for i in range(len(ws))]

    def in_map(s0, nb):
        def index(p, r):
            local = p - s0
            row = jnp.where(local < 0, 0, jnp.where(local >= nb, nr - 1, r))
            return layer, row, jnp.clip(local, 0, nb - 1)
        return index

    return pl.pallas_call(
        functools.partial(_cast_concat_kernel, starts=starts, nblks=nblks),
        grid=(sum(nblks), nr),
        in_specs=[pl.BlockSpec((None, tr, bw), in_map(s0, nb)) for s0, nb in zip(starts, nblks)],
        out_specs=pl.BlockSpec((tr, bw), lambda p, r: (r, p)),
        out_shape=jax.ShapeDtypeStruct((R, sum(nblks) * bw), BF16),
        compiler_params=_params(("arbitrary", "arbitrary")),
        name="cast_concat",
    )(*ws)


def _side_cast_plan(groups, grid):
    nsteps = math.prod(grid)

    def lin(*ids):
        l = ids[0]
        for n, i in zip(grid[1:], ids[1:]):
            l = l * n + i
        return l

    in_specs, args, out_specs, out_shapes, widths = [], [], [], [], []
    for ws, layer in groups:
        R = ws[0].shape[1]
        rows = BF16_ROWS
        while R // rows > nsteps:
            rows *= 2
        assert R % rows == 0
        nblk = R // rows
        rep = nsteps // nblk

        def blk(*ids, rep=rep, nblk=nblk):
            return jnp.minimum(lin(*ids) // rep, nblk - 1)

        for w in ws:
            in_specs.append(pl.BlockSpec((None, rows, w.shape[2]),
                                         lambda *ids, blk=blk, layer=layer: (layer, blk(*ids), 0)))
            args.append(w)
        cols = sum(w.shape[2] for w in ws)
        out_specs.append(pl.BlockSpec((rows, cols), lambda *ids, blk=blk: (blk(*ids), 0)))
        out_shapes.append(jax.ShapeDtypeStruct((R, cols), BF16))
        widths.append(tuple(w.shape[2] for w in ws))
    return in_specs, args, out_specs, out_shapes, tuple(widths)


def _side_cast_body(in_refs, out_refs, widths):
    k = 0
    for o_ref, ws in zip(out_refs, widths):
        c0 = 0
        for wd in ws:
            o_ref[:, c0:c0 + wd] = in_refs[k][...].astype(BF16)
            k += 1
            c0 += wd


def _n_cast_in(widths):
    return sum(len(ws) for ws in widths)


def _norm_matmul_kernel(x_ref, g_ref, w_ref, cs_ref, *rest, has_gate, row_chunk, cast_widths):
    rest = list(rest)
    wz_ref = rest.pop(0) if has_gate else None
    cast_in = [rest.pop(0) for _ in range(_n_cast_in(cast_widths))]
    o_ref = rest.pop(0)
    z_ref = rest.pop(0) if has_gate else None
    cast_out = [rest.pop(0) for _ in cast_widths]
    hn_ref, = rest
    tm = x_ref.shape[0]
    _side_cast_body(cast_in, cast_out, cast_widths)

    @pl.when(pl.program_id(1) == 0)
    def _():
        for r0 in range(0, tm, row_chunk):
            y = _rms_rows(x_ref[r0:r0 + row_chunk, :], g_ref[...])
            hn_ref[r0:r0 + row_chunk, :] = y.astype(BF16)
        if has_gate:
            z_ref[...] = jnp.dot(hn_ref[...], wz_ref[...], preferred_element_type=F32)

    acc = jnp.dot(hn_ref[...], w_ref[...], preferred_element_type=F32)
    o_ref[...] = (acc * cs_ref[...]).astype(o_ref.dtype)


def _norm_matmul(x, g, w, colscale, wz=None, *, casts=(), tm=1024, tn=1536):
    T, D = x.shape
    N = w.shape[1]
    tm = _tile(T, tm)
    tn = _tile(N, tn)
    has_gate = wz is not None
    grid = (T // tm, N // tn)
    c_in, c_args, c_out, c_shapes, c_widths = _side_cast_plan(casts, grid)
    in_specs = [
        pl.BlockSpec((tm, D), lambda i, j: (i, 0)),
        pl.BlockSpec((1, D), lambda i, j: (0, 0)),
        pl.BlockSpec((D, tn), lambda i, j: (0, j)),
        pl.BlockSpec((1, tn), lambda i, j: (0, j)),
    ]
    args = [x, g.reshape(1, D), w, colscale.reshape(1, N)]
    out_shape = [jax.ShapeDtypeStruct((T, N), BF16)]
    out_specs = [pl.BlockSpec((tm, tn), lambda i, j: (i, j))]
    if has_gate:
        in_specs.append(pl.BlockSpec((D, LANES), lambda i, j: (0, 0)))
        args.append(wz)
        out_shape.append(jax.ShapeDtypeStruct((T, LANES), F32))
        out_specs.append(pl.BlockSpec((tm, LANES), lambda i, j: (i, 0)))
    res = pl.pallas_call(
        functools.partial(_norm_matmul_kernel, has_gate=has_gate, row_chunk=min(256, tm),
                          cast_widths=c_widths),
        grid=grid,
        in_specs=in_specs + c_in,
        out_specs=out_specs + c_out,
        out_shape=out_shape + c_shapes,
        scratch_shapes=[pltpu.VMEM((tm, D), BF16)],
        compiler_params=_params(("parallel", "arbitrary")),
        name="norm_matmul_gate" if has_gate else "norm_matmul",
    )(*args, *c_args)
    return res


def _gate_cumsum_kernel(z_ref, b_ref, cp_ref, *, heads):
    z = z_ref[...] + b_ref[...]
    c = jnp.minimum(z, 0.0) - jnp.log1p(jnp.exp(-jnp.abs(z)))
    S = c.shape[0]
    row = lax.broadcasted_iota(jnp.int32, c.shape, 0)
    shift = 1
    while shift < S:
        c = c + jnp.where(row >= shift, pltpu.roll(c, shift, axis=0), 0.0)
        shift *= 2
    c = c * LOG2E
    hi = c.astype(BF16).astype(F32)
    r1 = c - hi
    mid = r1.astype(BF16).astype(F32)
    lo = r1 - mid
    lane = lax.broadcasted_iota(jnp.int32, c.shape, 1)
    packed = jnp.where(lane < heads, hi,
                       jnp.where(lane < 2 * heads, pltpu.roll(mid, heads, axis=1),
                                 jnp.where(lane < 3 * heads, pltpu.roll(lo, 2 * heads, axis=1), 0.0)))
    cp_ref[...] = packed.astype(BF16)


def _gate_cumsum(z, b_pad, batch, heads):
    T = z.shape[0]
    S = T // batch
    assert 3 * heads <= LANES
    return pl.pallas_call(
        functools.partial(_gate_cumsum_kernel, heads=heads),
        grid=(batch,),
        in_specs=[pl.BlockSpec((S, LANES), lambda b: (b, 0)),
                  pl.BlockSpec((1, LANES), lambda b: (0, 0))],
        out_specs=pl.BlockSpec((S, LANES), lambda b: (b, 0)),
        out_shape=jax.ShapeDtypeStruct((T, LANES), BF16),
        compiler_params=_params(("parallel",)),
        name="gate_cumsum",
    )(z, b_pad)


ONES_ROWS = 16
N_PIECES = 3


def _piece_selector(head0, hpg, heads, sign, lane0):
    r = lax.broadcasted_iota(jnp.int32, (LANES, hpg * LANES), 0)
    col = lax.broadcasted_iota(jnp.int32, (LANES, hpg * LANES), 1)
    g = col // LANES
    p = col % LANES - lane0
    hit = (p >= 0) & (p < N_PIECES) & (r == head0 + g + heads * p)
    return jnp.where(hit, sign, 0.0).astype(BF16)


def _fox_attn_kernel(q_ref, k_ref, v_ref, cp_ref, *rest, tq, tk, hpg, heads, cast_widths):
    rest = list(rest)
    cast_in = [rest.pop(0) for _ in range(_n_cast_in(cast_widths))]
    o_ref = rest.pop(0)
    cast_out = [rest.pop(0) for _ in cast_widths]
    kx_ref, vt_ref, acc_ref, st_ref, p_ref = rest
    _side_cast_body(cast_in, cast_out, cast_widths)
    hg = pl.program_id(1)
    qi = pl.program_id(2)
    S = k_ref.shape[0]
    dh = FOX_HEAD_DIM
    lane = lax.broadcasted_iota(jnp.int32, (1, hpg * LANES), 1) % LANES
    cols = [slice(g * dh, (g + 1) * dh) for g in range(hpg)]

    @pl.when(qi == 0)
    def _():
        sel = _piece_selector(hg * hpg, hpg, heads, -1.0, 0)
        ones = jnp.where((lane >= N_PIECES) & (lane < 2 * N_PIECES), 1.0, 0.0)
        for g in range(hpg):
            vt_ref[g, dh:, :] = jnp.ones((ONES_ROWS, S), BF16)

        def body(i, carry):
            sl = pl.ds(pl.multiple_of(i * tq, tq), tq)
            ek = (jnp.dot(cp_ref[sl, :], sel, preferred_element_type=F32) + ones).astype(BF16)
            for g in range(hpg):
                kx_ref[g, sl, :dh] = k_ref[sl, cols[g]]
                kx_ref[g, sl, dh:] = ek[:, cols[g]]
                vt_ref[g, :dh, sl] = v_ref[sl, cols[g]].astype(F32).T.astype(BF16)
            return carry
        lax.fori_loop(0, S // tq, body, 0)

    q0 = pl.multiple_of(qi * tq, tq)
    sel_q = _piece_selector(hg * hpg, hpg, heads, 1.0, N_PIECES)
    cref = jnp.dot(cp_ref[pl.ds(q0, ONES_ROWS), :], sel_q, preferred_element_type=F32)[0:1, :]
    eq = (cref + jnp.where(lane < N_PIECES, 1.0, 0.0)).astype(BF16)
    qx = [jnp.concatenate([q_ref[:, cols[g]], jnp.broadcast_to(eq[:, cols[g]], (tq, dh))], axis=1)
          for g in range(hpg)]

    def scores(g, j, diag):
        k0 = pl.multiple_of(j * tk, tk)
        st = lax.dot_general(kx_ref[g, pl.ds(k0, tk), :], qx[g], _NT, preferred_element_type=F32)
        if diag is not None:
            kk = lax.broadcasted_iota(jnp.int32, st.shape, 0) + diag * tk
            qq = lax.broadcasted_iota(jnp.int32, st.shape, 1)
            st = jnp.where(kk <= qq, st, NEG_BIG)
        return st, jnp.max(st, axis=0, keepdims=True)

    def pv_update(g, j, slot, alpha):
        k0 = pl.multiple_of(j * tk, tk)
        pv = jnp.dot(vt_ref[g, :, pl.ds(k0, tk)], p_ref[g, slot], preferred_element_type=F32)
        acc_ref[g] = alpha * acc_ref[g] + pv

    def stage(g, slot, carry, prev_blk, next_blk, next_diag=None):
        m, alpha_prev, cmax = carry
        if prev_blk is not None:
            pv_update(g, prev_blk, 1 - slot, alpha_prev)
        st_next, cmax_next = scores(g, next_blk, next_diag)
        st_ref[g, 1 - slot] = st_next
        m_new = jnp.maximum(m, cmax)
        p_ref[g, slot] = jnp.exp2(st_ref[g, slot] - m_new).astype(BF16)
        return m_new, jnp.exp2(m - m_new), cmax_next

    assert tq == 2 * tk
    d0 = 2 * qi
    carries = []
    for g in range(hpg):
        acc_ref[g] = jnp.zeros(acc_ref.shape[1:], F32)
        st, cmax = scores(g, d0, 0)
        st_ref[g, 0] = st
        c = (jnp.full((1, tq), NEG_BIG, F32), jnp.ones((1, tq), F32), cmax)
        c = stage(g, 0, c, None, d0 + 1, 1)
        carries.append(stage(g, 1, c, d0, 0))

    def pair(i, carries):
        out = []
        for g in range(hpg):
            c = stage(g, 0, carries[g], jnp.where(i == 0, d0 + 1, 2 * i - 1), 2 * i + 1)
            out.append(stage(g, 1, c, 2 * i, 2 * i + 2))
        return tuple(out)

    carries = lax.fori_loop(0, qi, pair, tuple(carries))
    last_blk = jnp.where(qi == 0, d0 + 1, d0 - 1)
    for g in range(hpg):
        pv_update(g, last_blk, 1, carries[g][1])
        ot = acc_ref[g, :dh, :] / acc_ref[g, dh:dh + 1, :]
        o_ref[:, cols[g]] = ot.T.astype(o_ref.dtype)


def _fox_attention(qkv, cp, *, batch, heads, casts=(), tq=512, hpg=4):
    T = qkv.shape[0]
    S = T // batch
    dh = FOX_HEAD_DIM
    tq = min(tq, S)
    tk = tq // 2
    nq = S // tq
    ng = heads // hpg
    w = hpg * dh
    grid = (batch, ng, nq)
    c_in, c_args, c_out, c_shapes, c_widths = _side_cast_plan(casts, grid)
    return pl.pallas_call(
        functools.partial(_fox_attn_kernel, tq=tq, tk=tk, hpg=hpg, heads=heads,
                          cast_widths=c_widths),
        grid=grid,
        in_specs=[
            pl.BlockSpec((tq, w), lambda b, h, i: (b * nq + i, h)),
            pl.BlockSpec((S, w), lambda b, h, i: (b, ng + h)),
            pl.BlockSpec((S, w), lambda b, h, i: (b, 2 * ng + h)),
            pl.BlockSpec((S, LANES), lambda b, h, i: (b, 0)),
        ] + c_in,
        out_specs=[pl.BlockSpec((tq, w), lambda b, h, i: (b * nq + i, h))] + c_out,
        out_shape=[jax.ShapeDtypeStruct((T, heads * dh), BF16)] + c_shapes,
        scratch_shapes=[pltpu.VMEM((hpg, S, 2 * dh), BF16),
                        pltpu.VMEM((hpg, dh + ONES_ROWS, S), BF16),
                        pltpu.VMEM((hpg, dh + ONES_ROWS, tq), F32),
                        pltpu.VMEM((hpg, 2, tk, tq), F32),
                        pltpu.VMEM((hpg, 2, tk, tq), BF16)],
        compiler_params=_params(("parallel", "parallel", "arbitrary")),
        name="fox_attention",
    )(qkv, qkv, qkv, cp, *c_args)


def _matmul_res_kernel(a_ref, w_ref, r_ref, o_ref):
    o_ref[...] = r_ref[...] + jnp.dot(a_ref[...], w_ref[...], preferred_element_type=F32)


def _matmul_res(a, w, res, *, tm=1024, tn=1024):
    T, K = a.shape
    N = w.shape[1]
    tm = _tile(T, tm)
    tn = _tile(N, tn)
    return pl.pallas_call(
        _matmul_res_kernel,
        grid=(T // tm, N // tn),
        in_specs=[pl.BlockSpec((tm, K), lambda i, j: (i, 0)),
                  pl.BlockSpec((K, tn), lambda i, j: (0, j)),
                  pl.BlockSpec((tm, tn), lambda i, j: (i, j))],
        out_specs=pl.BlockSpec((tm, tn), lambda i, j: (i, j)),
        out_shape=jax.ShapeDtypeStruct((T, N), F32),
        compiler_params=_params(("parallel", "arbitrary")),
        name="matmul_res",
    )(a, w, res)


def _mlp_kernel(x_ref, g_ref, wu_ref, wd_ref, *rest, has_final, row_chunk, cast_widths):
    rest = list(rest)
    gf_ref = rest.pop(0) if has_final else None
    cast_in = [rest.pop(0) for _ in range(_n_cast_in(cast_widths))]
    o_ref = rest.pop(0)
    cast_out = [rest.pop(0) for _ in cast_widths]
    hn_ref, = rest
    _side_cast_body(cast_in, cast_out, cast_widths)
    tm = x_ref.shape[0]
    f = pl.program_id(1)
    chunks = [slice(r0, r0 + row_chunk) for r0 in range(0, tm, row_chunk)]

    @pl.when(f == 0)
    def _():
        for sl in chunks:
            x = x_ref[sl, :]
            hn_ref[sl, :] = _rms_rows(x, g_ref[...]).astype(BF16)
            o_ref[sl, :] = x

    u = jnp.dot(hn_ref[...], wu_ref[...], preferred_element_type=F32)
    a = jnp.square(jnp.maximum(u, 0.0)).astype(BF16)
    o_ref[...] += jnp.dot(a, wd_ref[...], preferred_element_type=F32)

    if has_final:
        @pl.when(f == pl.num_programs(1) - 1)
        def _():
            for sl in chunks:
                o_ref[sl, :] = _rms_rows(o_ref[sl, :], gf_ref[...])


def _mlp(x, g, wu, wd, g_final=None, *, casts=(), tm=1024, tf=512):
    T, D = x.shape
    Fdim = wu.shape[1]
    tm = _tile(T, tm)
    tf = _tile(Fdim, tf)
    has_final = g_final is not None
    in_specs = [pl.BlockSpec((tm, D), lambda i, f: (i, 0)),
                pl.BlockSpec((1, D), lambda i, f: (0, 0)),
                pl.BlockSpec((D, tf), lambda i, f: (0, f)),
                pl.BlockSpec((tf, D), lambda i, f: (f, 0))]
    args = [x, g.reshape(1, D), wu, wd]
    if has_final:
        in_specs.append(pl.BlockSpec((1, D), lambda i, f: (0, 0)))
        args.append(g_final.reshape(1, D))
    grid = (T // tm, Fdim // tf)
    c_in, c_args, c_out, c_shapes, c_widths = _side_cast_plan(casts, grid)
    return pl.pallas_call(
        functools.partial(_mlp_kernel, has_final=has_final, row_chunk=min(256, tm),
                          cast_widths=c_widths),
        grid=grid,
        in_specs=in_specs + c_in,
        out_specs=[pl.BlockSpec((tm, D), lambda i, f: (i, 0))] + c_out,
        out_shape=[jax.ShapeDtypeStruct((T, D), F32)] + c_shapes,
        scratch_shapes=[pltpu.VMEM((tm, D), BF16)],
        compiler_params=_params(("parallel", "arbitrary")),
        name="mlp_final" if has_final else "mlp",
    )(*args, *c_args)


def _retention_kernel(q_ref, k_ref, v_ref, g_ref, cos_ref, sin_ref, din_ref, qd_ref, kd_ref,
                      cd_ref, gn_ref, *rest, chunk, nchunks, cast_widths):
    rest = list(rest)
    cast_in = [rest.pop(0) for _ in range(_n_cast_in(cast_widths))]
    o_ref = rest.pop(0)
    cast_out = [rest.pop(0) for _ in cast_widths]
    state_ref, = rest
    _side_cast_body(cast_in, cast_out, cast_widths)

    @pl.when(pl.program_id(2) == 0)
    def _():
        state_ref[...] = jnp.zeros_like(state_ref)

    half = q_ref.shape[1] // 2
    for ci in range(nchunks):
        sl = slice(ci * chunk, (ci + 1) * chunk)
        cos = cos_ref[sl, :]
        sin = sin_ref[sl, :]

        def rot(x):
            x1, x2 = x[:, :half], x[:, half:]
            return jnp.concatenate([x1 * cos - x2 * sin, x1 * sin + x2 * cos], axis=-1)

        q = rot(q_ref[sl, :].astype(F32))
        k = rot(k_ref[sl, :].astype(F32))
        v = v_ref[sl, :]
        state = state_ref[...]

        inner = lax.dot_general(q.astype(BF16), k.astype(BF16), _NT,
                                preferred_element_type=F32) * din_ref[...]
        o = jnp.dot(inner.astype(BF16), v, preferred_element_type=F32)
        o = o + jnp.dot((q * qd_ref[...]).astype(BF16), state.astype(BF16),
                        preferred_element_type=F32)
        kd_t = (k * kd_ref[...]).T.astype(BF16)
        state_ref[...] = cd_ref[...] * state + jnp.dot(kd_t, v, preferred_element_type=F32)

        ms = jnp.mean(o * o, axis=-1, keepdims=True)
        on = o * lax.rsqrt(ms + RMS_EPS) * gn_ref[...]
        gate = g_ref[sl, :].astype(F32)
        gate = gate * jax.nn.sigmoid(gate)
        o_ref[sl, :] = (gate * on).astype(o_ref.dtype)


def _retention_tables(heads, chunk, seq):
    dk, dv = RET_QK_DIM, RET_V_DIM
    half = dk // 2
    pos = np.arange(seq, dtype=np.float64)
    inv = ROPE_BASE ** (-np.arange(half, dtype=np.float64) / half)
    ang = pos[:, None] * inv[None, :]
    log_gamma = np.log1p(-np.exp2(-5.0 - np.arange(heads, dtype=np.float64)))
    idx = np.arange(chunk, dtype=np.float64)
    diff = idx[:, None] - idx[None, :]
    decay_in = np.where(diff >= 0,
                        np.exp(log_gamma[:, None, None] * np.maximum(diff, 0.0)), 0.0)
    q_decay = np.exp(log_gamma[:, None] * (idx + 1.0))[:, :, None]
    k_decay = np.exp(log_gamma[:, None] * (chunk - 1.0 - idx))[:, :, None]
    chunk_decay = np.exp(log_gamma * chunk)[:, None, None]
    tables = (np.cos(ang), np.sin(ang), decay_in,
              np.broadcast_to(q_decay, (heads, chunk, dk)),
              np.broadcast_to(k_decay, (heads, chunk, dk)),
              np.broadcast_to(chunk_decay, (heads, 1, dv)))
    return tuple(jnp.asarray(t, dtype=F32) for t in tables)


def _retention(proj, gn, *, batch, heads, casts=(), chunk=256, rows=512):
    T = proj.shape[0]
    S = T // batch
    dk, dv = RET_QK_DIM, RET_V_DIM
    chunk = min(chunk, S)
    rows = min(rows, S)
    ns = S // rows
    cos, sin, din, qd, kd, cd = _retention_tables(heads, chunk, S)
    vblk0 = 2 * heads * dk // dv
    grid = (batch, heads, ns)
    c_in, c_args, c_out, c_shapes, c_widths = _side_cast_plan(casts, grid)
    return pl.pallas_call(
        functools.partial(_retention_kernel, chunk=chunk, nchunks=rows // chunk,
                          cast_widths=c_widths),
        grid=grid,
        in_specs=[
            pl.BlockSpec((rows, dk), lambda b, h, s: (b * ns + s, h)),
            pl.BlockSpec((rows, dk), lambda b, h, s: (b * ns + s, heads + h)),
            pl.BlockSpec((rows, dv), lambda b, h, s: (b * ns + s, vblk0 + h)),
            pl.BlockSpec((rows, dv), lambda b, h, s: (b * ns + s, vblk0 + heads + h)),
            pl.BlockSpec((rows, dk // 2), lambda b, h, s: (s, 0)),
            pl.BlockSpec((rows, dk // 2), lambda b, h, s: (s, 0)),
            pl.BlockSpec((None, chunk, chunk), lambda b, h, s: (h, 0, 0)),
            pl.BlockSpec((None, chunk, dk), lambda b, h, s: (h, 0, 0)),
            pl.BlockSpec((None, chunk, dk), lambda b, h, s: (h, 0, 0)),
            pl.BlockSpec((None, 1, dv), lambda b, h, s: (h, 0, 0)),
            pl.BlockSpec((1, dv), lambda b, h, s: (0, h)),
        ] + c_in,
        out_specs=[pl.BlockSpec((rows, dv), lambda b, h, s: (b * ns + s, h))] + c_out,
        out_shape=[jax.ShapeDtypeStruct((T, heads * dv), BF16)] + c_shapes,
        scratch_shapes=[pltpu.VMEM((dk, dv), F32)],
        compiler_params=_params(("parallel", "parallel", "arbitrary")),
        name="retention",
    )(proj, proj, proj, proj, cos, sin, din, qd, kd, cd, gn.reshape(1, heads * dv), *c_args)


def kernel(x, fox_norm, fox_wq, fox_wk, fox_wv, fox_wf, fox_bf, fox_wo, ret_norm, ret_wq, ret_wk,
           ret_wv, ret_wg, ret_gn, ret_wo, mlp_norm, mlp_up, mlp_down, final_norm):
    B, S, D = x.shape
    T = B * S
    fox_heads = D // FOX_HEAD_DIM
    ret_heads = D // RET_QK_DIM
    h = x.reshape(T, D)

    w_qkv = _cast_concat([fox_wq, fox_wk, fox_wv], 0)
    cs = jnp.concatenate([jnp.full((D,), LOG2E * FOX_HEAD_DIM ** -0.5, F32),
                          jnp.ones((2 * D,), F32)])
    wf_pad = jnp.pad(fox_wf[0], ((0, 0), (0, LANES - fox_heads))).astype(BF16)
    bf_pad = jnp.pad(fox_bf[0], (0, LANES - fox_heads)).reshape(1, LANES)
    qkv, z = _norm_matmul(h, fox_norm[0], w_qkv, cs, wf_pad)
    cp = _gate_cumsum(z, bf_pad, B, fox_heads)
    attn, wo0, up0, down0 = _fox_attention(
        qkv, cp, batch=B, heads=fox_heads,
        casts=[([fox_wo], 0), ([mlp_up], 0), ([mlp_down], 0)])
    h = _matmul_res(attn, wo0, h)
    h, w_proj = _mlp(h, mlp_norm[0], up0, down0,
                     casts=[([ret_wq, ret_wk, ret_wv, ret_wg], 0)])

    cs = jnp.concatenate([jnp.ones((D,), F32), jnp.full((D,), RET_QK_DIM ** -0.5, F32),
                          jnp.ones((2 * ret_heads * RET_V_DIM,), F32)])
    proj, wo1 = _norm_matmul(h, ret_norm[0], w_proj, cs, casts=[([ret_wo], 0)])
    y, up1, down1 = _retention(proj, ret_gn[0], batch=B, heads=ret_heads,
                               casts=[([mlp_up], 1), ([mlp_down], 1)])
    h = _matmul_res(y, wo1, h)
    h, = _mlp(h, mlp_norm[1], up1, down1, final_norm)
    return h.reshape(B, S, D)
```

```python
import functools
import math

import numpy as np
import jax
import jax.numpy as jnp
from jax import lax
from jax.experimental import pallas as pl
from jax.experimental.pallas import tpu as pltpu

F32 = jnp.float32
BF16 = jnp.bfloat16

RMS_EPS = 1e-6
ROPE_BASE = 10000.0
FOX_HEAD_DIM = 128
RET_QK_DIM = 256
RET_V_DIM = 512
RET_CHUNK = 256
LANES = 128
BF16_ROWS = 16
NEG_BIG = -1e30
LOG2E = math.log2(math.e)
V7X_VMEM_LIMIT = 56 * 1024 * 1024

_NT = (((1,), (1,)), ((), ()))


def _params(sem, vmem=V7X_VMEM_LIMIT):
    return pltpu.CompilerParams(dimension_semantics=sem, vmem_limit_bytes=vmem)


def _tile(n, target, align=LANES):
    if n <= target:
        return n
    t = (target // align) * align
    while n % t:
        t -= align
    return t


def _rms_rows(x, g):
    ms = jnp.mean(x * x, axis=-1, keepdims=True)
    return x * lax.rsqrt(ms + RMS_EPS) * g


def _cast_concat_kernel(*refs, starts, nblks):
    o_ref = refs[-1]
    p = pl.program_id(0)
    for w_ref, s0, nb in zip(refs[:-1], starts, nblks):
        @pl.when((p >= s0) & (p < s0 + nb))
        def _(w_ref=w_ref):
            o_ref[...] = w_ref[...].astype(o_ref.dtype)


def _cast_concat(ws, layer, *, bw=2048, tr=512):
    R = ws[0].shape[1]
    bw = min(bw, min(w.shape[2] for w in ws))
    tr = _tile(R, tr)
    nr = R // tr
    nblks = [w.shape[2] // bw for w in ws]
    starts = [sum(nblks[:i]) for i in range(len(ws))]

    def in_map(s0, nb):
        def index(p, r):
            local = p - s0
            row = jnp.where(local < 0, 0, jnp.where(local >= nb, nr - 1, r))
            return layer, row, jnp.clip(local, 0, nb - 1)
        return index

    return pl.pallas_call(
        functools.partial(_cast_concat_kernel, starts=starts, nblks=nblks),
        grid=(sum(nblks), nr),
        in_specs=[pl.BlockSpec((None, tr, bw), in_map(s0, nb)) for s0, nb in zip(starts, nblks)],
        out_specs=pl.BlockSpec((tr, bw), lambda p, r: (r, p)),
        out_shape=jax.ShapeDtypeStruct((R, sum(nblks) * bw), BF16),
        compiler_params=_params(("arbitrary", "arbitrary")),
        name="cast_concat",
    )(*ws)


def _side_cast_plan(groups, grid):
    nsteps = math.prod(grid)

    def lin(*ids):
        l = ids[0]
        for n, i in zip(grid[1:], ids[1:]):
            l = l * n + i
        return l

    in_specs, args, out_specs, out_shapes, widths = [], [], [], [], []
    for ws, layer in groups:
        R = ws[0].shape[1]
        rows = BF16_ROWS
        while R // rows > nsteps:
            rows *= 2
        assert R % rows == 0
        nblk = R // rows
        rep = nsteps // nblk

        def blk(*ids, rep=rep, nblk=nblk):
            return jnp.minimum(lin(*ids) // rep, nblk - 1)

        for w in ws:
            in_specs.append(pl.BlockSpec((None, rows, w.shape[2]),
                                         lambda *ids, blk=blk, layer=layer: (layer, blk(*ids), 0)))
            args.append(w)
        cols = sum(w.shape[2] for w in ws)
        out_specs.append(pl.BlockSpec((rows, cols), lambda *ids, blk=blk: (blk(*ids), 0)))
        out_shapes.append(jax.ShapeDtypeStruct((R, cols), BF16))
        widths.append(tuple(w.shape[2] for w in ws))
    return in_specs, args, out_specs, out_shapes, tuple(widths)


def _side_cast_body(in_refs, out_refs, widths):
    k = 0
    for o_ref, ws in zip(out_refs, widths):
        c0 = 0
        for wd in ws:
            o_ref[:, c0:c0 + wd] = in_refs[k][...].astype(BF16)
            k += 1
            c0 += wd


def _n_cast_in(widths):
    return sum(len(ws) for ws in widths)


def _norm_matmul_kernel(x_ref, g_ref, w_ref, cs_ref, *rest, has_gate, ret, row_chunk, cast_widths):
    rest = list(rest)
    wz_ref = rest.pop(0) if has_gate else None
    if ret is not None:
        cos_ref, sin_ref, lg_ref = rest.pop(0), rest.pop(0), rest.pop(0)
    cast_in = [rest.pop(0) for _ in range(_n_cast_in(cast_widths))]
    o_ref = rest.pop(0)
    z_ref = rest.pop(0) if has_gate else None
    cast_out = [rest.pop(0) for _ in cast_widths]
    hn_ref, = rest
    tm = x_ref.shape[0]
    _side_cast_body(cast_in, cast_out, cast_widths)

    @pl.when(pl.program_id(1) == 0)
    def _():
        for r0 in range(0, tm, row_chunk):
            y = _rms_rows(x_ref[r0:r0 + row_chunk, :], g_ref[...])
            hn_ref[r0:r0 + row_chunk, :] = y.astype(BF16)
        if has_gate:
            z_ref[...] = jnp.dot(hn_ref[...], wz_ref[...], preferred_element_type=F32)

    if ret is None:
        acc = jnp.dot(hn_ref[...], w_ref[...], preferred_element_type=F32)
        o_ref[...] = (acc * cs_ref[...]).astype(o_ref.dtype)
        return

    n_rot, n_plain, chunk = ret
    j = pl.program_id(1)
    tn = w_ref.shape[1]
    half = RET_QK_DIM // 2

    def heads(epilogue):
        for c0 in range(0, tn, RET_QK_DIM):
            acc = jnp.dot(hn_ref[...], w_ref[:, c0:c0 + RET_QK_DIM], preferred_element_type=F32)
            epilogue(c0, acc)

    @pl.when(j < n_rot)
    def _():
        t = (lax.broadcasted_iota(jnp.int32, (tm, half), 0) % chunk + 1).astype(F32)
        cos = cos_ref[...]
        sin = sin_ref[...]

        def rotary(c0, acc):
            h1 = slice(c0, c0 + half)
            h2 = slice(c0 + half, c0 + 2 * half)
            x1 = acc[:, :half]
            x2 = acc[:, half:]
            scale = jnp.exp(t * lg_ref[:, h1]) * cs_ref[:, h1]
            o_ref[:, h1] = ((x1 * cos - x2 * sin) * scale).astype(o_ref.dtype)
            o_ref[:, h2] = ((x1 * sin + x2 * cos) * scale).astype(o_ref.dtype)
        heads(rotary)

    @pl.when((j >= n_rot) & (j < n_rot + n_plain))
    def _():
        def plain(c0, acc):
            cols = slice(c0, c0 + RET_QK_DIM)
            o_ref[:, cols] = (acc * cs_ref[:, cols]).astype(o_ref.dtype)
        heads(plain)

    @pl.when(j >= n_rot + n_plain)
    def _():
        def swish(c0, acc):
            cols = slice(c0, c0 + RET_QK_DIM)
            y = acc * cs_ref[:, cols]
            o_ref[:, cols] = (y * jax.nn.sigmoid(y)).astype(o_ref.dtype)
        heads(swish)


def _norm_matmul(x, g, w, colscale, wz=None, *, ret=None, casts=(), tm=1024, tn=1536):
    T, D = x.shape
    N = w.shape[1]
    tm = _tile(T, tm)
    tn = _tile(N, tn)
    has_gate = wz is not None
    grid = (T // tm, N // tn)
    c_in, c_args, c_out, c_shapes, c_widths = _side_cast_plan(casts, grid)
    in_specs = [
        pl.BlockSpec((tm, D), lambda i, j: (i, 0)),
        pl.BlockSpec((1, D), lambda i, j: (0, 0)),
        pl.BlockSpec((D, tn), lambda i, j: (0, j)),
        pl.BlockSpec((1, tn), lambda i, j: (0, j)),
    ]
    args = [x, g.reshape(1, D), w, colscale.reshape(1, N)]
    out_shape = [jax.ShapeDtypeStruct((T, N), BF16)]
    out_specs = [pl.BlockSpec((tm, tn), lambda i, j: (i, j))]
    if has_gate:
        in_specs.append(pl.BlockSpec((D, LANES), lambda i, j: (0, 0)))
        args.append(wz)
        out_shape.append(jax.ShapeDtypeStruct((T, LANES), F32))
        out_specs.append(pl.BlockSpec((tm, LANES), lambda i, j: (i, 0)))
    ret_static = None
    if ret is not None:
        cos, sin, lgcol, rot_cols, plain_cols, chunk = ret
        assert rot_cols % tn == 0 and plain_cols % tn == 0 and tm % chunk == 0
        nrt = cos.shape[0] // tm
        half = RET_QK_DIM // 2
        in_specs += [pl.BlockSpec((tm, half), lambda i, j: (i % nrt, 0)),
                     pl.BlockSpec((tm, half), lambda i, j: (i % nrt, 0)),
                     pl.BlockSpec((1, tn), lambda i, j: (0, j))]
        args += [cos, sin, lgcol.reshape(1, N)]
        ret_static = (rot_cols // tn, plain_cols // tn, chunk)
    res = pl.pallas_call(
        functools.partial(_norm_matmul_kernel, has_gate=has_gate, ret=ret_static,
                          row_chunk=min(256, tm), cast_widths=c_widths),
        grid=grid,
        in_specs=in_specs + c_in,
        out_specs=out_specs + c_out,
        out_shape=out_shape + c_shapes,
        scratch_shapes=[pltpu.VMEM((tm, D), BF16)],
        compiler_params=_params(("parallel", "arbitrary")),
        name="norm_matmul_gate" if has_gate else "norm_matmul",
    )(*args, *c_args)
    return res


def _gate_cumsum_kernel(z_ref, b_ref, cp_ref, *, heads):
    z = z_ref[...] + b_ref[...]
    c = jnp.minimum(z, 0.0) - jnp.log1p(jnp.exp(-jnp.abs(z)))
    S = c.shape[0]
    row = lax.broadcasted_iota(jnp.int32, c.shape, 0)
    shift = 1
    while shift < S:
        c = c + jnp.where(row >= shift, pltpu.roll(c, shift, axis=0), 0.0)
        shift *= 2
    c = c * LOG2E
    hi = c.astype(BF16).astype(F32)
    r1 = c - hi
    mid = r1.astype(BF16).astype(F32)
    lo = r1 - mid
    lane = lax.broadcasted_iota(jnp.int32, c.shape, 1)
    packed = jnp.where(lane < heads, hi,
                       jnp.where(lane < 2 * heads, pltpu.roll(mid, heads, axis=1),
                                 jnp.where(lane < 3 * heads, pltpu.roll(lo, 2 * heads, axis=1), 0.0)))
    cp_ref[...] = packed.astype(BF16)


def _gate_cumsum(z, b_pad, batch, heads):
    T = z.shape[0]
    S = T // batch
    assert 3 * heads <= LANES
    return pl.pallas_call(
        functools.partial(_gate_cumsum_kernel, heads=heads),
        grid=(batch,),
        in_specs=[pl.BlockSpec((S, LANES), lambda b: (b, 0)),
                  pl.BlockSpec((1, LANES), lambda b: (0, 0))],
        out_specs=pl.BlockSpec((S, LANES), lambda b: (b, 0)),
        out_shape=jax.ShapeDtypeStruct((T, LANES), BF16),
        compiler_params=_params(("parallel",)),
        name="gate_cumsum",
    )(z, b_pad)


ONES_ROWS = 16
N_PIECES = 3


def _piece_selector(head0, hpg, heads, sign, lane0):
    r = lax.broadcasted_iota(jnp.int32, (LANES, hpg * LANES), 0)
    col = lax.broadcasted_iota(jnp.int32, (LANES, hpg * LANES), 1)
    g = col // LANES
    p = col % LANES - lane0
    hit = (p >= 0) & (p < N_PIECES) & (r == head0 + g + heads * p)
    return jnp.where(hit, sign, 0.0).astype(BF16)


def _fox_attn_kernel(q_ref, k_ref, v_ref, cp_ref, *rest, tq, tk, hpg, heads, cast_widths):
    rest = list(rest)
    cast_in = [rest.pop(0) for _ in range(_n_cast_in(cast_widths))]
    o_ref = rest.pop(0)
    cast_out = [rest.pop(0) for _ in cast_widths]
    kx_ref, vt_ref, acc_ref, st_ref, p_ref = rest
    _side_cast_body(cast_in, cast_out, cast_widths)
    hg = pl.program_id(1)
    qi = pl.program_id(2)
    S = k_ref.shape[0]
    dh = FOX_HEAD_DIM
    lane = lax.broadcasted_iota(jnp.int32, (1, hpg * LANES), 1) % LANES
    cols = [slice(g * dh, (g + 1) * dh) for g in range(hpg)]

    @pl.when(qi == 0)
    def _():
        sel = _piece_selector(hg * hpg, hpg, heads, -1.0, 0)
        ones = jnp.where((lane >= N_PIECES) & (lane < 2 * N_PIECES), 1.0, 0.0)
        for g in range(hpg):
            vt_ref[g, dh:, :] = jnp.ones((ONES_ROWS, S), BF16)

        def body(i, carry):
            sl = pl.ds(pl.multiple_of(i * tq, tq), tq)
            ek = (jnp.dot(cp_ref[sl, :], sel, preferred_element_type=F32) + ones).astype(BF16)
            for g in range(hpg):
                kx_ref[g, sl, :dh] = k_ref[sl, cols[g]]
                kx_ref[g, sl, dh:] = ek[:, cols[g]]
                vt_ref[g, :dh, sl] = v_ref[sl, cols[g]].astype(F32).T.astype(BF16)
            return carry
        lax.fori_loop(0, S // tq, body, 0)

    q0 = pl.multiple_of(qi * tq, tq)
    sel_q = _piece_selector(hg * hpg, hpg, heads, 1.0, N_PIECES)
    cref = jnp.dot(cp_ref[pl.ds(q0, ONES_ROWS), :], sel_q, preferred_element_type=F32)[0:1, :]
    eq = (cref + jnp.where(lane < N_PIECES, 1.0, 0.0)).astype(BF16)
    qx = [jnp.concatenate([q_ref[:, cols[g]], jnp.broadcast_to(eq[:, cols[g]], (tq, dh))], axis=1)
          for g in range(hpg)]

    def scores(g, j, diag):
        k0 = pl.multiple_of(j * tk, tk)
        st = lax.dot_general(kx_ref[g, pl.ds(k0, tk), :], qx[g], _NT, preferred_element_type=F32)
        if diag is not None:
            kk = lax.broadcasted_iota(jnp.int32, st.shape, 0) + diag * tk
            qq = lax.broadcasted_iota(jnp.int32, st.shape, 1)
            st = jnp.where(kk <= qq, st, NEG_BIG)
        return st, jnp.max(st, axis=0, keepdims=True)

    def pv_update(g, j, slot, alpha):
        k0 = pl.multiple_of(j * tk, tk)
        pv = jnp.dot(vt_ref[g, :, pl.ds(k0, tk)], p_ref[g, slot], preferred_element_type=F32)
        acc_ref[g] = alpha * acc_ref[g] + pv

    def stage(g, slot, carry, prev_blk, next_blk, next_diag=None):
        m, alpha_prev, cmax = carry
        if prev_blk is not None:
            pv_update(g, prev_blk, 1 - slot, alpha_prev)
        st_next, cmax_next = scores(g, next_blk, next_diag)
        st_ref[g, 1 - slot] = st_next
        m_new = jnp.maximum(m, cmax)
        p_ref[g, slot] = jnp.exp2(st_ref[g, slot] - m_new).astype(BF16)
        return m_new, jnp.exp2(m - m_new), cmax_next

    assert tq == 2 * tk
    d0 = 2 * qi
    carries = []
    for g in range(hpg):
        acc_ref[g] = jnp.zeros(acc_ref.shape[1:], F32)
        st, cmax = scores(g, d0, 0)
        st_ref[g, 0] = st
        c = (jnp.full((1, tq), NEG_BIG, F32), jnp.ones((1, tq), F32), cmax)
        c = stage(g, 0, c, None, d0 + 1, 1)
        carries.append(stage(g, 1, c, d0, 0))

    def pair(i, carries):
        out = []
        for g in range(hpg):
            c = stage(g, 0, carries[g], jnp.where(i == 0, d0 + 1, 2 * i - 1), 2 * i + 1)
            out.append(stage(g, 1, c, 2 * i, 2 * i + 2))
        return tuple(out)

    carries = lax.fori_loop(0, qi, pair, tuple(carries))
    last_blk = jnp.where(qi == 0, d0 + 1, d0 - 1)
    for g in range(hpg):
        pv_update(g, last_blk, 1, carries[g][1])
        ot = acc_ref[g, :dh, :] / acc_ref[g, dh:dh + 1, :]
        o_ref[:, cols[g]] = ot.T.astype(o_ref.dtype)


def _fox_attention(qkv, cp, *, batch, heads, casts=(), tq=512, hpg=4):
    T = qkv.shape[0]
    S = T // batch
    dh = FOX_HEAD_DIM
    tq = min(tq, S)
    tk = tq // 2
    nq = S // tq
    ng = heads // hpg
    w = hpg * dh
    grid = (batch, ng, nq)
    c_in, c_args, c_out, c_shapes, c_widths = _side_cast_plan(casts, grid)
    return pl.pallas_call(
        functools.partial(_fox_attn_kernel, tq=tq, tk=tk, hpg=hpg, heads=heads,
                          cast_widths=c_widths),
        grid=grid,
        in_specs=[
            pl.BlockSpec((tq, w), lambda b, h, i: (b * nq + i, h)),
            pl.BlockSpec((S, w), lambda b, h, i: (b, ng + h)),
            pl.BlockSpec((S, w), lambda b, h, i: (b, 2 * ng + h)),
            pl.BlockSpec((S, LANES), lambda b, h, i: (b, 0)),
        ] + c_in,
        out_specs=[pl.BlockSpec((tq, w), lambda b, h, i: (b * nq + i, h))] + c_out,
        out_shape=[jax.ShapeDtypeStruct((T, heads * dh), BF16)] + c_shapes,
        scratch_shapes=[pltpu.VMEM((hpg, S, 2 * dh), BF16),
                        pltpu.VMEM((hpg, dh + ONES_ROWS, S), BF16),
                        pltpu.VMEM((hpg, dh + ONES_ROWS, tq), F32),
                        pltpu.VMEM((hpg, 2, tk, tq), F32),
                        pltpu.VMEM((hpg, 2, tk, tq), BF16)],
        compiler_params=_params(("parallel", "parallel", "arbitrary")),
        name="fox_attention",
    )(qkv, qkv, qkv, cp, *c_args)


def _matmul_res_kernel(a_ref, w_ref, r_ref, o_ref):
    o_ref[...] = r_ref[...] + jnp.dot(a_ref[...], w_ref[...], preferred_element_type=F32)


def _matmul_res(a, w, res, *, tm=1024, tn=1024):
    T, K = a.shape
    N = w.shape[1]
    tm = _tile(T, tm)
    tn = _tile(N, tn)
    return pl.pallas_call(
        _matmul_res_kernel,
        grid=(T // tm, N // tn),
        in_specs=[pl.BlockSpec((tm, K), lambda i, j: (i, 0)),
                  pl.BlockSpec((K, tn), lambda i, j: (0, j)),
                  pl.BlockSpec((tm, tn), lambda i, j: (i, j))],
        out_specs=pl.BlockSpec((tm, tn), lambda i, j: (i, j)),
        out_shape=jax.ShapeDtypeStruct((T, N), F32),
        compiler_params=_params(("parallel", "arbitrary")),
        name="matmul_res",
    )(a, w, res)


def _mlp_kernel(x_ref, g_ref, wu_ref, wd_ref, *rest, has_final, row_chunk, cast_widths):
    rest = list(rest)
    gf_ref = rest.pop(0) if has_final else None
    cast_in = [rest.pop(0) for _ in range(_n_cast_in(cast_widths))]
    o_ref = rest.pop(0)
    cast_out = [rest.pop(0) for _ in cast_widths]
    hn_ref, = rest
    _side_cast_body(cast_in, cast_out, cast_widths)
    tm = x_ref.shape[0]
    f = pl.program_id(1)
    chunks = [slice(r0, r0 + row_chunk) for r0 in range(0, tm, row_chunk)]

    @pl.when(f == 0)
    def _():
        for sl in chunks:
            x = x_ref[sl, :]
            hn_ref[sl, :] = _rms_rows(x, g_ref[...]).astype(BF16)
            o_ref[sl, :] = x

    u = jnp.dot(hn_ref[...], wu_ref[...], preferred_element_type=F32)
    a = jnp.square(jnp.maximum(u, 0.0)).astype(BF16)
    o_ref[...] += jnp.dot(a, wd_ref[...], preferred_element_type=F32)

    if has_final:
        @pl.when(f == pl.num_programs(1) - 1)
        def _():
            for sl in chunks:
                o_ref[sl, :] = _rms_rows(o_ref[sl, :], gf_ref[...])


def _mlp(x, g, wu, wd, g_final=None, *, casts=(), tm=1024, tf=512):
    T, D = x.shape
    Fdim = wu.shape[1]
    tm = _tile(T, tm)
    tf = _tile(Fdim, tf)
    has_final = g_final is not None
    in_specs = [pl.BlockSpec((tm, D), lambda i, f: (i, 0)),
                pl.BlockSpec((1, D), lambda i, f: (0, 0)),
                pl.BlockSpec((D, tf), lambda i, f: (0, f)),
                pl.BlockSpec((tf, D), lambda i, f: (f, 0))]
    args = [x, g.reshape(1, D), wu, wd]
    if has_final:
        in_specs.append(pl.BlockSpec((1, D), lambda i, f: (0, 0)))
        args.append(g_final.reshape(1, D))
    grid = (T // tm, Fdim // tf)
    c_in, c_args, c_out, c_shapes, c_widths = _side_cast_plan(casts, grid)
    return pl.pallas_call(
        functools.partial(_mlp_kernel, has_final=has_final, row_chunk=min(256, tm),
                          cast_widths=c_widths),
        grid=grid,
        in_specs=in_specs + c_in,
        out_specs=[pl.BlockSpec((tm, D), lambda i, f: (i, 0))] + c_out,
        out_shape=[jax.ShapeDtypeStruct((T, D), F32)] + c_shapes,
        scratch_shapes=[pltpu.VMEM((tm, D), BF16)],
        compiler_params=_params(("parallel", "arbitrary")),
        name="mlp_final" if has_final else "mlp",
    )(*args, *c_args)


def _retention_kernel(q_ref, k_ref, v_ref, sg_ref, cd_ref, gn_ref, *rest, chunk, nchunks,
                      cast_widths):
    rest = list(rest)
    cast_in = [rest.pop(0) for _ in range(_n_cast_in(cast_widths))]
    o_ref = rest.pop(0)
    cast_out = [rest.pop(0) for _ in cast_widths]
    state_ref, = rest
    _side_cast_body(cast_in, cast_out, cast_widths)

    @pl.when(pl.program_id(2) == 0)
    def _():
        state_ref[...] = jnp.zeros_like(state_ref)

    causal = (lax.broadcasted_iota(jnp.int32, (chunk, chunk), 0)
              >= lax.broadcasted_iota(jnp.int32, (chunk, chunk), 1))
    for ci in range(nchunks):
        sl = slice(ci * chunk, (ci + 1) * chunk)
        q = q_ref[sl, :]
        k = k_ref[sl, :]
        v = v_ref[sl, :]
        state = state_ref[...]

        inner = lax.dot_general(q, k, _NT, preferred_element_type=F32)
        inner = jnp.where(causal, inner, 0.0).astype(BF16)
        o = jnp.dot(inner, v, preferred_element_type=F32)
        o = o + jnp.dot(q, state.astype(BF16), preferred_element_type=F32)
        k_t = k.astype(F32).T.astype(BF16)
        state_ref[...] = cd_ref[...] * (state + jnp.dot(k_t, v, preferred_element_type=F32))

        ms = jnp.mean(o * o, axis=-1, keepdims=True)
        on = o * lax.rsqrt(ms + RMS_EPS) * gn_ref[...]
        o_ref[sl, :] = (sg_ref[sl, :].astype(F32) * on).astype(o_ref.dtype)


def _retention_tables(heads, chunk, seq):
    dk, dv = RET_QK_DIM, RET_V_DIM
    half = dk // 2
    pos = np.arange(seq, dtype=np.float64)
    inv = ROPE_BASE ** (-np.arange(half, dtype=np.float64) / half)
    ang = pos[:, None] * inv[None, :]
    log_gamma = np.log1p(-np.exp2(-5.0 - np.arange(heads, dtype=np.float64)))
    chunk_decay = np.broadcast_to(np.exp(log_gamma * chunk)[:, None, None], (heads, 1, dv))
    return (jnp.asarray(np.cos(ang), dtype=F32), jnp.asarray(np.sin(ang), dtype=F32),
            log_gamma, jnp.asarray(chunk_decay, dtype=F32))


def _retention(proj, gn, cd, *, batch, heads, chunk, casts=(), rows=1024):
    T = proj.shape[0]
    S = T // batch
    dk, dv = RET_QK_DIM, RET_V_DIM
    rows = min(rows, S)
    ns = S // rows
    vblk0 = 2 * heads * dk // dv
    grid = (batch, heads, ns)
    c_in, c_args, c_out, c_shapes, c_widths = _side_cast_plan(casts, grid)
    return pl.pallas_call(
        functools.partial(_retention_kernel, chunk=chunk, nchunks=rows // chunk,
                          cast_widths=c_widths),
        grid=grid,
        in_specs=[
            pl.BlockSpec((rows, dk), lambda b, h, s: (b * ns + s, h)),
            pl.BlockSpec((rows, dk), lambda b, h, s: (b * ns + s, heads + h)),
            pl.BlockSpec((rows, dv), lambda b, h, s: (b * ns + s, vblk0 + h)),
            pl.BlockSpec((rows, dv), lambda b, h, s: (b * ns + s, vblk0 + heads + h)),
            pl.BlockSpec((None, 1, dv), lambda b, h, s: (h, 0, 0)),
            pl.BlockSpec((1, dv), lambda b, h, s: (0, h)),
        ] + c_in,
        out_specs=[pl.BlockSpec((rows, dv), lambda b, h, s: (b * ns + s, h))] + c_out,
        out_shape=[jax.ShapeDtypeStruct((T, heads * dv), BF16)] + c_shapes,
        scratch_shapes=[pltpu.VMEM((dk, dv), F32)],
        compiler_params=_params(("parallel", "parallel", "arbitrary")),
        name="retention",
    )(proj, proj, proj, proj, cd, gn.reshape(1, heads * dv), *c_args)


def kernel(x, fox_norm, fox_wq, fox_wk, fox_wv, fox_wf, fox_bf, fox_wo, ret_norm, ret_wq, ret_wk,
           ret_wv, ret_wg, ret_gn, ret_wo, mlp_norm, mlp_up, mlp_down, final_norm):
    B, S, D = x.shape
    T = B * S
    fox_heads = D // FOX_HEAD_DIM
    ret_heads = D // RET_QK_DIM
    h = x.reshape(T, D)

    w_qkv = _cast_concat([fox_wq, fox_wk, fox_wv], 0)
    cs = jnp.concatenate([jnp.full((D,), LOG2E * FOX_HEAD_DIM ** -0.5, F32),
                          jnp.ones((2 * D,), F32)])
    wf_pad = jnp.pad(fox_wf[0], ((0, 0), (0, LANES - fox_heads))).astype(BF16)
    bf_pad = jnp.pad(fox_bf[0], (0, LANES - fox_heads)).reshape(1, LANES)
    qkv, z = _norm_matmul(h, fox_norm[0], w_qkv, cs, wf_pad)
    cp = _gate_cumsum(z, bf_pad, B, fox_heads)
    attn, wo0, up0, down0 = _fox_attention(
        qkv, cp, batch=B, heads=fox_heads,
        casts=[([fox_wo], 0), ([mlp_up], 0), ([mlp_down], 0)])
    h = _matmul_res(attn, wo0, h)
    h, w_proj = _mlp(h, mlp_norm[0], up0, down0,
                     casts=[([ret_wq, ret_wk, ret_wv, ret_wg], 0)])

    chunk = min(RET_CHUNK, S)
    cos, sin, log_gamma, cd = _retention_tables(ret_heads, chunk, S)
    rv = ret_heads * RET_V_DIM
    cs = jnp.concatenate([jnp.ones((D,), F32), jnp.full((D,), RET_QK_DIM ** -0.5, F32),
                          jnp.ones((2 * rv,), F32)])
    lg_cols = np.repeat(log_gamma, RET_QK_DIM)
    lgcol = jnp.asarray(np.concatenate([lg_cols, -lg_cols, np.zeros(2 * rv)]), dtype=F32)
    proj, wo1 = _norm_matmul(h, ret_norm[0], w_proj, cs, casts=[([ret_wo], 0)],
                             ret=(cos, sin, lgcol, 2 * D, rv, chunk), tn=1024)
    y, up1, down1 = _retention(proj, ret_gn[0], cd, batch=B, heads=ret_heads, chunk=chunk,
                               casts=[([mlp_up], 1), ([mlp_down], 1)])
    h = _matmul_res(y, wo1, h)
    h, = _mlp(h, mlp_norm[1], up1, down1, final_norm)
    return h.reshape(B, S, D)
```

```python
import functools
import math

import numpy as np
import jax
import jax.numpy as jnp
from jax import lax
from jax.experimental import pallas as pl
from jax.experimental.pallas import tpu as pltpu

F32 = jnp.float32
BF16 = jnp.bfloat16

RMS_EPS = 1e-6
ROPE_BASE = 10000.0
FOX_HEAD_DIM = 128
RET_QK_DIM = 256
RET_V_DIM = 512
RET_CHUNK = 256
LANES = 128
BF16_ROWS = 16
NEG_BIG = -1e30
LOG2E = math.log2(math.e)
V7X_VMEM_LIMIT = 56 * 1024 * 1024

_NT = (((1,), (1,)), ((), ()))


def _params(sem, vmem=V7X_VMEM_LIMIT):
    return pltpu.CompilerParams(dimension_semantics=sem, vmem_limit_bytes=vmem)


def _tile(n, target, align=LANES):
    if n <= target:
        return n
    t = (target // align) * align
    while n % t:
        t -= align
    return t


def _rms_rows(x, g):
    ms = jnp.mean(x * x, axis=-1, keepdims=True)
    return x * lax.rsqrt(ms + RMS_EPS) * g


def _cast_concat_kernel(*refs, starts, nblks):
    o_ref = refs[-1]
    p = pl.program_id(0)
    for w_ref, s0, nb in zip(refs[:-1], starts, nblks):
        @pl.when((p >= s0) & (p < s0 + nb))
        def _(w_ref=w_ref):
            o_ref[...] = w_ref[...].astype(o_ref.dtype)


def _cast_concat(ws, layer, *, bw=2048, tr=512):
    R = ws[0].shape[1]
    bw = min(bw, min(w.shape[2] for w in ws))
    tr = _tile(R, tr)
    nr = R // tr
    nblks = [w.shape[2] // bw for w in ws]
    starts = [sum(nblks[:i]) for i in range(len(ws))]

    def in_map(s0, nb):
        def index(p, r):
            local = p - s0
            row = jnp.where(local < 0, 0, jnp.where(local >= nb, nr - 1, r))
            return layer, row, jnp.clip(local, 0, nb - 1)
        return index

    return pl.pallas_call(
        functools.partial(_cast_concat_kernel, starts=starts, nblks=nblks),
        grid=(sum(nblks), nr),
        in_specs=[pl.BlockSpec((None, tr, bw), in_map(s0, nb)) for s0, nb in zip(starts, nblks)],
        out_specs=pl.BlockSpec((tr, bw), lambda p, r: (r, p)),
        out_shape=jax.ShapeDtypeStruct((R, sum(nblks) * bw), BF16),
        compiler_params=_params(("arbitrary", "arbitrary")),
        name="cast_concat",
    )(*ws)


def _side_cast_plan(groups, grid):
    nsteps = math.prod(grid)

    def lin(*ids):
        l = ids[0]
        for n, i in zip(grid[1:], ids[1:]):
            l = l * n + i
        return l

    in_specs, args, out_specs, out_shapes, widths = [], [], [], [], []
    for ws, layer in groups:
        R = ws[0].shape[1]
        rows = BF16_ROWS
        while R // rows > nsteps:
            rows *= 2
        assert R % rows == 0
        nblk = R // rows
        rep = nsteps // nblk

        def blk(*ids, rep=rep, nblk=nblk):
            return jnp.minimum(lin(*ids) // rep, nblk - 1)

        for w in ws:
            in_specs.append(pl.BlockSpec((None, rows, w.shape[2]),
                                         lambda *ids, blk=blk, layer=layer: (layer, blk(*ids), 0)))
            args.append(w)
        cols = sum(w.shape[2] for w in ws)
        out_specs.append(pl.BlockSpec((rows, cols), lambda *ids, blk=blk: (blk(*ids), 0)))
        out_shapes.append(jax.ShapeDtypeStruct((R, cols), BF16))
        widths.append(tuple(w.shape[2] for w in ws))
    return in_specs, args, out_specs, out_shapes, tuple(widths)


def _side_cast_body(in_refs, out_refs, widths):
    k = 0
    for o_ref, ws in zip(out_refs, widths):
        c0 = 0
        for wd in ws:
            o_ref[:, c0:c0 + wd] = in_refs[k][...].astype(BF16)
            k += 1
            c0 += wd


def _n_cast_in(widths):
    return sum(len(ws) for ws in widths)


def _norm_matmul_kernel(x_ref, g_ref, w_ref, cs_ref, *rest, has_gate, ret, row_chunk, cast_widths):
    rest = list(rest)
    wz_ref = rest.pop(0) if has_gate else None
    if ret is not None:
        cos_ref, sin_ref, lg_ref = rest.pop(0), rest.pop(0), rest.pop(0)
    cast_in = [rest.pop(0) for _ in range(_n_cast_in(cast_widths))]
    o_ref = rest.pop(0)
    z_ref = rest.pop(0) if has_gate else None
    cast_out = [rest.pop(0) for _ in cast_widths]
    hn_ref, = rest
    tm = x_ref.shape[0]
    _side_cast_body(cast_in, cast_out, cast_widths)

    @pl.when(pl.program_id(1) == 0)
    def _():
        for r0 in range(0, tm, row_chunk):
            y = _rms_rows(x_ref[r0:r0 + row_chunk, :], g_ref[...])
            hn_ref[r0:r0 + row_chunk, :] = y.astype(BF16)
        if has_gate:
            z_ref[...] = jnp.dot(hn_ref[...], wz_ref[...], preferred_element_type=F32)

    tn = w_ref.shape[1]

    def heads(epilogue):
        for c0 in range(0, tn, RET_QK_DIM):
            acc = jnp.dot(hn_ref[...], w_ref[:, c0:c0 + RET_QK_DIM], preferred_element_type=F32)
            epilogue(c0, acc)

    def plain(c0, acc):
        cols = slice(c0, c0 + RET_QK_DIM)
        o_ref[:, cols] = (acc * cs_ref[:, cols]).astype(o_ref.dtype)

    if ret is None:
        heads(plain)
        return

    n_rot, n_plain, chunk = ret
    j = pl.program_id(1)
    half = RET_QK_DIM // 2

    @pl.when(j < n_rot)
    def _():
        t = (lax.broadcasted_iota(jnp.int32, (tm, half), 0) % chunk + 1).astype(F32)
        cos = cos_ref[...]
        sin = sin_ref[...]

        def rotary(c0, acc):
            h1 = slice(c0, c0 + half)
            h2 = slice(c0 + half, c0 + 2 * half)
            x1 = acc[:, :half]
            x2 = acc[:, half:]
            scale = jnp.exp(t * lg_ref[:, h1]) * cs_ref[:, h1]
            o_ref[:, h1] = ((x1 * cos - x2 * sin) * scale).astype(o_ref.dtype)
            o_ref[:, h2] = ((x1 * sin + x2 * cos) * scale).astype(o_ref.dtype)
        heads(rotary)

    @pl.when((j >= n_rot) & (j < n_rot + n_plain))
    def _():
        heads(plain)

    @pl.when(j >= n_rot + n_plain)
    def _():
        def swish(c0, acc):
            cols = slice(c0, c0 + RET_QK_DIM)
            y = acc * cs_ref[:, cols]
            sig = 0.5 * jnp.tanh(0.5 * y) + 0.5
            o_ref[:, cols] = (y * sig).astype(o_ref.dtype)
        heads(swish)


def _norm_matmul(x, g, w, colscale, wz=None, *, ret=None, casts=(), tm=1024, tn=2048):
    T, D = x.shape
    N = w.shape[1]
    tm = _tile(T, tm)
    tn = _tile(N if ret is None else math.gcd(ret[3], ret[4]), tn)
    has_gate = wz is not None
    grid = (T // tm, N // tn)
    c_in, c_args, c_out, c_shapes, c_widths = _side_cast_plan(casts, grid)
    in_specs = [
        pl.BlockSpec((tm, D), lambda i, j: (i, 0)),
        pl.BlockSpec((1, D), lambda i, j: (0, 0)),
        pl.BlockSpec((D, tn), lambda i, j: (0, j)),
        pl.BlockSpec((1, tn), lambda i, j: (0, j)),
    ]
    args = [x, g.reshape(1, D), w, colscale.reshape(1, N)]
    out_shape = [jax.ShapeDtypeStruct((T, N), BF16)]
    out_specs = [pl.BlockSpec((tm, tn), lambda i, j: (i, j))]
    if has_gate:
        in_specs.append(pl.BlockSpec((D, LANES), lambda i, j: (0, 0)))
        args.append(wz)
        out_shape.append(jax.ShapeDtypeStruct((T, LANES), F32))
        out_specs.append(pl.BlockSpec((tm, LANES), lambda i, j: (i, 0)))
    ret_static = None
    if ret is not None:
        cos, sin, lgcol, rot_cols, plain_cols, chunk = ret
        assert rot_cols % tn == 0 and plain_cols % tn == 0 and tm % chunk == 0
        nrt = cos.shape[0] // tm
        half = RET_QK_DIM // 2
        in_specs += [pl.BlockSpec((tm, half), lambda i, j: (i % nrt, 0)),
                     pl.BlockSpec((tm, half), lambda i, j: (i % nrt, 0)),
                     pl.BlockSpec((1, tn), lambda i, j: (0, j))]
        args += [cos, sin, lgcol.reshape(1, N)]
        ret_static = (rot_cols // tn, plain_cols // tn, chunk)
    res = pl.pallas_call(
        functools.partial(_norm_matmul_kernel, has_gate=has_gate, ret=ret_static,
                          row_chunk=min(256, tm), cast_widths=c_widths),
        grid=grid,
        in_specs=in_specs + c_in,
        out_specs=out_specs + c_out,
        out_shape=out_shape + c_shapes,
        scratch_shapes=[pltpu.VMEM((tm, D), BF16)],
        compiler_params=_params(("parallel", "arbitrary")),
        name="norm_matmul_gate" if has_gate else "norm_matmul",
    )(*args, *c_args)
    return res


def _gate_cumsum_kernel(z_ref, b_ref, cp_ref, *, heads):
    z = z_ref[...] + b_ref[...]
    c = jnp.minimum(z, 0.0) - jnp.log1p(jnp.exp(-jnp.abs(z)))
    S = c.shape[0]
    row = lax.broadcasted_iota(jnp.int32, c.shape, 0)
    shift = 1
    while shift < S:
        c = c + jnp.where(row >= shift, pltpu.roll(c, shift, axis=0), 0.0)
        shift *= 2
    c = c * LOG2E
    hi = c.astype(BF16).astype(F32)
    r1 = c - hi
    mid = r1.astype(BF16).astype(F32)
    lo = r1 - mid
    lane = lax.broadcasted_iota(jnp.int32, c.shape, 1)
    packed = jnp.where(lane < heads, hi,
                       jnp.where(lane < 2 * heads, pltpu.roll(mid, heads, axis=1),
                                 jnp.where(lane < 3 * heads, pltpu.roll(lo, 2 * heads, axis=1), 0.0)))
    cp_ref[...] = packed.astype(BF16)


def _gate_cumsum(z, b_pad, batch, heads):
    T = z.shape[0]
    S = T // batch
    assert 3 * heads <= LANES
    return pl.pallas_call(
        functools.partial(_gate_cumsum_kernel, heads=heads),
        grid=(batch,),
        in_specs=[pl.BlockSpec((S, LANES), lambda b: (b, 0)),
                  pl.BlockSpec((1, LANES), lambda b: (0, 0))],
        out_specs=pl.BlockSpec((S, LANES), lambda b: (b, 0)),
        out_shape=jax.ShapeDtypeStruct((T, LANES), BF16),
        compiler_params=_params(("parallel",)),
        name="gate_cumsum",
    )(z, b_pad)


ONES_ROWS = 16
N_PIECES = 3


def _piece_selector(head0, hpg, heads, sign, lane0):
    r = lax.broadcasted_iota(jnp.int32, (LANES, hpg * LANES), 0)
    col = lax.broadcasted_iota(jnp.int32, (LANES, hpg * LANES), 1)
    g = col // LANES
    p = col % LANES - lane0
    hit = (p >= 0) & (p < N_PIECES) & (r == head0 + g + heads * p)
    return jnp.where(hit, sign, 0.0).astype(BF16)


def _fox_attn_kernel(q_ref, k_ref, v_ref, cp_ref, *rest, tq, tk, hpg, heads, cast_widths):
    rest = list(rest)
    cast_in = [rest.pop(0) for _ in range(_n_cast_in(cast_widths))]
    o_ref = rest.pop(0)
    cast_out = [rest.pop(0) for _ in cast_widths]
    kx_ref, vt_ref, acc_ref, st_ref, p_ref = rest
    _side_cast_body(cast_in, cast_out, cast_widths)
    hg = pl.program_id(1)
    qi = pl.program_id(2)
    S = k_ref.shape[0]
    dh = FOX_HEAD_DIM
    lane = lax.broadcasted_iota(jnp.int32, (1, hpg * LANES), 1) % LANES
    cols = [slice(g * dh, (g + 1) * dh) for g in range(hpg)]

    @pl.when(qi == 0)
    def _():
        sel = _piece_selector(hg * hpg, hpg, heads, -1.0, 0)
        ones = jnp.where((lane >= N_PIECES) & (lane < 2 * N_PIECES), 1.0, 0.0)
        for g in range(hpg):
            vt_ref[g, dh:, :] = jnp.ones((ONES_ROWS, S), BF16)

        def body(i, carry):
            sl = pl.ds(pl.multiple_of(i * tq, tq), tq)
            ek = (jnp.dot(cp_ref[sl, :], sel, preferred_element_type=F32) + ones).astype(BF16)
            for g in range(hpg):
                kx_ref[g, sl, :dh] = k_ref[sl, cols[g]]
                kx_ref[g, sl, dh:] = ek[:, cols[g]]
                vt_ref[g, :dh, sl] = v_ref[sl, cols[g]].astype(F32).T.astype(BF16)
            return carry
        lax.fori_loop(0, S // tq, body, 0)

    q0 = pl.multiple_of(qi * tq, tq)
    sel_q = _piece_selector(hg * hpg, hpg, heads, 1.0, N_PIECES)
    cref = jnp.dot(cp_ref[pl.ds(q0, ONES_ROWS), :], sel_q, preferred_element_type=F32)[0:1, :]
    eq = (cref + jnp.where(lane < N_PIECES, 1.0, 0.0)).astype(BF16)
    qx = [jnp.concatenate([q_ref[:, cols[g]], jnp.broadcast_to(eq[:, cols[g]], (tq, dh))], axis=1)
          for g in range(hpg)]

    def scores(g, j, diag):
        k0 = pl.multiple_of(j * tk, tk)
        st = lax.dot_general(kx_ref[g, pl.ds(k0, tk), :], qx[g], _NT, preferred_element_type=F32)
        if diag is not None:
            kk = lax.broadcasted_iota(jnp.int32, st.shape, 0) + diag * tk
            qq = lax.broadcasted_iota(jnp.int32, st.shape, 1)
            st = jnp.where(kk <= qq, st, NEG_BIG)
        return st, jnp.max(st, axis=0, keepdims=True)

    def pv_update(g, j, slot, alpha):
        k0 = pl.multiple_of(j * tk, tk)
        pv = jnp.dot(vt_ref[g, :, pl.ds(k0, tk)], p_ref[g, slot], preferred_element_type=F32)
        acc_ref[g] = alpha * acc_ref[g] + pv

    def stage(g, slot, carry, prev_blk, next_blk, next_diag=None):
        m, alpha_prev, cmax = carry
        if prev_blk is not None:
            pv_update(g, prev_blk, 1 - slot, alpha_prev)
        st_next, cmax_next = scores(g, next_blk, next_diag)
        st_ref[g, 1 - slot] = st_next
        m_new = jnp.maximum(m, cmax)
        p_ref[g, slot] = jnp.exp2(st_ref[g, slot] - m_new).astype(BF16)
        return m_new, jnp.exp2(m - m_new), cmax_next

    assert tq == 2 * tk
    d0 = 2 * qi
    carries = []
    for g in range(hpg):
        acc_ref[g] = jnp.zeros(acc_ref.shape[1:], F32)
        st, cmax = scores(g, d0, 0)
        st_ref[g, 0] = st
        c = (jnp.full((1, tq), NEG_BIG, F32), jnp.ones((1, tq), F32), cmax)
        c = stage(g, 0, c, None, d0 + 1, 1)
        carries.append(stage(g, 1, c, d0, 0))

    def pair(i, carries):
        out = []
        for g in range(hpg):
            c = stage(g, 0, carries[g], jnp.where(i == 0, d0 + 1, 2 * i - 1), 2 * i + 1)
            out.append(stage(g, 1, c, 2 * i, 2 * i + 2))
        return tuple(out)

    carries = lax.fori_loop(0, qi, pair, tuple(carries))
    last_blk = jnp.where(qi == 0, d0 + 1, d0 - 1)
    for g in range(hpg):
        pv_update(g, last_blk, 1, carries[g][1])
        ot = acc_ref[g, :dh, :] / acc_ref[g, dh:dh + 1, :]
        o_ref[:, cols[g]] = ot.T.astype(o_ref.dtype)


def _fox_attention(qkv, cp, *, batch, heads, casts=(), tq=512, hpg=4):
    T = qkv.shape[0]
    S = T // batch
    dh = FOX_HEAD_DIM
    tq = min(tq, S)
    tk = tq // 2
    nq = S // tq
    ng = heads // hpg
    w = hpg * dh
    grid = (batch, ng, nq)
    c_in, c_args, c_out, c_shapes, c_widths = _side_cast_plan(casts, grid)
    return pl.pallas_call(
        functools.partial(_fox_attn_kernel, tq=tq, tk=tk, hpg=hpg, heads=heads,
                          cast_widths=c_widths),
        grid=grid,
        in_specs=[
            pl.BlockSpec((tq, w), lambda b, h, i: (b * nq + i, h)),
            pl.BlockSpec((S, w), lambda b, h, i: (b, ng + h)),
            pl.BlockSpec((S, w), lambda b, h, i: (b, 2 * ng + h)),
            pl.BlockSpec((S, LANES), lambda b, h, i: (b, 0)),
        ] + c_in,
        out_specs=[pl.BlockSpec((tq, w), lambda b, h, i: (b * nq + i, h))] + c_out,
        out_shape=[jax.ShapeDtypeStruct((T, heads * dh), BF16)] + c_shapes,
        scratch_shapes=[pltpu.VMEM((hpg, S, 2 * dh), BF16),
                        pltpu.VMEM((hpg, dh + ONES_ROWS, S), BF16),
                        pltpu.VMEM((hpg, dh + ONES_ROWS, tq), F32),
                        pltpu.VMEM((hpg, 2, tk, tq), F32),
                        pltpu.VMEM((hpg, 2, tk, tq), BF16)],
        compiler_params=_params(("parallel", "parallel", "arbitrary")),
        name="fox_attention",
    )(qkv, qkv, qkv, cp, *c_args)


def _matmul_res_kernel(a_ref, w_ref, r_ref, o_ref):
    o_ref[...] = r_ref[...] + jnp.dot(a_ref[...], w_ref[...], preferred_element_type=F32)


def _matmul_res(a, w, res, *, tm=1024, tn=1024):
    T, K = a.shape
    N = w.shape[1]
    tm = _tile(T, tm)
    tn = _tile(N, tn)
    return pl.pallas_call(
        _matmul_res_kernel,
        grid=(T // tm, N // tn),
        in_specs=[pl.BlockSpec((tm, K), lambda i, j: (i, 0)),
                  pl.BlockSpec((K, tn), lambda i, j: (0, j)),
                  pl.BlockSpec((tm, tn), lambda i, j: (i, j))],
        out_specs=pl.BlockSpec((tm, tn), lambda i, j: (i, j)),
        out_shape=jax.ShapeDtypeStruct((T, N), F32),
        compiler_params=_params(("parallel", "arbitrary")),
        name="matmul_res",
    )(a, w, res)


def _mlp_kernel(x_ref, g_ref, wu_ref, wd_ref, *rest, has_final, row_chunk, cast_widths):
    rest = list(rest)
    gf_ref = rest.pop(0) if has_final else None
    cast_in = [rest.pop(0) for _ in range(_n_cast_in(cast_widths))]
    o_ref = rest.pop(0)
    cast_out = [rest.pop(0) for _ in cast_widths]
    hn_ref, = rest
    _side_cast_body(cast_in, cast_out, cast_widths)
    tm = x_ref.shape[0]
    f = pl.program_id(1)
    chunks = [slice(r0, r0 + row_chunk) for r0 in range(0, tm, row_chunk)]

    @pl.when(f == 0)
    def _():
        for sl in chunks:
            x = x_ref[sl, :]
            hn_ref[sl, :] = _rms_rows(x, g_ref[...]).astype(BF16)
            o_ref[sl, :] = x

    u = jnp.dot(hn_ref[...], wu_ref[...], preferred_element_type=F32)
    a = jnp.square(jnp.maximum(u, 0.0)).astype(BF16)
    o_ref[...] += jnp.dot(a, wd_ref[...], preferred_element_type=F32)

    if has_final:
        @pl.when(f == pl.num_programs(1) - 1)
        def _():
            for sl in chunks:
                o_ref[sl, :] = _rms_rows(o_ref[sl, :], gf_ref[...])


def _mlp(x, g, wu, wd, g_final=None, *, casts=(), tm=1024, tf=512):
    T, D = x.shape
    Fdim = wu.shape[1]
    tm = _tile(T, tm)
    tf = _tile(Fdim, tf)
    has_final = g_final is not None
    in_specs = [pl.BlockSpec((tm, D), lambda i, f: (i, 0)),
                pl.BlockSpec((1, D), lambda i, f: (0, 0)),
                pl.BlockSpec((D, tf), lambda i, f: (0, f)),
                pl.BlockSpec((tf, D), lambda i, f: (f, 0))]
    args = [x, g.reshape(1, D), wu, wd]
    if has_final:
        in_specs.append(pl.BlockSpec((1, D), lambda i, f: (0, 0)))
        args.append(g_final.reshape(1, D))
    grid = (T // tm, Fdim // tf)
    c_in, c_args, c_out, c_shapes, c_widths = _side_cast_plan(casts, grid)
    return pl.pallas_call(
        functools.partial(_mlp_kernel, has_final=has_final, row_chunk=min(256, tm),
                          cast_widths=c_widths),
        grid=grid,
        in_specs=in_specs + c_in,
        out_specs=[pl.BlockSpec((tm, D), lambda i, f: (i, 0))] + c_out,
        out_shape=[jax.ShapeDtypeStruct((T, D), F32)] + c_shapes,
        scratch_shapes=[pltpu.VMEM((tm, D), BF16)],
        compiler_params=_params(("parallel", "arbitrary")),
        name="mlp_final" if has_final else "mlp",
    )(*args, *c_args)


def _retention_kernel(q_ref, k_ref, v_ref, sg_ref, cd_ref, gn_ref, *rest, chunk, nchunks,
                      cast_widths):
    rest = list(rest)
    cast_in = [rest.pop(0) for _ in range(_n_cast_in(cast_widths))]
    o_ref = rest.pop(0)
    cast_out = [rest.pop(0) for _ in cast_widths]
    state_ref, = rest
    _side_cast_body(cast_in, cast_out, cast_widths)

    @pl.when(pl.program_id(2) == 0)
    def _():
        state_ref[...] = jnp.zeros_like(state_ref)

    causal = (lax.broadcasted_iota(jnp.int32, (chunk, chunk), 0)
              >= lax.broadcasted_iota(jnp.int32, (chunk, chunk), 1))
    for ci in range(nchunks):
        sl = slice(ci * chunk, (ci + 1) * chunk)
        q = q_ref[sl, :]
        k = k_ref[sl, :]
        v = v_ref[sl, :]
        state = state_ref[...]

        inner = lax.dot_general(q, k, _NT, preferred_element_type=F32)
        inner = jnp.where(causal, inner, 0.0).astype(BF16)
        o = jnp.dot(inner, v, preferred_element_type=F32)
        o = o + jnp.dot(q, state.astype(BF16), preferred_element_type=F32)
        k_t = k.astype(F32).T.astype(BF16)
        state_ref[...] = cd_ref[...] * (state + jnp.dot(k_t, v, preferred_element_type=F32))

        ms = jnp.mean(o * o, axis=-1, keepdims=True)
        on = o * lax.rsqrt(ms + RMS_EPS) * gn_ref[...]
        o_ref[sl, :] = (sg_ref[sl, :].astype(F32) * on).astype(o_ref.dtype)


def _retention_tables(heads, chunk, seq):
    dk, dv = RET_QK_DIM, RET_V_DIM
    half = dk // 2
    pos = np.arange(seq, dtype=np.float64)
    inv = ROPE_BASE ** (-np.arange(half, dtype=np.float64) / half)
    ang = pos[:, None] * inv[None, :]
    log_gamma = np.log1p(-np.exp2(-5.0 - np.arange(heads, dtype=np.float64)))
    chunk_decay = np.broadcast_to(np.exp(log_gamma * chunk)[:, None, None], (heads, 1, dv))
    return (jnp.asarray(np.cos(ang), dtype=F32), jnp.asarray(np.sin(ang), dtype=F32),
            log_gamma, jnp.asarray(chunk_decay, dtype=F32))


def _retention(proj, gn, cd, *, batch, heads, chunk, casts=(), rows=1024):
    T = proj.shape[0]
    S = T // batch
    dk, dv = RET_QK_DIM, RET_V_DIM
    rows = min(rows, S)
    ns = S // rows
    vblk0 = 2 * heads * dk // dv
    grid = (batch, heads, ns)
    c_in, c_args, c_out, c_shapes, c_widths = _side_cast_plan(casts, grid)
    return pl.pallas_call(
        functools.partial(_retention_kernel, chunk=chunk, nchunks=rows // chunk,
                          cast_widths=c_widths),
        grid=grid,
        in_specs=[
            pl.BlockSpec((rows, dk), lambda b, h, s: (b * ns + s, h)),
            pl.BlockSpec((rows, dk), lambda b, h, s: (b * ns + s, heads + h)),
            pl.BlockSpec((rows, dv), lambda b, h, s: (b * ns + s, vblk0 + h)),
            pl.BlockSpec((rows, dv), lambda b, h, s: (b * ns + s, vblk0 + heads + h)),
            pl.BlockSpec((None, 1, dv), lambda b, h, s: (h, 0, 0)),
            pl.BlockSpec((1, dv), lambda b, h, s: (0, h)),
        ] + c_in,
        out_specs=[pl.BlockSpec((rows, dv), lambda b, h, s: (b * ns + s, h))] + c_out,
        out_shape=[jax.ShapeDtypeStruct((T, heads * dv), BF16)] + c_shapes,
        scratch_shapes=[pltpu.VMEM((dk, dv), F32)],
        compiler_params=_params(("parallel", "parallel", "arbitrary")),
        name="retention",
    )(proj, proj, proj, proj, cd, gn.reshape(1, heads * dv), *c_args)


def kernel(x, fox_norm, fox_wq, fox_wk, fox_wv, fox_wf, fox_bf, fox_wo, ret_norm, ret_wq, ret_wk,
           ret_wv, ret_wg, ret_gn, ret_wo, mlp_norm, mlp_up, mlp_down, final_norm):
    B, S, D = x.shape
    T = B * S
    fox_heads = D // FOX_HEAD_DIM
    ret_heads = D // RET_QK_DIM
    h = x.reshape(T, D)

    w_qkv = _cast_concat([fox_wq, fox_wk, fox_wv], 0)
    cs = jnp.concatenate([jnp.full((D,), LOG2E * FOX_HEAD_DIM ** -0.5, F32),
                          jnp.ones((2 * D,), F32)])
    wf_pad = jnp.pad(fox_wf[0], ((0, 0), (0, LANES - fox_heads))).astype(BF16)
    bf_pad = jnp.pad(fox_bf[0], (0, LANES - fox_heads)).reshape(1, LANES)
    qkv, z = _norm_matmul(h, fox_norm[0], w_qkv, cs, wf_pad)
    cp = _gate_cumsum(z, bf_pad, B, fox_heads)
    attn, wo0, up0, down0 = _fox_attention(
        qkv, cp, batch=B, heads=fox_heads,
        casts=[([fox_wo], 0), ([mlp_up], 0), ([mlp_down], 0)])
    h = _matmul_res(attn, wo0, h)
    h, w_proj = _mlp(h, mlp_norm[0], up0, down0,
                     casts=[([ret_wq, ret_wk, ret_wv, ret_wg], 0)])

    chunk = min(RET_CHUNK, S)
    cos, sin, log_gamma, cd = _retention_tables(ret_heads, chunk, S)
    rv = ret_heads * RET_V_DIM
    cs = jnp.concatenate([jnp.ones((D,), F32), jnp.full((D,), RET_QK_DIM ** -0.5, F32),
                          jnp.ones((2 * rv,), F32)])
    lg_cols = np.repeat(log_gamma, RET_QK_DIM)
    lgcol = jnp.asarray(np.concatenate([lg_cols, -lg_cols, np.zeros(2 * rv)]), dtype=F32)
    proj, wo1 = _norm_matmul(h, ret_norm[0], w_proj, cs, casts=[([ret_wo], 0)],
                             ret=(cos, sin, lgcol, 2 * D, rv, chunk))
    y, up1, down1 = _retention(proj, ret_gn[0], cd, batch=B, heads=ret_heads, chunk=chunk,
                               casts=[([mlp_up], 1), ([mlp_down], 1)])
    h = _matmul_res(y, wo1, h)
    h, = _mlp(h, mlp_norm[1], up1, down1, final_norm)
    return h.reshape(B, S, D)
```

```python
import functools
import math

import numpy as np
import jax
import jax.numpy as jnp
from jax import lax
from jax.experimental import pallas as pl
from jax.experimental.pallas import tpu as pltpu

F32 = jnp.float32
BF16 = jnp.bfloat16

RMS_EPS = 1e-6
ROPE_BASE = 10000.0
FOX_HEAD_DIM = 128
RET_QK_DIM = 256
RET_V_DIM = 512
RET_CHUNK = 256
LANES = 128
BF16_ROWS = 16
NEG_BIG = -1e30
LOG2E = math.log2(math.e)
V7X_VMEM_LIMIT = 56 * 1024 * 1024

_NT = (((1,), (1,)), ((), ()))


def _params(sem, vmem=V7X_VMEM_LIMIT):
    return pltpu.CompilerParams(dimension_semantics=sem, vmem_limit_bytes=vmem)


def _tile(n, target, align=LANES):
    if n <= target:
        return n
    t = (target // align) * align
    while n % t:
        t -= align
    return t


def _rms_rows(x, g):
    ms = jnp.mean(x * x, axis=-1, keepdims=True)
    return x * lax.rsqrt(ms + RMS_EPS) * g


def _cast_concat_kernel(*refs, starts, nblks):
    o_ref = refs[-1]
    p = pl.program_id(0)
    for w_ref, s0, nb in zip(refs[:-1], starts, nblks):
        @pl.when((p >= s0) & (p < s0 + nb))
        def _(w_ref=w_ref):
            o_ref[...] = w_ref[...].astype(o_ref.dtype)


def _cast_concat(ws, layer, *, bw=2048, tr=512):
    R = ws[0].shape[1]
    bw = min(bw, min(w.shape[2] for w in ws))
    tr = _tile(R, tr)
    nr = R // tr
    nblks = [w.shape[2] // bw for w in ws]
    starts = [sum(nblks[:i]) for i in range(len(ws))]

    def in_map(s0, nb):
        def index(p, r):
            local = p - s0
            row = jnp.where(local < 0, 0, jnp.where(local >= nb, nr - 1, r))
            return layer, row, jnp.clip(local, 0, nb - 1)
        return index

    return pl.pallas_call(
        functools.partial(_cast_concat_kernel, starts=starts, nblks=nblks),
        grid=(sum(nblks), nr),
        in_specs=[pl.BlockSpec((None, tr, bw), in_map(s0, nb)) for s0, nb in zip(starts, nblks)],
        out_specs=pl.BlockSpec((tr, bw), lambda p, r: (r, p)),
        out_shape=jax.ShapeDtypeStruct((R, sum(nblks) * bw), BF16),
        compiler_params=_params(("arbitrary", "arbitrary")),
        name="cast_concat",
    )(*ws)


def _side_cast_plan(groups, grid):
    nsteps = math.prod(grid)

    def lin(*ids):
        l = ids[0]
        for n, i in zip(grid[1:], ids[1:]):
            l = l * n + i
        return l

    in_specs, args, out_specs, out_shapes, widths = [], [], [], [], []
    for ws, layer in groups:
        R = ws[0].shape[1]
        rows = BF16_ROWS
        while R // rows > nsteps:
            rows *= 2
        assert R % rows == 0
        nblk = R // rows
        rep = nsteps // nblk

        def blk(*ids, rep=rep, nblk=nblk):
            return jnp.minimum(lin(*ids) // rep, nblk - 1)

        for w in ws:
            in_specs.append(pl.BlockSpec((None, rows, w.shape[2]),
                                         lambda *ids, blk=blk, layer=layer: (layer, blk(*ids), 0)))
            args.append(w)
        cols = sum(w.shape[2] for w in ws)
        out_specs.append(pl.BlockSpec((rows, cols), lambda *ids, blk=blk: (blk(*ids), 0)))
        out_shapes.append(jax.ShapeDtypeStruct((R, cols), BF16))
        widths.append(tuple(w.shape[2] for w in ws))
    return in_specs, args, out_specs, out_shapes, tuple(widths)


def _side_cast_body(in_refs, out_refs, widths):
    k = 0
    for o_ref, ws in zip(out_refs, widths):
        c0 = 0
        for wd in ws:
            o_ref[:, c0:c0 + wd] = in_refs[k][...].astype(BF16)
            k += 1
            c0 += wd


def _n_cast_in(widths):
    return sum(len(ws) for ws in widths)


def _norm_matmul_kernel(x_ref, g_ref, w_ref, cs_ref, *rest, has_gate, ret, row_chunk, cast_widths):
    rest = list(rest)
    wz_ref = rest.pop(0) if has_gate else None
    if ret is not None:
        cos_ref, sin_ref, lg_ref = rest.pop(0), rest.pop(0), rest.pop(0)
    cast_in = [rest.pop(0) for _ in range(_n_cast_in(cast_widths))]
    o_ref = rest.pop(0)
    z_ref = rest.pop(0) if has_gate else None
    cast_out = [rest.pop(0) for _ in cast_widths]
    hn_ref, = rest
    tm = x_ref.shape[0]
    _side_cast_body(cast_in, cast_out, cast_widths)

    @pl.when(pl.program_id(1) == 0)
    def _():
        for r0 in range(0, tm, row_chunk):
            y = _rms_rows(x_ref[r0:r0 + row_chunk, :], g_ref[...])
            hn_ref[r0:r0 + row_chunk, :] = y.astype(BF16)
        if has_gate:
            z_ref[...] = jnp.dot(hn_ref[...], wz_ref[...], preferred_element_type=F32)

    tn = w_ref.shape[1]

    def heads(epilogue):
        for c0 in range(0, tn, RET_QK_DIM):
            acc = jnp.dot(hn_ref[...], w_ref[:, c0:c0 + RET_QK_DIM], preferred_element_type=F32)
            epilogue(c0, acc)

    def plain(c0, acc):
        cols = slice(c0, c0 + RET_QK_DIM)
        o_ref[:, cols] = (acc * cs_ref[:, cols]).astype(o_ref.dtype)

    if ret is None:
        heads(plain)
        return

    n_rot, n_plain, chunk = ret
    j = pl.program_id(1)
    half = RET_QK_DIM // 2

    @pl.when(j < n_rot)
    def _():
        t = (lax.broadcasted_iota(jnp.int32, (tm, half), 0) % chunk + 1).astype(F32)
        cos = cos_ref[...]
        sin = sin_ref[...]

        def rotary(c0, acc):
            h1 = slice(c0, c0 + half)
            h2 = slice(c0 + half, c0 + 2 * half)
            x1 = acc[:, :half]
            x2 = acc[:, half:]
            scale = jnp.exp(t * lg_ref[:, h1]) * cs_ref[:, h1]
            o_ref[:, h1] = ((x1 * cos - x2 * sin) * scale).astype(o_ref.dtype)
            o_ref[:, h2] = ((x1 * sin + x2 * cos) * scale).astype(o_ref.dtype)
        heads(rotary)

    @pl.when((j >= n_rot) & (j < n_rot + n_plain))
    def _():
        heads(plain)

    @pl.when(j >= n_rot + n_plain)
    def _():
        def swish(c0, acc):
            cols = slice(c0, c0 + RET_QK_DIM)
            y = acc * cs_ref[:, cols]
            sig = 0.5 * jnp.tanh(0.5 * y) + 0.5
            o_ref[:, cols] = (y * sig).astype(o_ref.dtype)
        heads(swish)


def _norm_matmul(x, g, w, colscale, wz=None, *, ret=None, casts=(), tm=1024, tn=2048):
    T, D = x.shape
    N = w.shape[1]
    tm = _tile(T, tm)
    tn = _tile(N if ret is None else math.gcd(ret[3], ret[4]), tn)
    has_gate = wz is not None
    grid = (T // tm, N // tn)
    c_in, c_args, c_out, c_shapes, c_widths = _side_cast_plan(casts, grid)
    in_specs = [
        pl.BlockSpec((tm, D), lambda i, j: (i, 0)),
        pl.BlockSpec((1, D), lambda i, j: (0, 0)),
        pl.BlockSpec((D, tn), lambda i, j: (0, j)),
        pl.BlockSpec((1, tn), lambda i, j: (0, j)),
    ]
    args = [x, g.reshape(1, D), w, colscale.reshape(1, N)]
    out_shape = [jax.ShapeDtypeStruct((T, N), BF16)]
    out_specs = [pl.BlockSpec((tm, tn), lambda i, j: (i, j))]
    if has_gate:
        in_specs.append(pl.BlockSpec((D, LANES), lambda i, j: (0, 0)))
        args.append(wz)
        out_shape.append(jax.ShapeDtypeStruct((T, LANES), F32))
        out_specs.append(pl.BlockSpec((tm, LANES), lambda i, j: (i, 0)))
    ret_static = None
    if ret is not None:
        cos, sin, lgcol, rot_cols, plain_cols, chunk = ret
        assert rot_cols % tn == 0 and plain_cols % tn == 0 and tm % chunk == 0
        nrt = cos.shape[0] // tm
        half = RET_QK_DIM // 2
        in_specs += [pl.BlockSpec((tm, half), lambda i, j: (i % nrt, 0)),
                     pl.BlockSpec((tm, half), lambda i, j: (i % nrt, 0)),
                     pl.BlockSpec((1, tn), lambda i, j: (0, j))]
        args += [cos, sin, lgcol.reshape(1, N)]
        ret_static = (rot_cols // tn, plain_cols // tn, chunk)
    res = pl.pallas_call(
        functools.partial(_norm_matmul_kernel, has_gate=has_gate, ret=ret_static,
                          row_chunk=min(256, tm), cast_widths=c_widths),
        grid=grid,
        in_specs=in_specs + c_in,
        out_specs=out_specs + c_out,
        out_shape=out_shape + c_shapes,
        scratch_shapes=[pltpu.VMEM((tm, D), BF16)],
        compiler_params=_params(("parallel", "arbitrary")),
        name="norm_matmul_gate" if has_gate else "norm_matmul",
    )(*args, *c_args)
    return res


def _gate_cumsum_kernel(z_ref, b_ref, cp_ref, *, heads):
    z = z_ref[...] + b_ref[...]
    c = jnp.minimum(z, 0.0) - jnp.log1p(jnp.exp(-jnp.abs(z)))
    S = c.shape[0]
    row = lax.broadcasted_iota(jnp.int32, c.shape, 0)
    shift = 1
    while shift < S:
        c = c + jnp.where(row >= shift, pltpu.roll(c, shift, axis=0), 0.0)
        shift *= 2
    c = c * LOG2E
    hi = c.astype(BF16).astype(F32)
    r1 = c - hi
    mid = r1.astype(BF16).astype(F32)
    lo = r1 - mid
    lane = lax.broadcasted_iota(jnp.int32, c.shape, 1)
    packed = jnp.where(lane < heads, hi,
                       jnp.where(lane < 2 * heads, pltpu.roll(mid, heads, axis=1),
                                 jnp.where(lane < 3 * heads, pltpu.roll(lo, 2 * heads, axis=1), 0.0)))
    cp_ref[...] = packed.astype(BF16)


def _gate_cumsum(z, b_pad, batch, heads):
    T = z.shape[0]
    S = T // batch
    assert 3 * heads <= LANES
    return pl.pallas_call(
        functools.partial(_gate_cumsum_kernel, heads=heads),
        grid=(batch,),
        in_specs=[pl.BlockSpec((S, LANES), lambda b: (b, 0)),
                  pl.BlockSpec((1, LANES), lambda b: (0, 0))],
        out_specs=pl.BlockSpec((S, LANES), lambda b: (b, 0)),
        out_shape=jax.ShapeDtypeStruct((T, LANES), BF16),
        compiler_params=_params(("parallel",)),
        name="gate_cumsum",
    )(z, b_pad)


ONES_ROWS = 16
N_PIECES = 3


def _piece_selector(head0, hpg, heads, sign, lane0):
    r = lax.broadcasted_iota(jnp.int32, (LANES, hpg * LANES), 0)
    col = lax.broadcasted_iota(jnp.int32, (LANES, hpg * LANES), 1)
    g = col // LANES
    p = col % LANES - lane0
    hit = (p >= 0) & (p < N_PIECES) & (r == head0 + g + heads * p)
    return jnp.where(hit, sign, 0.0).astype(BF16)


def _fox_attn_kernel(q_ref, k_ref, v_ref, cp_ref, *rest, tq, tk, hpg, heads, cast_widths):
    rest = list(rest)
    cast_in = [rest.pop(0) for _ in range(_n_cast_in(cast_widths))]
    o_ref = rest.pop(0)
    cast_out = [rest.pop(0) for _ in cast_widths]
    kx_ref, vt_ref, acc_ref, st_ref, p_ref = rest
    _side_cast_body(cast_in, cast_out, cast_widths)
    hg = pl.program_id(1)
    qi = pl.program_id(2)
    S = k_ref.shape[0]
    dh = FOX_HEAD_DIM
    lane = lax.broadcasted_iota(jnp.int32, (1, hpg * LANES), 1) % LANES
    cols = [slice(g * dh, (g + 1) * dh) for g in range(hpg)]

    @pl.when(qi == 0)
    def _():
        sel = _piece_selector(hg * hpg, hpg, heads, -1.0, 0)
        ones = jnp.where((lane >= N_PIECES) & (lane < 2 * N_PIECES), 1.0, 0.0)
        for g in range(hpg):
            vt_ref[g, dh:, :] = jnp.ones((ONES_ROWS, S), BF16)

        def body(i, carry):
            sl = pl.ds(pl.multiple_of(i * tq, tq), tq)
            ek = (jnp.dot(cp_ref[sl, :], sel, preferred_element_type=F32) + ones).astype(BF16)
            for g in range(hpg):
                kx_ref[g, sl, :dh] = k_ref[sl, cols[g]]
                kx_ref[g, sl, dh:] = ek[:, cols[g]]
                vt_ref[g, :dh, sl] = v_ref[sl, cols[g]].astype(F32).T.astype(BF16)
            return carry
        lax.fori_loop(0, S // tq, body, 0)

    q0 = pl.multiple_of(qi * tq, tq)
    sel_q = _piece_selector(hg * hpg, hpg, heads, 1.0, N_PIECES)
    cref = jnp.dot(cp_ref[pl.ds(q0, ONES_ROWS), :], sel_q, preferred_element_type=F32)[0:1, :]
    eq = (cref + jnp.where(lane < N_PIECES, 1.0, 0.0)).astype(BF16)
    qx = [jnp.concatenate([q_ref[:, cols[g]], jnp.broadcast_to(eq[:, cols[g]], (tq, dh))], axis=1)
          for g in range(hpg)]

    def scores(g, j, diag):
        k0 = pl.multiple_of(j * tk, tk)
        st = lax.dot_general(kx_ref[g, pl.ds(k0, tk), :], qx[g], _NT, preferred_element_type=F32)
        if diag is not None:
            kk = lax.broadcasted_iota(jnp.int32, st.shape, 0) + diag * tk
            qq = lax.broadcasted_iota(jnp.int32, st.shape, 1)
            st = jnp.where(kk <= qq, st, NEG_BIG)
        return st, jnp.max(st, axis=0, keepdims=True)

    def pv_update(g, j, slot, alpha):
        k0 = pl.multiple_of(j * tk, tk)
        pv = jnp.dot(vt_ref[g, :, pl.ds(k0, tk)], p_ref[g, slot], preferred_element_type=F32)
        acc_ref[g] = alpha * acc_ref[g] + pv

    def stage(g, slot, carry, prev_blk, next_blk, next_diag=None):
        m, alpha_prev, cmax = carry
        if prev_blk is not None:
            pv_update(g, prev_blk, 1 - slot, alpha_prev)
        st_next, cmax_next = scores(g, next_blk, next_diag)
        st_ref[g, 1 - slot] = st_next
        m_new = jnp.maximum(m, cmax)
        p_ref[g, slot] = jnp.exp2(st_ref[g, slot] - m_new).astype(BF16)
        return m_new, jnp.exp2(m - m_new), cmax_next

    assert tq == 2 * tk
    d0 = 2 * qi
    carries = []
    for g in range(hpg):
        acc_ref[g] = jnp.zeros(acc_ref.shape[1:], F32)
        st, cmax = scores(g, d0, 0)
        st_ref[g, 0] = st
        c = (jnp.full((1, tq), NEG_BIG, F32), jnp.ones((1, tq), F32), cmax)
        c = stage(g, 0, c, None, d0 + 1, 1)
        carries.append(stage(g, 1, c, d0, 0))

    def pair(i, carries):
        out = []
        for g in range(hpg):
            c = stage(g, 0, carries[g], jnp.where(i == 0, d0 + 1, 2 * i - 1), 2 * i + 1)
            out.append(stage(g, 1, c, 2 * i, 2 * i + 2))
        return tuple(out)

    carries = lax.fori_loop(0, qi, pair, tuple(carries))
    last_blk = jnp.where(qi == 0, d0 + 1, d0 - 1)
    for g in range(hpg):
        pv_update(g, last_blk, 1, carries[g][1])
        ot = acc_ref[g, :dh, :] / acc_ref[g, dh:dh + 1, :]
        o_ref[:, cols[g]] = ot.T.astype(o_ref.dtype)


def _fox_attention(qkv, cp, *, batch, heads, casts=(), tq=512, hpg=4):
    T = qkv.shape[0]
    S = T // batch
    dh = FOX_HEAD_DIM
    tq = min(tq, S)
    tk = tq // 2
    nq = S // tq
    ng = heads // hpg
    w = hpg * dh
    grid = (batch, ng, nq)
    c_in, c_args, c_out, c_shapes, c_widths = _side_cast_plan(casts, grid)
    return pl.pallas_call(
        functools.partial(_fox_attn_kernel, tq=tq, tk=tk, hpg=hpg, heads=heads,
                          cast_widths=c_widths),
        grid=grid,
        in_specs=[
            pl.BlockSpec((tq, w), lambda b, h, i: (b * nq + i, h)),
            pl.BlockSpec((S, w), lambda b, h, i: (b, ng + h)),
            pl.BlockSpec((S, w), lambda b, h, i: (b, 2 * ng + h)),
            pl.BlockSpec((S, LANES), lambda b, h, i: (b, 0)),
        ] + c_in,
        out_specs=[pl.BlockSpec((tq, w), lambda b, h, i: (b * nq + i, h))] + c_out,
        out_shape=[jax.ShapeDtypeStruct((T, heads * dh), BF16)] + c_shapes,
        scratch_shapes=[pltpu.VMEM((hpg, S, 2 * dh), BF16),
                        pltpu.VMEM((hpg, dh + ONES_ROWS, S), BF16),
                        pltpu.VMEM((hpg, dh + ONES_ROWS, tq), F32),
                        pltpu.VMEM((hpg, 2, tk, tq), F32),
                        pltpu.VMEM((hpg, 2, tk, tq), BF16)],
        compiler_params=_params(("parallel", "parallel", "arbitrary")),
        name="fox_attention",
    )(qkv, qkv, qkv, cp, *c_args)


MXU_COLS = 256


def _matmul_res_kernel(a_ref, w_ref, r_ref, o_ref):
    for c0 in range(0, w_ref.shape[1], MXU_COLS):
        cols = slice(c0, c0 + MXU_COLS)
        o_ref[:, cols] = r_ref[:, cols] + jnp.dot(a_ref[...], w_ref[:, cols],
                                                  preferred_element_type=F32)


def _matmul_res(a, w, res, *, tm=512):
    T, K = a.shape
    N = w.shape[1]
    tm = _tile(T, tm)
    return pl.pallas_call(
        _matmul_res_kernel,
        grid=(T // tm,),
        in_specs=[pl.BlockSpec((tm, K), lambda i: (i, 0)),
                  pl.BlockSpec((K, N), lambda i: (0, 0), pipeline_mode=pl.Buffered(1)),
                  pl.BlockSpec((tm, N), lambda i: (i, 0))],
        out_specs=pl.BlockSpec((tm, N), lambda i: (i, 0)),
        out_shape=jax.ShapeDtypeStruct((T, N), F32),
        compiler_params=_params(("parallel",)),
        name="matmul_res",
    )(a, w, res)


def _mlp_kernel(x_ref, g_ref, wu_ref, wd_ref, *rest, has_final, row_chunk, cast_widths):
    rest = list(rest)
    gf_ref = rest.pop(0) if has_final else None
    cast_in = [rest.pop(0) for _ in range(_n_cast_in(cast_widths))]
    o_ref = rest.pop(0)
    cast_out = [rest.pop(0) for _ in cast_widths]
    hn_ref, = rest
    _side_cast_body(cast_in, cast_out, cast_widths)
    tm = x_ref.shape[0]
    f = pl.program_id(1)
    chunks = [slice(r0, r0 + row_chunk) for r0 in range(0, tm, row_chunk)]

    @pl.when(f == 0)
    def _():
        for sl in chunks:
            x = x_ref[sl, :]
            hn_ref[sl, :] = _rms_rows(x, g_ref[...]).astype(BF16)
            o_ref[sl, :] = x

    u = jnp.dot(hn_ref[...], wu_ref[...], preferred_element_type=F32)
    a = jnp.square(jnp.maximum(u, 0.0)).astype(BF16)
    o_ref[...] += jnp.dot(a, wd_ref[...], preferred_element_type=F32)

    if has_final:
        @pl.when(f == pl.num_programs(1) - 1)
        def _():
            for sl in chunks:
                o_ref[sl, :] = _rms_rows(o_ref[sl, :], gf_ref[...])


def _mlp(x, g, wu, wd, g_final=None, *, casts=(), tm=1024, tf=1024):
    T, D = x.shape
    Fdim = wu.shape[1]
    tm = _tile(T, tm)
    tf = _tile(Fdim, tf)
    has_final = g_final is not None
    in_specs = [pl.BlockSpec((tm, D), lambda i, f: (i, 0), pipeline_mode=pl.Buffered(1)),
                pl.BlockSpec((1, D), lambda i, f: (0, 0)),
                pl.BlockSpec((D, tf), lambda i, f: (0, f)),
                pl.BlockSpec((tf, D), lambda i, f: (f, 0))]
    args = [x, g.reshape(1, D), wu, wd]
    if has_final:
        in_specs.append(pl.BlockSpec((1, D), lambda i, f: (0, 0)))
        args.append(g_final.reshape(1, D))
    grid = (T // tm, Fdim // tf)
    c_in, c_args, c_out, c_shapes, c_widths = _side_cast_plan(casts, grid)
    return pl.pallas_call(
        functools.partial(_mlp_kernel, has_final=has_final, row_chunk=min(256, tm),
                          cast_widths=c_widths),
        grid=grid,
        in_specs=in_specs + c_in,
        out_specs=[pl.BlockSpec((tm, D), lambda i, f: (i, 0))] + c_out,
        out_shape=[jax.ShapeDtypeStruct((T, D), F32)] + c_shapes,
        scratch_shapes=[pltpu.VMEM((tm, D), BF16)],
        compiler_params=_params(("parallel", "arbitrary")),
        name="mlp_final" if has_final else "mlp",
    )(*args, *c_args)


def _retention_kernel(q_ref, k_ref, v_ref, sg_ref, cd_ref, gn_ref, *rest, chunk, nchunks,
                      cast_widths):
    rest = list(rest)
    cast_in = [rest.pop(0) for _ in range(_n_cast_in(cast_widths))]
    o_ref = rest.pop(0)
    cast_out = [rest.pop(0) for _ in cast_widths]
    state_ref, = rest
    _side_cast_body(cast_in, cast_out, cast_widths)

    @pl.when(pl.program_id(2) == 0)
    def _():
        state_ref[...] = jnp.zeros_like(state_ref)

    causal = (lax.broadcasted_iota(jnp.int32, (chunk, chunk), 0)
              >= lax.broadcasted_iota(jnp.int32, (chunk, chunk), 1))
    for ci in range(nchunks):
        sl = slice(ci * chunk, (ci + 1) * chunk)
        q = q_ref[sl, :]
        k = k_ref[sl, :]
        v = v_ref[sl, :]
        state = state_ref[...]

        inner = lax.dot_general(q, k, _NT, preferred_element_type=F32)
        inner = jnp.where(causal, inner, 0.0).astype(BF16)
        o = jnp.dot(inner, v, preferred_element_type=F32)
        o = o + jnp.dot(q, state.astype(BF16), preferred_element_type=F32)
        k_t = k.astype(F32).T.astype(BF16)
        state_ref[...] = cd_ref[...] * (state + jnp.dot(k_t, v, preferred_element_type=F32))

        ms = jnp.mean(o * o, axis=-1, keepdims=True)
        on = o * lax.rsqrt(ms + RMS_EPS) * gn_ref[...]
        o_ref[sl, :] = (sg_ref[sl, :].astype(F32) * on).astype(o_ref.dtype)


def _retention_tables(heads, chunk, seq):
    dk, dv = RET_QK_DIM, RET_V_DIM
    half = dk // 2
    pos = np.arange(seq, dtype=np.float64)
    inv = ROPE_BASE ** (-np.arange(half, dtype=np.float64) / half)
    ang = pos[:, None] * inv[None, :]
    log_gamma = np.log1p(-np.exp2(-5.0 - np.arange(heads, dtype=np.float64)))
    chunk_decay = np.broadcast_to(np.exp(log_gamma * chunk)[:, None, None], (heads, 1, dv))
    return (jnp.asarray(np.cos(ang), dtype=F32), jnp.asarray(np.sin(ang), dtype=F32),
            log_gamma, jnp.asarray(chunk_decay, dtype=F32))


def _retention(proj, gn, cd, *, batch, heads, chunk, casts=(), rows=1024):
    T = proj.shape[0]
    S = T // batch
    dk, dv = RET_QK_DIM, RET_V_DIM
    rows = min(rows, S)
    ns = S // rows
    vblk0 = 2 * heads * dk // dv
    grid = (batch, heads, ns)
    c_in, c_args, c_out, c_shapes, c_widths = _side_cast_plan(casts, grid)
    return pl.pallas_call(
        functools.partial(_retention_kernel, chunk=chunk, nchunks=rows // chunk,
                          cast_widths=c_widths),
        grid=grid,
        in_specs=[
            pl.BlockSpec((rows, dk), lambda b, h, s: (b * ns + s, h)),
            pl.BlockSpec((rows, dk), lambda b, h, s: (b * ns + s, heads + h)),
            pl.BlockSpec((rows, dv), lambda b, h, s: (b * ns + s, vblk0 + h)),
            pl.BlockSpec((rows, dv), lambda b, h, s: (b * ns + s, vblk0 + heads + h)),
            pl.BlockSpec((None, 1, dv), lambda b, h, s: (h, 0, 0)),
            pl.BlockSpec((1, dv), lambda b, h, s: (0, h)),
        ] + c_in,
        out_specs=[pl.BlockSpec((rows, dv), lambda b, h, s: (b * ns + s, h))] + c_out,
        out_shape=[jax.ShapeDtypeStruct((T, heads * dv), BF16)] + c_shapes,
        scratch_shapes=[pltpu.VMEM((dk, dv), F32)],
        compiler_params=_params(("parallel", "parallel", "arbitrary")),
        name="retention",
    )(proj, proj, proj, proj, cd, gn.reshape(1, heads * dv), *c_args)


def kernel(x, fox_norm, fox_wq, fox_wk, fox_wv, fox_wf, fox_bf, fox_wo, ret_norm, ret_wq, ret_wk,
           ret_wv, ret_wg, ret_gn, ret_wo, mlp_norm, mlp_up, mlp_down, final_norm):
    B, S, D = x.shape
    T = B * S
    fox_heads = D // FOX_HEAD_DIM
    ret_heads = D // RET_QK_DIM
    h = x.reshape(T, D)

    w_qkv = _cast_concat([fox_wq, fox_wk, fox_wv], 0)
    cs = jnp.concatenate([jnp.full((D,), LOG2E * FOX_HEAD_DIM ** -0.5, F32),
                          jnp.ones((2 * D,), F32)])
    wf_pad = jnp.pad(fox_wf[0], ((0, 0), (0, LANES - fox_heads))).astype(BF16)
    bf_pad = jnp.pad(fox_bf[0], (0, LANES - fox_heads)).reshape(1, LANES)
    qkv, z = _norm_matmul(h, fox_norm[0], w_qkv, cs, wf_pad)
    cp = _gate_cumsum(z, bf_pad, B, fox_heads)
    attn, wo0, up0, down0 = _fox_attention(
        qkv, cp, batch=B, heads=fox_heads,
        casts=[([fox_wo], 0), ([mlp_up], 0), ([mlp_down], 0)])
    h = _matmul_res(attn, wo0, h)
    h, w_proj = _mlp(h, mlp_norm[0], up0, down0,
                     casts=[([ret_wq, ret_wk, ret_wv, ret_wg], 0)])

    chunk = min(RET_CHUNK, S)
    cos, sin, log_gamma, cd = _retention_tables(ret_heads, chunk, S)
    rv = ret_heads * RET_V_DIM
    cs = jnp.concatenate([jnp.ones((D,), F32), jnp.full((D,), RET_QK_DIM ** -0.5, F32),
                          jnp.ones((2 * rv,), F32)])
    lg_cols = np.repeat(log_gamma, RET_QK_DIM)
    lgcol = jnp.asarray(np.concatenate([lg_cols, -lg_cols, np.zeros(2 * rv)]), dtype=F32)
    proj, wo1 = _norm_matmul(h, ret_norm[0], w_proj, cs, casts=[([ret_wo], 0)],
                             ret=(cos, sin, lgcol, 2 * D, rv, chunk))
    y, up1, down1 = _retention(proj, ret_gn[0], cd, batch=B, heads=ret_heads, chunk=chunk,
                               casts=[([mlp_up], 1), ([mlp_down], 1)])
    h = _matmul_res(y, wo1, h)
    h, = _mlp(h, mlp_norm[1], up1, down1, final_norm)
    return h.reshape(B, S, D)
```

```python
import functools
import math

import numpy as np
import jax
import jax.numpy as jnp
from jax import lax
from jax.experimental import pallas as pl
from jax.experimental.pallas import tpu as pltpu

F32 = jnp.float32
BF16 = jnp.bfloat16

RMS_EPS = 1e-6
ROPE_BASE = 10000.0
FOX_HEAD_DIM = 128
RET_QK_DIM = 256
RET_V_DIM = 512
RET_CHUNK = 256
LANES = 128
BF16_ROWS = 16
NEG_BIG = -1e30
LOG2E = math.log2(math.e)
V7X_VMEM_LIMIT = 56 * 1024 * 1024

_NT = (((1,), (1,)), ((), ()))


def _params(sem, vmem=V7X_VMEM_LIMIT):
    return pltpu.CompilerParams(dimension_semantics=sem, vmem_limit_bytes=vmem)


def _tile(n, target, align=LANES):
    if n <= target:
        return n
    t = (target // align) * align
    while n % t:
        t -= align
    return t


def _rms_rows(x, g):
    ms = jnp.mean(x * x, axis=-1, keepdims=True)
    return x * lax.rsqrt(ms + RMS_EPS) * g


def _cast_concat_kernel(*refs, starts, nblks):
    o_ref = refs[-1]
    p = pl.program_id(0)
    for w_ref, s0, nb in zip(refs[:-1], starts, nblks):
        @pl.when((p >= s0) & (p < s0 + nb))
        def _(w_ref=w_ref):
            o_ref[...] = w_ref[...].astype(o_ref.dtype)


def _cast_concat(ws, layer, *, bw=2048, tr=512):
    R = ws[0].shape[1]
    bw = min(bw, min(w.shape[2] for w in ws))
    tr = _tile(R, tr)
    nr = R // tr
    nblks = [w.shape[2] // bw for w in ws]
    starts = [sum(nblks[:i]) for i in range(len(ws))]

    def in_map(s0, nb):
        def index(p, r):
            local = p - s0
            row = jnp.where(local < 0, 0, jnp.where(local >= nb, nr - 1, r))
            return layer, row, jnp.clip(local, 0, nb - 1)
        return index

    return pl.pallas_call(
        functools.partial(_cast_concat_kernel, starts=starts, nblks=nblks),
        grid=(sum(nblks), nr),
        in_specs=[pl.BlockSpec((None, tr, bw), in_map(s0, nb)) for s0, nb in zip(starts, nblks)],
        out_specs=pl.BlockSpec((tr, bw), lambda p, r: (r, p)),
        out_shape=jax.ShapeDtypeStruct((R, sum(nblks) * bw), BF16),
        compiler_params=_params(("arbitrary", "arbitrary")),
        name="cast_concat",
    )(*ws)


def _side_cast_plan(groups, grid):
    nsteps = math.prod(grid)

    def lin(*ids):
        l = ids[0]
        for n, i in zip(grid[1:], ids[1:]):
            l = l * n + i
        return l

    in_specs, args, out_specs, out_shapes, widths = [], [], [], [], []
    for ws, layer in groups:
        R = ws[0].shape[1]
        rows = BF16_ROWS
        while R // rows > nsteps:
            rows *= 2
        assert R % rows == 0
        nblk = R // rows
        rep = nsteps // nblk

        def blk(*ids, rep=rep, nblk=nblk):
            return jnp.minimum(lin(*ids) // rep, nblk - 1)

        for w in ws:
            in_specs.append(pl.BlockSpec((None, rows, w.shape[2]),
                                         lambda *ids, blk=blk, layer=layer: (layer, blk(*ids), 0)))
            args.append(w)
        cols = sum(w.shape[2] for w in ws)
        out_specs.append(pl.BlockSpec((rows, cols), lambda *ids, blk=blk: (blk(*ids), 0)))
        out_shapes.append(jax.ShapeDtypeStruct((R, cols), BF16))
        widths.append(tuple(w.shape[2] for w in ws))
    return in_specs, args, out_specs, out_shapes, tuple(widths)


def _side_cast_body(in_refs, out_refs, widths):
    k = 0
    for o_ref, ws in zip(out_refs, widths):
        c0 = 0
        for wd in ws:
            o_ref[:, c0:c0 + wd] = in_refs[k][...].astype(BF16)
            k += 1
            c0 += wd


def _n_cast_in(widths):
    return sum(len(ws) for ws in widths)


def _norm_matmul_kernel(x_ref, g_ref, w_ref, cs_ref, *rest, has_gate, ret, row_chunk, cast_widths):
    rest = list(rest)
    wz_ref = rest.pop(0) if has_gate else None
    if ret is not None:
        cos_ref, sin_ref, lg_ref = rest.pop(0), rest.pop(0), rest.pop(0)
    cast_in = [rest.pop(0) for _ in range(_n_cast_in(cast_widths))]
    o_ref = rest.pop(0)
    z_ref = rest.pop(0) if has_gate else None
    cast_out = [rest.pop(0) for _ in cast_widths]
    hn_ref, = rest
    tm = x_ref.shape[0]
    _side_cast_body(cast_in, cast_out, cast_widths)

    @pl.when(pl.program_id(1) == 0)
    def _():
        for r0 in range(0, tm, row_chunk):
            y = _rms_rows(x_ref[r0:r0 + row_chunk, :], g_ref[...])
            hn_ref[r0:r0 + row_chunk, :] = y.astype(BF16)
        if has_gate:
            z_ref[...] = jnp.dot(hn_ref[...], wz_ref[...], preferred_element_type=F32)

    tn = w_ref.shape[1]

    def heads(epilogue):
        for c0 in range(0, tn, RET_QK_DIM):
            acc = jnp.dot(hn_ref[...], w_ref[:, c0:c0 + RET_QK_DIM], preferred_element_type=F32)
            epilogue(c0, acc)

    def plain(c0, acc):
        cols = slice(c0, c0 + RET_QK_DIM)
        o_ref[:, cols] = (acc * cs_ref[:, cols]).astype(o_ref.dtype)

    if ret is None:
        heads(plain)
        return

    n_rot, n_plain, chunk = ret
    j = pl.program_id(1)
    half = RET_QK_DIM // 2

    @pl.when(j < n_rot)
    def _():
        t = (lax.broadcasted_iota(jnp.int32, (tm, half), 0) % chunk + 1).astype(F32)
        cos = cos_ref[...]
        sin = sin_ref[...]

        def rotary(c0, acc):
            h1 = slice(c0, c0 + half)
            h2 = slice(c0 + half, c0 + 2 * half)
            x1 = acc[:, :half]
            x2 = acc[:, half:]
            scale = jnp.exp(t * lg_ref[:, h1]) * cs_ref[:, h1]
            o_ref[:, h1] = ((x1 * cos - x2 * sin) * scale).astype(o_ref.dtype)
            o_ref[:, h2] = ((x1 * sin + x2 * cos) * scale).astype(o_ref.dtype)
        heads(rotary)

    @pl.when((j >= n_rot) & (j < n_rot + n_plain))
    def _():
        heads(plain)

    @pl.when(j >= n_rot + n_plain)
    def _():
        def swish(c0, acc):
            cols = slice(c0, c0 + RET_QK_DIM)
            y = acc * cs_ref[:, cols]
            sig = 0.5 * jnp.tanh(0.5 * y) + 0.5
            o_ref[:, cols] = (y * sig).astype(o_ref.dtype)
        heads(swish)


def _norm_matmul(x, g, w, colscale, wz=None, *, ret=None, casts=(), tm=1024, tn=2048):
    T, D = x.shape
    N = w.shape[1]
    tm = _tile(T, tm)
    tn = _tile(N if ret is None else math.gcd(ret[3], ret[4]), tn)
    has_gate = wz is not None
    grid = (T // tm, N // tn)
    c_in, c_args, c_out, c_shapes, c_widths = _side_cast_plan(casts, grid)
    in_specs = [
        pl.BlockSpec((tm, D), lambda i, j: (i, 0)),
        pl.BlockSpec((1, D), lambda i, j: (0, 0)),
        pl.BlockSpec((D, tn), lambda i, j: (0, j)),
        pl.BlockSpec((1, tn), lambda i, j: (0, j)),
    ]
    args = [x, g.reshape(1, D), w, colscale.reshape(1, N)]
    out_shape = [jax.ShapeDtypeStruct((T, N), BF16)]
    out_specs = [pl.BlockSpec((tm, tn), lambda i, j: (i, j))]
    if has_gate:
        in_specs.append(pl.BlockSpec((D, LANES), lambda i, j: (0, 0)))
        args.append(wz)
        out_shape.append(jax.ShapeDtypeStruct((T, LANES), F32))
        out_specs.append(pl.BlockSpec((tm, LANES), lambda i, j: (i, 0)))
    ret_static = None
    if ret is not None:
        cos, sin, lgcol, rot_cols, plain_cols, chunk = ret
        assert rot_cols % tn == 0 and plain_cols % tn == 0 and tm % chunk == 0
        nrt = cos.shape[0] // tm
        half = RET_QK_DIM // 2
        in_specs += [pl.BlockSpec((tm, half), lambda i, j: (i % nrt, 0)),
                     pl.BlockSpec((tm, half), lambda i, j: (i % nrt, 0)),
                     pl.BlockSpec((1, tn), lambda i, j: (0, j))]
        args += [cos, sin, lgcol.reshape(1, N)]
        ret_static = (rot_cols // tn, plain_cols // tn, chunk)
    res = pl.pallas_call(
        functools.partial(_norm_matmul_kernel, has_gate=has_gate, ret=ret_static,
                          row_chunk=min(256, tm), cast_widths=c_widths),
        grid=grid,
        in_specs=in_specs + c_in,
        out_specs=out_specs + c_out,
        out_shape=out_shape + c_shapes,
        scratch_shapes=[pltpu.VMEM((tm, D), BF16)],
        compiler_params=_params(("parallel", "arbitrary")),
        name="norm_matmul_gate" if has_gate else "norm_matmul",
    )(*args, *c_args)
    return res


def _gate_cumsum_kernel(z_ref, b_ref, cp_ref, *, heads):
    z = z_ref[...] + b_ref[...]
    c = jnp.minimum(z, 0.0) - jnp.log1p(jnp.exp(-jnp.abs(z)))
    S = c.shape[0]
    row = lax.broadcasted_iota(jnp.int32, c.shape, 0)
    shift = 1
    while shift < S:
        c = c + jnp.where(row >= shift, pltpu.roll(c, shift, axis=0), 0.0)
        shift *= 2
    c = c * LOG2E
    hi = c.astype(BF16).astype(F32)
    r1 = c - hi
    mid = r1.astype(BF16).astype(F32)
    lo = r1 - mid
    lane = lax.broadcasted_iota(jnp.int32, c.shape, 1)
    packed = jnp.where(lane < heads, hi,
                       jnp.where(lane < 2 * heads, pltpu.roll(mid, heads, axis=1),
                                 jnp.where(lane < 3 * heads, pltpu.roll(lo, 2 * heads, axis=1), 0.0)))
    cp_ref[...] = packed.astype(BF16)


def _gate_cumsum(z, b_pad, batch, heads):
    T = z.shape[0]
    S = T // batch
    assert 3 * heads <= LANES
    return pl.pallas_call(
        functools.partial(_gate_cumsum_kernel, heads=heads),
        grid=(batch,),
        in_specs=[pl.BlockSpec((S, LANES), lambda b: (b, 0)),
                  pl.BlockSpec((1, LANES), lambda b: (0, 0))],
        out_specs=pl.BlockSpec((S, LANES), lambda b: (b, 0)),
        out_shape=jax.ShapeDtypeStruct((T, LANES), BF16),
        compiler_params=_params(("parallel",)),
        name="gate_cumsum",
    )(z, b_pad)


ONES_ROWS = 16
N_PIECES = 3


def _piece_selector(head0, hpg, heads, sign, lane0):
    r = lax.broadcasted_iota(jnp.int32, (LANES, hpg * LANES), 0)
    col = lax.broadcasted_iota(jnp.int32, (LANES, hpg * LANES), 1)
    g = col // LANES
    p = col % LANES - lane0
    hit = (p >= 0) & (p < N_PIECES) & (r == head0 + g + heads * p)
    return jnp.where(hit, sign, 0.0).astype(BF16)


def _fox_attn_kernel(q_ref, k_ref, v_ref, cp_ref, *rest, tq, tk, hpg, heads, cast_widths):
    rest = list(rest)
    cast_in = [rest.pop(0) for _ in range(_n_cast_in(cast_widths))]
    o_ref = rest.pop(0)
    cast_out = [rest.pop(0) for _ in cast_widths]
    kx_ref, vt_ref, acc_ref, st_ref, p_ref = rest
    _side_cast_body(cast_in, cast_out, cast_widths)
    hg = pl.program_id(1)
    qi = pl.program_id(2)
    S = k_ref.shape[0]
    dh = FOX_HEAD_DIM
    lane = lax.broadcasted_iota(jnp.int32, (1, hpg * LANES), 1) % LANES
    cols = [slice(g * dh, (g + 1) * dh) for g in range(hpg)]

    @pl.when(qi == 0)
    def _():
        sel = _piece_selector(hg * hpg, hpg, heads, -1.0, 0)
        ones = jnp.where((lane >= N_PIECES) & (lane < 2 * N_PIECES), 1.0, 0.0)
        for g in range(hpg):
            vt_ref[g, dh:, :] = jnp.ones((ONES_ROWS, S), BF16)

        def body(i, carry):
            sl = pl.ds(pl.multiple_of(i * tq, tq), tq)
            ek = (jnp.dot(cp_ref[sl, :], sel, preferred_element_type=F32) + ones).astype(BF16)
            for g in range(hpg):
                kx_ref[g, sl, :dh] = k_ref[sl, cols[g]]
                kx_ref[g, sl, dh:] = ek[:, cols[g]]
                vt_ref[g, :dh, sl] = v_ref[sl, cols[g]].astype(F32).T.astype(BF16)
            return carry
        lax.fori_loop(0, S // tq, body, 0)

    q0 = pl.multiple_of(qi * tq, tq)
    sel_q = _piece_selector(hg * hpg, hpg, heads, 1.0, N_PIECES)
    cref = jnp.dot(cp_ref[pl.ds(q0, ONES_ROWS), :], sel_q, preferred_element_type=F32)[0:1, :]
    eq = (cref + jnp.where(lane < N_PIECES, 1.0, 0.0)).astype(BF16)
    qx = [jnp.concatenate([q_ref[:, cols[g]], jnp.broadcast_to(eq[:, cols[g]], (tq, dh))], axis=1)
          for g in range(hpg)]

    def scores(g, j, diag):
        k0 = pl.multiple_of(j * tk, tk)
        st = lax.dot_general(kx_ref[g, pl.ds(k0, tk), :], qx[g], _NT, preferred_element_type=F32)
        if diag is not None:
            kk = lax.broadcasted_iota(jnp.int32, st.shape, 0) + diag * tk
            qq = lax.broadcasted_iota(jnp.int32, st.shape, 1)
            st = jnp.where(kk <= qq, st, NEG_BIG)
        return st, jnp.max(st, axis=0, keepdims=True)

    def pv_update(g, j, slot, alpha):
        k0 = pl.multiple_of(j * tk, tk)
        pv = jnp.dot(vt_ref[g, :, pl.ds(k0, tk)], p_ref[g, slot], preferred_element_type=F32)
        acc_ref[g] = alpha * acc_ref[g] + pv

    def stage(g, slot, carry, prev_blk, next_blk, next_diag=None):
        m, alpha_prev, cmax = carry
        if prev_blk is not None:
            pv_update(g, prev_blk, 1 - slot, alpha_prev)
        st_next, cmax_next = scores(g, next_blk, next_diag)
        st_ref[g, 1 - slot] = st_next
        m_new = jnp.maximum(m, cmax)
        p_ref[g, slot] = jnp.exp2(st_ref[g, slot] - m_new).astype(BF16)
        return m_new, jnp.exp2(m - m_new), cmax_next

    assert tq == 2 * tk
    d0 = 2 * qi
    carries = []
    for g in range(hpg):
        acc_ref[g] = jnp.zeros(acc_ref.shape[1:], F32)
        st, cmax = scores(g, d0, 0)
        st_ref[g, 0] = st
        c = (jnp.full((1, tq), NEG_BIG, F32), jnp.ones((1, tq), F32), cmax)
        c = stage(g, 0, c, None, d0 + 1, 1)
        carries.append(stage(g, 1, c, d0, 0))

    def pair(i, carries):
        out = []
        for g in range(hpg):
            c = stage(g, 0, carries[g], jnp.where(i == 0, d0 + 1, 2 * i - 1), 2 * i + 1)
            out.append(stage(g, 1, c, 2 * i, 2 * i + 2))
        return tuple(out)

    carries = lax.fori_loop(0, qi, pair, tuple(carries))
    last_blk = jnp.where(qi == 0, d0 + 1, d0 - 1)
    for g in range(hpg):
        pv_update(g, last_blk, 1, carries[g][1])
        ot = acc_ref[g, :dh, :] / acc_ref[g, dh:dh + 1, :]
        o_ref[:, cols[g]] = ot.T.astype(o_ref.dtype)


def _fox_attention(qkv, cp, *, batch, heads, casts=(), tq=512, hpg=4):
    T = qkv.shape[0]
    S = T // batch
    dh = FOX_HEAD_DIM
    tq = min(tq, S)
    tk = tq // 2
    nq = S // tq
    ng = heads // hpg
    w = hpg * dh
    grid = (batch, ng, nq)
    c_in, c_args, c_out, c_shapes, c_widths = _side_cast_plan(casts, grid)
    return pl.pallas_call(
        functools.partial(_fox_attn_kernel, tq=tq, tk=tk, hpg=hpg, heads=heads,
                          cast_widths=c_widths),
        grid=grid,
        in_specs=[
            pl.BlockSpec((tq, w), lambda b, h, i: (b * nq + i, h)),
            pl.BlockSpec((S, w), lambda b, h, i: (b, ng + h)),
            pl.BlockSpec((S, w), lambda b, h, i: (b, 2 * ng + h)),
            pl.BlockSpec((S, LANES), lambda b, h, i: (b, 0)),
        ] + c_in,
        out_specs=[pl.BlockSpec((tq, w), lambda b, h, i: (b * nq + i, h))] + c_out,
        out_shape=[jax.ShapeDtypeStruct((T, heads * dh), BF16)] + c_shapes,
        scratch_shapes=[pltpu.VMEM((hpg, S, 2 * dh), BF16),
                        pltpu.VMEM((hpg, dh + ONES_ROWS, S), BF16),
                        pltpu.VMEM((hpg, dh + ONES_ROWS, tq), F32),
                        pltpu.VMEM((hpg, 2, tk, tq), F32),
                        pltpu.VMEM((hpg, 2, tk, tq), BF16)],
        compiler_params=_params(("parallel", "parallel", "arbitrary")),
        name="fox_attention",
    )(qkv, qkv, qkv, cp, *c_args)


MXU_COLS = 256


def _matmul_res_kernel(a_ref, w_ref, r_ref, o_ref):
    for c0 in range(0, w_ref.shape[1], MXU_COLS):
        cols = slice(c0, c0 + MXU_COLS)
        o_ref[:, cols] = r_ref[:, cols] + jnp.dot(a_ref[...], w_ref[:, cols],
                                                  preferred_element_type=F32)


def _matmul_res(a, w, res, *, tm=512):
    T, K = a.shape
    N = w.shape[1]
    tm = _tile(T, tm)
    return pl.pallas_call(
        _matmul_res_kernel,
        grid=(T // tm,),
        in_specs=[pl.BlockSpec((tm, K), lambda i: (i, 0)),
                  pl.BlockSpec((K, N), lambda i: (0, 0), pipeline_mode=pl.Buffered(1)),
                  pl.BlockSpec((tm, N), lambda i: (i, 0))],
        out_specs=pl.BlockSpec((tm, N), lambda i: (i, 0)),
        out_shape=jax.ShapeDtypeStruct((T, N), F32),
        compiler_params=_params(("parallel",)),
        name="matmul_res",
    )(a, w, res)


def _mlp_kernel(x_ref, g_ref, wu_ref, wd_ref, *rest, has_final, row_chunk, cast_widths):
    rest = list(rest)
    gf_ref = rest.pop(0) if has_final else None
    cast_in = [rest.pop(0) for _ in range(_n_cast_in(cast_widths))]
    o_ref = rest.pop(0)
    cast_out = [rest.pop(0) for _ in cast_widths]
    hn_ref, = rest
    _side_cast_body(cast_in, cast_out, cast_widths)
    tm = x_ref.shape[0]
    f = pl.program_id(1)
    chunks = [slice(r0, r0 + row_chunk) for r0 in range(0, tm, row_chunk)]

    @pl.when(f == 0)
    def _():
        for sl in chunks:
            x = x_ref[sl, :]
            hn_ref[sl, :] = _rms_rows(x, g_ref[...]).astype(BF16)
            o_ref[sl, :] = x

    u = jnp.dot(hn_ref[...], wu_ref[...], preferred_element_type=F32)
    a = jnp.square(jnp.maximum(u, 0.0)).astype(BF16)
    o_ref[...] += jnp.dot(a, wd_ref[...], preferred_element_type=F32)

    if has_final:
        @pl.when(f == pl.num_programs(1) - 1)
        def _():
            for sl in chunks:
                o_ref[sl, :] = _rms_rows(o_ref[sl, :], gf_ref[...])


def _mlp(x, g, wu, wd, g_final=None, *, casts=(), tm=1024, tf=512):
    T, D = x.shape
    Fdim = wu.shape[1]
    tm = _tile(T, tm)
    tf = _tile(Fdim, tf)
    has_final = g_final is not None
    in_specs = [pl.BlockSpec((tm, D), lambda i, f: (i, 0)),
                pl.BlockSpec((1, D), lambda i, f: (0, 0)),
                pl.BlockSpec((D, tf), lambda i, f: (0, f)),
                pl.BlockSpec((tf, D), lambda i, f: (f, 0))]
    args = [x, g.reshape(1, D), wu, wd]
    if has_final:
        in_specs.append(pl.BlockSpec((1, D), lambda i, f: (0, 0)))
        args.append(g_final.reshape(1, D))
    grid = (T // tm, Fdim // tf)
    c_in, c_args, c_out, c_shapes, c_widths = _side_cast_plan(casts, grid)
    return pl.pallas_call(
        functools.partial(_mlp_kernel, has_final=has_final, row_chunk=min(256, tm),
                          cast_widths=c_widths),
        grid=grid,
        in_specs=in_specs + c_in,
        out_specs=[pl.BlockSpec((tm, D), lambda i, f: (i, 0))] + c_out,
        out_shape=[jax.ShapeDtypeStruct((T, D), F32)] + c_shapes,
        scratch_shapes=[pltpu.VMEM((tm, D), BF16)],
        compiler_params=_params(("parallel", "arbitrary")),
        name="mlp_final" if has_final else "mlp",
    )(*args, *c_args)


def _retention_kernel(q_ref, k_ref, v_ref, sg_ref, cd_ref, gn_ref, *rest, chunk, nchunks,
                      cast_widths):
    rest = list(rest)
    cast_in = [rest.pop(0) for _ in range(_n_cast_in(cast_widths))]
    o_ref = rest.pop(0)
    cast_out = [rest.pop(0) for _ in cast_widths]
    state_ref, = rest
    _side_cast_body(cast_in, cast_out, cast_widths)

    @pl.when(pl.program_id(2) == 0)
    def _():
        state_ref[...] = jnp.zeros_like(state_ref)

    causal = (lax.broadcasted_iota(jnp.int32, (chunk, chunk), 0)
              >= lax.broadcasted_iota(jnp.int32, (chunk, chunk), 1))
    for ci in range(nchunks):
        sl = slice(ci * chunk, (ci + 1) * chunk)
        q = q_ref[sl, :]
        k = k_ref[sl, :]
        v = v_ref[sl, :]
        state = state_ref[...]

        inner = lax.dot_general(q, k, _NT, preferred_element_type=F32)
        inner = jnp.where(causal, inner, 0.0).astype(BF16)
        o = jnp.dot(inner, v, preferred_element_type=F32)
        o = o + jnp.dot(q, state.astype(BF16), preferred_element_type=F32)
        k_t = k.astype(F32).T.astype(BF16)
        state_ref[...] = cd_ref[...] * (state + jnp.dot(k_t, v, preferred_element_type=F32))

        ms = jnp.mean(o * o, axis=-1, keepdims=True)
        on = o * lax.rsqrt(ms + RMS_EPS) * gn_ref[...]
        o_ref[sl, :] = (sg_ref[sl, :].astype(F32) * on).astype(o_ref.dtype)


def _retention_tables(heads, chunk, seq):
    dk, dv = RET_QK_DIM, RET_V_DIM
    half = dk // 2
    pos = np.arange(seq, dtype=np.float64)
    inv = ROPE_BASE ** (-np.arange(half, dtype=np.float64) / half)
    ang = pos[:, None] * inv[None, :]
    log_gamma = np.log1p(-np.exp2(-5.0 - np.arange(heads, dtype=np.float64)))
    chunk_decay = np.broadcast_to(np.exp(log_gamma * chunk)[:, None, None], (heads, 1, dv))
    return (jnp.asarray(np.cos(ang), dtype=F32), jnp.asarray(np.sin(ang), dtype=F32),
            log_gamma, jnp.asarray(chunk_decay, dtype=F32))


def _retention(proj, gn, cd, *, batch, heads, chunk, casts=(), rows=1024):
    T = proj.shape[0]
    S = T // batch
    dk, dv = RET_QK_DIM, RET_V_DIM
    rows = min(rows, S)
    ns = S // rows
    vblk0 = 2 * heads * dk // dv
    grid = (batch, heads, ns)
    c_in, c_args, c_out, c_shapes, c_widths = _side_cast_plan(casts, grid)
    return pl.pallas_call(
        functools.partial(_retention_kernel, chunk=chunk, nchunks=rows // chunk,
                          cast_widths=c_widths),
        grid=grid,
        in_specs=[
            pl.BlockSpec((rows, dk), lambda b, h, s: (b * ns + s, h)),
            pl.BlockSpec((rows, dk), lambda b, h, s: (b * ns + s, heads + h)),
            pl.BlockSpec((rows, dv), lambda b, h, s: (b * ns + s, vblk0 + h)),
            pl.BlockSpec((rows, dv), lambda b, h, s: (b * ns + s, vblk0 + heads + h)),
            pl.BlockSpec((None, 1, dv), lambda b, h, s: (h, 0, 0)),
            pl.BlockSpec((1, dv), lambda b, h, s: (0, h)),
        ] + c_in,
        out_specs=[pl.BlockSpec((rows, dv), lambda b, h, s: (b * ns + s, h))] + c_out,
        out_shape=[jax.ShapeDtypeStruct((T, heads * dv), BF16)] + c_shapes,
        scratch_shapes=[pltpu.VMEM((dk, dv), F32)],
        compiler_params=_params(("parallel", "parallel", "arbitrary")),
        name="retention",
    )(proj, proj, proj, proj, cd, gn.reshape(1, heads * dv), *c_args)


def kernel(x, fox_norm, fox_wq, fox_wk, fox_wv, fox_wf, fox_bf, fox_wo, ret_norm, ret_wq, ret_wk,
           ret_wv, ret_wg, ret_gn, ret_wo, mlp_norm, mlp_up, mlp_down, final_norm):
    B, S, D = x.shape
    T = B * S
    fox_heads = D // FOX_HEAD_DIM
    ret_heads = D // RET_QK_DIM
    h = x.reshape(T, D)

    w_qkv = _cast_concat([fox_wq, fox_wk, fox_wv], 0)
    cs = jnp.concatenate([jnp.full((D,), LOG2E * FOX_HEAD_DIM ** -0.5, F32),
                          jnp.ones((2 * D,), F32)])
    wf_pad = jnp.pad(fox_wf[0], ((0, 0), (0, LANES - fox_heads))).astype(BF16)
    bf_pad = jnp.pad(fox_bf[0], (0, LANES - fox_heads)).reshape(1, LANES)
    qkv, z = _norm_matmul(h, fox_norm[0], w_qkv, cs, wf_pad)
    cp = _gate_cumsum(z, bf_pad, B, fox_heads)
    attn, wo0, up0, down0, w_proj, wo1, up1, down1 = _fox_attention(
        qkv, cp, batch=B, heads=fox_heads,
        casts=[([fox_wo], 0), ([mlp_up], 0), ([mlp_down], 0),
               ([ret_wq, ret_wk, ret_wv, ret_wg], 0), ([ret_wo], 0),
               ([mlp_up], 1), ([mlp_down], 1)])
    h = _matmul_res(attn, wo0, h)
    h, = _mlp(h, mlp_norm[0], up0, down0)

    chunk = min(RET_CHUNK, S)
    cos, sin, log_gamma, cd = _retention_tables(ret_heads, chunk, S)
    rv = ret_heads * RET_V_DIM
    cs = jnp.concatenate([jnp.ones((D,), F32), jnp.full((D,), RET_QK_DIM ** -0.5, F32),
                          jnp.ones((2 * rv,), F32)])
    lg_cols = np.repeat(log_gamma, RET_QK_DIM)
    lgcol = jnp.asarray(np.concatenate([lg_cols, -lg_cols, np.zeros(2 * rv)]), dtype=F32)
    proj, = _norm_matmul(h, ret_norm[0], w_proj, cs, ret=(cos, sin, lgcol, 2 * D, rv, chunk))
    y, = _retention(proj, ret_gn[0], cd, batch=B, heads=ret_heads, chunk=chunk)
    h = _matmul_res(y, wo1, h)
    h, = _mlp(h, mlp_norm[1], up1, down1, final_norm)
    return h.reshape(B, S, D)
```

```python
import functools
import math

import numpy as np
import jax
import jax.numpy as jnp
from jax import lax
from jax.experimental import pallas as pl
from jax.experimental.pallas import tpu as pltpu

F32 = jnp.float32
BF16 = jnp.bfloat16

RMS_EPS = 1e-6
ROPE_BASE = 10000.0
FOX_HEAD_DIM = 128
RET_QK_DIM = 256
RET_V_DIM = 512
RET_CHUNK = 256
LANES = 128
BF16_ROWS = 16
NEG_BIG = -1e30
LOG2E = math.log2(math.e)
V7X_VMEM_LIMIT = 56 * 1024 * 1024

_NT = (((1,), (1,)), ((), ()))


def _params(sem, vmem=V7X_VMEM_LIMIT):
    return pltpu.CompilerParams(dimension_semantics=sem, vmem_limit_bytes=vmem)


def _tile(n, target, align=LANES):
    if n <= target:
        return n
    t = (target // align) * align
    while n % t:
        t -= align
    return t


def _rms_rows(x, g):
    ms = jnp.mean(x * x, axis=-1, keepdims=True)
    return x * lax.rsqrt(ms + RMS_EPS) * g


def _cast_concat_kernel(*refs, starts, nblks):
    o_ref = refs[-1]
    p = pl.program_id(0)
    for w_ref, s0, nb in zip(refs[:-1], starts, nblks):
        @pl.when((p >= s0) & (p < s0 + nb))
        def _(w_ref=w_ref):
            o_ref[...] = w_ref[...].astype(o_ref.dtype)


def _cast_concat(ws, layer, *, bw=2048, tr=512):
    R = ws[0].shape[1]
    bw = min(bw, min(w.shape[2] for w in ws))
    tr = _tile(R, tr)
    nr = R // tr
    nblks = [w.shape[2] // bw for w in ws]
    starts = [sum(nblks[:i]) for i in range(len(ws))]

    def in_map(s0, nb):
        def index(p, r):
            local = p - s0
            row = jnp.where(local < 0, 0, jnp.where(local >= nb, nr - 1, r))
            return layer, row, jnp.clip(local, 0, nb - 1)
        return index

    return pl.pallas_call(
        functools.partial(_cast_concat_kernel, starts=starts, nblks=nblks),
        grid=(sum(nblks), nr),
        in_specs=[pl.BlockSpec((None, tr, bw), in_map(s0, nb)) for s0, nb in zip(starts, nblks)],
        out_specs=pl.BlockSpec((tr, bw), lambda p, r: (r, p)),
        out_shape=jax.ShapeDtypeStruct((R, sum(nblks) * bw), BF16),
        compiler_params=_params(("arbitrary", "arbitrary")),
        name="cast_concat",
    )(*ws)


def _side_cast_plan(groups, grid):
    nsteps = math.prod(grid)

    def lin(*ids):
        l = ids[0]
        for n, i in zip(grid[1:], ids[1:]):
            l = l * n + i
        return l

    in_specs, args, out_specs, out_shapes, widths = [], [], [], [], []
    for ws, layer in groups:
        R = ws[0].shape[1]
        rows = BF16_ROWS
        while R // rows > nsteps:
            rows *= 2
        assert R % rows == 0
        nblk = R // rows
        rep = nsteps // nblk

        def blk(*ids, rep=rep, nblk=nblk):
            return jnp.minimum(lin(*ids) // rep, nblk - 1)

        for w in ws:
            in_specs.append(pl.BlockSpec((None, rows, w.shape[2]),
                                         lambda *ids, blk=blk, layer=layer: (layer, blk(*ids), 0)))
            args.append(w)
        cols = sum(w.shape[2] for w in ws)
        out_specs.append(pl.BlockSpec((rows, cols), lambda *ids, blk=blk: (blk(*ids), 0)))
        out_shapes.append(jax.ShapeDtypeStruct((R, cols), BF16))
        widths.append(tuple(w.shape[2] for w in ws))
    return in_specs, args, out_specs, out_shapes, tuple(widths)


def _side_cast_body(in_refs, out_refs, widths):
    k = 0
    for o_ref, ws in zip(out_refs, widths):
        c0 = 0
        for wd in ws:
            o_ref[:, c0:c0 + wd] = in_refs[k][...].astype(BF16)
            k += 1
            c0 += wd


def _n_cast_in(widths):
    return sum(len(ws) for ws in widths)


def _norm_matmul_kernel(x_ref, g_ref, w_ref, cs_ref, *rest, has_gate, ret, row_chunk, cast_widths):
    rest = list(rest)
    wz_ref = rest.pop(0) if has_gate else None
    if ret is not None:
        cos_ref, sin_ref, lg_ref = rest.pop(0), rest.pop(0), rest.pop(0)
    cast_in = [rest.pop(0) for _ in range(_n_cast_in(cast_widths))]
    o_ref = rest.pop(0)
    z_ref = rest.pop(0) if has_gate else None
    cast_out = [rest.pop(0) for _ in cast_widths]
    hn_ref, = rest
    tm = x_ref.shape[0]
    _side_cast_body(cast_in, cast_out, cast_widths)

    @pl.when(pl.program_id(1) == 0)
    def _():
        for r0 in range(0, tm, row_chunk):
            y = _rms_rows(x_ref[r0:r0 + row_chunk, :], g_ref[...])
            hn_ref[r0:r0 + row_chunk, :] = y.astype(BF16)
        if has_gate:
            z_ref[...] = jnp.dot(hn_ref[...], wz_ref[...], preferred_element_type=F32)

    tn = w_ref.shape[1]

    def heads(epilogue):
        for c0 in range(0, tn, RET_QK_DIM):
            acc = jnp.dot(hn_ref[...], w_ref[:, c0:c0 + RET_QK_DIM], preferred_element_type=F32)
            epilogue(c0, acc)

    def plain(c0, acc):
        cols = slice(c0, c0 + RET_QK_DIM)
        o_ref[:, cols] = (acc * cs_ref[:, cols]).astype(o_ref.dtype)

    if ret is None:
        heads(plain)
        return

    n_rot, n_plain, chunk = ret
    j = pl.program_id(1)
    half = RET_QK_DIM // 2

    @pl.when(j < n_rot)
    def _():
        t = (lax.broadcasted_iota(jnp.int32, (tm, half), 0) % chunk + 1).astype(F32)
        cos = cos_ref[...]
        sin = sin_ref[...]

        def rotary(c0, acc):
            h1 = slice(c0, c0 + half)
            h2 = slice(c0 + half, c0 + 2 * half)
            x1 = acc[:, :half]
            x2 = acc[:, half:]
            scale = jnp.exp(t * lg_ref[:, h1]) * cs_ref[:, h1]
            o_ref[:, h1] = ((x1 * cos - x2 * sin) * scale).astype(o_ref.dtype)
            o_ref[:, h2] = ((x1 * sin + x2 * cos) * scale).astype(o_ref.dtype)
        heads(rotary)

    @pl.when((j >= n_rot) & (j < n_rot + n_plain))
    def _():
        heads(plain)

    @pl.when(j >= n_rot + n_plain)
    def _():
        def swish(c0, acc):
            cols = slice(c0, c0 + RET_QK_DIM)
            y = acc * cs_ref[:, cols]
            sig = 0.5 * jnp.tanh(0.5 * y) + 0.5
            o_ref[:, cols] = (y * sig).astype(o_ref.dtype)
        heads(swish)


def _norm_matmul(x, g, w, colscale, wz=None, *, ret=None, casts=(), tm=1024, tn=2048):
    T, D = x.shape
    N = w.shape[1]
    tm = _tile(T, tm)
    tn = _tile(N if ret is None else math.gcd(ret[3], ret[4]), tn)
    has_gate = wz is not None
    grid = (T // tm, N // tn)
    c_in, c_args, c_out, c_shapes, c_widths = _side_cast_plan(casts, grid)
    in_specs = [
        pl.BlockSpec((tm, D), lambda i, j: (i, 0)),
        pl.BlockSpec((1, D), lambda i, j: (0, 0)),
        pl.BlockSpec((D, tn), lambda i, j: (0, j)),
        pl.BlockSpec((1, tn), lambda i, j: (0, j)),
    ]
    args = [x, g.reshape(1, D), w, colscale.reshape(1, N)]
    out_shape = [jax.ShapeDtypeStruct((T, N), BF16)]
    out_specs = [pl.BlockSpec((tm, tn), lambda i, j: (i, j))]
    if has_gate:
        in_specs.append(pl.BlockSpec((D, LANES), lambda i, j: (0, 0)))
        args.append(wz)
        out_shape.append(jax.ShapeDtypeStruct((T, LANES), F32))
        out_specs.append(pl.BlockSpec((tm, LANES), lambda i, j: (i, 0)))
    ret_static = None
    if ret is not None:
        cos, sin, lgcol, rot_cols, plain_cols, chunk = ret
        assert rot_cols % tn == 0 and plain_cols % tn == 0 and tm % chunk == 0
        nrt = cos.shape[0] // tm
        half = RET_QK_DIM // 2
        in_specs += [pl.BlockSpec((tm, half), lambda i, j: (i % nrt, 0)),
                     pl.BlockSpec((tm, half), lambda i, j: (i % nrt, 0)),
                     pl.BlockSpec((1, tn), lambda i, j: (0, j))]
        args += [cos, sin, lgcol.reshape(1, N)]
        ret_static = (rot_cols // tn, plain_cols // tn, chunk)
    res = pl.pallas_call(
        functools.partial(_norm_matmul_kernel, has_gate=has_gate, ret=ret_static,
                          row_chunk=min(256, tm), cast_widths=c_widths),
        grid=grid,
        in_specs=in_specs + c_in,
        out_specs=out_specs + c_out,
        out_shape=out_shape + c_shapes,
        scratch_shapes=[pltpu.VMEM((tm, D), BF16)],
        compiler_params=_params(("parallel", "arbitrary")),
        name="norm_matmul_gate" if has_gate else "norm_matmul",
    )(*args, *c_args)
    return res


def _gate_cumsum_kernel(z_ref, b_ref, cp_ref, *, heads):
    z = z_ref[...] + b_ref[...]
    c = jnp.minimum(z, 0.0) - jnp.log1p(jnp.exp(-jnp.abs(z)))
    S = c.shape[0]
    row = lax.broadcasted_iota(jnp.int32, c.shape, 0)
    shift = 1
    while shift < S:
        c = c + jnp.where(row >= shift, pltpu.roll(c, shift, axis=0), 0.0)
        shift *= 2
    c = c * LOG2E
    hi = c.astype(BF16).astype(F32)
    r1 = c - hi
    mid = r1.astype(BF16).astype(F32)
    lo = r1 - mid
    lane = lax.broadcasted_iota(jnp.int32, c.shape, 1)
    packed = jnp.where(lane < heads, hi,
                       jnp.where(lane < 2 * heads, pltpu.roll(mid, heads, axis=1),
                                 jnp.where(lane < 3 * heads, pltpu.roll(lo, 2 * heads, axis=1), 0.0)))
    cp_ref[...] = packed.astype(BF16)


def _gate_cumsum(z, b_pad, batch, heads):
    T = z.shape[0]
    S = T // batch
    assert 3 * heads <= LANES
    return pl.pallas_call(
        functools.partial(_gate_cumsum_kernel, heads=heads),
        grid=(batch,),
        in_specs=[pl.BlockSpec((S, LANES), lambda b: (b, 0)),
                  pl.BlockSpec((1, LANES), lambda b: (0, 0))],
        out_specs=pl.BlockSpec((S, LANES), lambda b: (b, 0)),
        out_shape=jax.ShapeDtypeStruct((T, LANES), BF16),
        compiler_params=_params(("parallel",)),
        name="gate_cumsum",
    )(z, b_pad)


ONES_ROWS = 16
ROW_SKEW = LANES
N_PIECES = 3


def _piece_selector(head0, hpg, heads, sign, lane0):
    r = lax.broadcasted_iota(jnp.int32, (LANES, hpg * LANES), 0)
    col = lax.broadcasted_iota(jnp.int32, (LANES, hpg * LANES), 1)
    g = col // LANES
    p = col % LANES - lane0
    hit = (p >= 0) & (p < N_PIECES) & (r == head0 + g + heads * p)
    return jnp.where(hit, sign, 0.0).astype(BF16)


def _fox_attn_kernel(q_ref, k_ref, v_ref, cp_ref, *rest, tq, tk, hpg, heads, cast_widths):
    rest = list(rest)
    cast_in = [rest.pop(0) for _ in range(_n_cast_in(cast_widths))]
    o_ref = rest.pop(0)
    cast_out = [rest.pop(0) for _ in cast_widths]
    kx_ref, vt_ref, acc_ref, st_ref, p_ref = rest
    _side_cast_body(cast_in, cast_out, cast_widths)
    hg = pl.program_id(1)
    qi = pl.program_id(2)
    S = k_ref.shape[0]
    dh = FOX_HEAD_DIM
    lane = lax.broadcasted_iota(jnp.int32, (1, hpg * LANES), 1) % LANES
    cols = [slice(g * dh, (g + 1) * dh) for g in range(hpg)]

    @pl.when(qi == 0)
    def _():
        sel = _piece_selector(hg * hpg, hpg, heads, -1.0, 0)
        ones = jnp.where((lane >= N_PIECES) & (lane < 2 * N_PIECES), 1.0, 0.0)
        for g in range(hpg):
            vt_ref[g, dh:, :] = jnp.ones((ONES_ROWS, S), BF16)

        def body(i, carry):
            sl = pl.ds(pl.multiple_of(i * tq, tq), tq)
            ek = (jnp.dot(cp_ref[sl, :], sel, preferred_element_type=F32) + ones).astype(BF16)
            for g in range(hpg):
                kx_ref[g, sl, :dh] = k_ref[sl, cols[g]]
                kx_ref[g, sl, dh:] = ek[:, cols[g]]
                vt_ref[g, :dh, sl] = v_ref[sl, cols[g]].astype(F32).T.astype(BF16)
            return carry
        lax.fori_loop(0, S // tq, body, 0)

    q0 = pl.multiple_of(qi * tq, tq)
    sel_q = _piece_selector(hg * hpg, hpg, heads, 1.0, N_PIECES)
    cref = jnp.dot(cp_ref[pl.ds(q0, ONES_ROWS), :], sel_q, preferred_element_type=F32)[0:1, :]
    eq = (cref + jnp.where(lane < N_PIECES, 1.0, 0.0)).astype(BF16)
    qx = [jnp.concatenate([q_ref[:, cols[g]], jnp.broadcast_to(eq[:, cols[g]], (tq, dh))], axis=1)
          for g in range(hpg)]

    def scores(g, j, diag):
        k0 = pl.multiple_of(j * tk, tk)
        st = lax.dot_general(kx_ref[g, pl.ds(k0, tk), :], qx[g], _NT, preferred_element_type=F32)
        if diag is not None:
            kk = lax.broadcasted_iota(jnp.int32, st.shape, 0) + diag * tk
            qq = lax.broadcasted_iota(jnp.int32, st.shape, 1)
            st = jnp.where(kk <= qq, st, NEG_BIG)
        return st, jnp.max(st, axis=0, keepdims=True)

    def pv_update(g, j, slot, alpha):
        k0 = pl.multiple_of(j * tk, tk)
        pv = jnp.dot(vt_ref[g, :, pl.ds(k0, tk)], p_ref[g, slot, :, :tq],
                     preferred_element_type=F32)
        acc_ref[g, :, :tq] = alpha * acc_ref[g, :, :tq] + pv

    def stage(g, slot, carry, prev_blk, next_blk, next_diag=None):
        m, alpha_prev, cmax = carry
        if prev_blk is not None:
            pv_update(g, prev_blk, 1 - slot, alpha_prev)
        st_next, cmax_next = scores(g, next_blk, next_diag)
        st_ref[g, 1 - slot, :, :tq] = st_next
        m_new = jnp.maximum(m, cmax)
        p_ref[g, slot, :, :tq] = jnp.exp2(st_ref[g, slot, :, :tq] - m_new).astype(BF16)
        return m_new, jnp.exp2(m - m_new), cmax_next

    assert tq == 2 * tk
    d0 = 2 * qi
    carries = []
    for g in range(hpg):
        acc_ref[g, :, :tq] = jnp.zeros((acc_ref.shape[1], tq), F32)
        st, cmax = scores(g, d0, 0)
        st_ref[g, 0, :, :tq] = st
        c = (jnp.full((1, tq), NEG_BIG, F32), jnp.ones((1, tq), F32), cmax)
        c = stage(g, 0, c, None, d0 + 1, 1)
        carries.append(stage(g, 1, c, d0, 0))

    def pair(i, carries):
        out = []
        for g in range(hpg):
            c = stage(g, 0, carries[g], jnp.where(i == 0, d0 + 1, 2 * i - 1), 2 * i + 1)
            out.append(stage(g, 1, c, 2 * i, 2 * i + 2))
        return tuple(out)

    carries = lax.fori_loop(0, qi, pair, tuple(carries))
    last_blk = jnp.where(qi == 0, d0 + 1, d0 - 1)
    for g in range(hpg):
        pv_update(g, last_blk, 1, carries[g][1])
        ot = acc_ref[g, :dh, :tq] / acc_ref[g, dh:dh + 1, :tq]
        o_ref[:, cols[g]] = ot.T.astype(o_ref.dtype)


def _fox_attention(qkv, cp, *, batch, heads, casts=(), tq=512, hpg=4):
    T = qkv.shape[0]
    S = T // batch
    dh = FOX_HEAD_DIM
    tq = min(tq, S)
    tk = tq // 2
    nq = S // tq
    ng = heads // hpg
    w = hpg * dh
    grid = (batch, ng, nq)
    c_in, c_args, c_out, c_shapes, c_widths = _side_cast_plan(casts, grid)
    return pl.pallas_call(
        functools.partial(_fox_attn_kernel, tq=tq, tk=tk, hpg=hpg, heads=heads,
                          cast_widths=c_widths),
        grid=grid,
        in_specs=[
            pl.BlockSpec((tq, w), lambda b, h, i: (b * nq + i, h)),
            pl.BlockSpec((S, w), lambda b, h, i: (b, ng + h)),
            pl.BlockSpec((S, w), lambda b, h, i: (b, 2 * ng + h)),
            pl.BlockSpec((S, LANES), lambda b, h, i: (b, 0)),
        ] + c_in,
        out_specs=[pl.BlockSpec((tq, w), lambda b, h, i: (b * nq + i, h))] + c_out,
        out_shape=[jax.ShapeDtypeStruct((T, heads * dh), BF16)] + c_shapes,
        scratch_shapes=[pltpu.VMEM((hpg, S, 2 * dh), BF16),
                        pltpu.VMEM((hpg, dh + ONES_ROWS, S), BF16),
                        pltpu.VMEM((hpg, dh + ONES_ROWS, tq + ROW_SKEW), F32),
                        pltpu.VMEM((hpg, 2, tk, tq + ROW_SKEW), F32),
                        pltpu.VMEM((hpg, 2, tk, tq + ROW_SKEW), BF16)],
        compiler_params=_params(("parallel", "parallel", "arbitrary")),
        name="fox_attention",
    )(qkv, qkv, qkv, cp, *c_args)


MXU_COLS = 256


def _matmul_res_kernel(a_ref, w_ref, r_ref, o_ref):
    for c0 in range(0, w_ref.shape[1], MXU_COLS):
        cols = slice(c0, c0 + MXU_COLS)
        o_ref[:, cols] = r_ref[:, cols] + jnp.dot(a_ref[...], w_ref[:, cols],
                                                  preferred_element_type=F32)


def _matmul_res(a, w, res, *, tm=512):
    T, K = a.shape
    N = w.shape[1]
    tm = _tile(T, tm)
    return pl.pallas_call(
        _matmul_res_kernel,
        grid=(T // tm,),
        in_specs=[pl.BlockSpec((tm, K), lambda i: (i, 0)),
                  pl.BlockSpec((K, N), lambda i: (0, 0), pipeline_mode=pl.Buffered(1)),
                  pl.BlockSpec((tm, N), lambda i: (i, 0))],
        out_specs=pl.BlockSpec((tm, N), lambda i: (i, 0)),
        out_shape=jax.ShapeDtypeStruct((T, N), F32),
        compiler_params=_params(("parallel",)),
        name="matmul_res",
    )(a, w, res)


def _mlp_kernel(x_ref, g_ref, wu_ref, wd_ref, *rest, has_final, row_chunk, cast_widths):
    rest = list(rest)
    gf_ref = rest.pop(0) if has_final else None
    cast_in = [rest.pop(0) for _ in range(_n_cast_in(cast_widths))]
    o_ref = rest.pop(0)
    cast_out = [rest.pop(0) for _ in cast_widths]
    hn_ref, = rest
    _side_cast_body(cast_in, cast_out, cast_widths)
    tm = x_ref.shape[0]
    f = pl.program_id(1)
    chunks = [slice(r0, r0 + row_chunk) for r0 in range(0, tm, row_chunk)]

    @pl.when(f == 0)
    def _():
        for sl in chunks:
            x = x_ref[sl, :]
            hn_ref[sl, :] = _rms_rows(x, g_ref[...]).astype(BF16)
            o_ref[sl, :] = x

    u = jnp.dot(hn_ref[...], wu_ref[...], preferred_element_type=F32)
    a = jnp.square(jnp.maximum(u, 0.0)).astype(BF16)
    o_ref[...] += jnp.dot(a, wd_ref[...], preferred_element_type=F32)

    if has_final:
        @pl.when(f == pl.num_programs(1) - 1)
        def _():
            for sl in chunks:
                o_ref[sl, :] = _rms_rows(o_ref[sl, :], gf_ref[...])


def _mlp(x, g, wu, wd, g_final=None, *, casts=(), tm=1024, tf=512):
    T, D = x.shape
    Fdim = wu.shape[1]
    tm = _tile(T, tm)
    tf = _tile(Fdim, tf)
    has_final = g_final is not None
    in_specs = [pl.BlockSpec((tm, D), lambda i, f: (i, 0)),
                pl.BlockSpec((1, D), lambda i, f: (0, 0)),
                pl.BlockSpec((D, tf), lambda i, f: (0, f)),
                pl.BlockSpec((tf, D), lambda i, f: (f, 0))]
    args = [x, g.reshape(1, D), wu, wd]
    if has_final:
        in_specs.append(pl.BlockSpec((1, D), lambda i, f: (0, 0)))
        args.append(g_final.reshape(1, D))
    grid = (T // tm, Fdim // tf)
    c_in, c_args, c_out, c_shapes, c_widths = _side_cast_plan(casts, grid)
    return pl.pallas_call(
        functools.partial(_mlp_kernel, has_final=has_final, row_chunk=min(256, tm),
                          cast_widths=c_widths),
        grid=grid,
        in_specs=in_specs + c_in,
        out_specs=[pl.BlockSpec((tm, D), lambda i, f: (i, 0))] + c_out,
        out_shape=[jax.ShapeDtypeStruct((T, D), F32)] + c_shapes,
        scratch_shapes=[pltpu.VMEM((tm, D), BF16)],
        compiler_params=_params(("parallel", "arbitrary")),
        name="mlp_final" if has_final else "mlp",
    )(*args, *c_args)


def _retention_kernel(q_ref, k_ref, v_ref, sg_ref, cd_ref, gn_ref, *rest, chunk, nchunks,
                      cast_widths):
    rest = list(rest)
    cast_in = [rest.pop(0) for _ in range(_n_cast_in(cast_widths))]
    o_ref = rest.pop(0)
    cast_out = [rest.pop(0) for _ in cast_widths]
    state_ref, = rest
    _side_cast_body(cast_in, cast_out, cast_widths)

    @pl.when(pl.program_id(2) == 0)
    def _():
        state_ref[...] = jnp.zeros_like(state_ref)

    causal = (lax.broadcasted_iota(jnp.int32, (chunk, chunk), 0)
              >= lax.broadcasted_iota(jnp.int32, (chunk, chunk), 1))
    for ci in range(nchunks):
        sl = slice(ci * chunk, (ci + 1) * chunk)
        q = q_ref[sl, :]
        k = k_ref[sl, :]
        v = v_ref[sl, :]
        state = state_ref[...]

        inner = lax.dot_general(q, k, _NT, preferred_element_type=F32)
        inner = jnp.where(causal, inner, 0.0).astype(BF16)
        o = jnp.dot(inner, v, preferred_element_type=F32)
        o = o + jnp.dot(q, state.astype(BF16), preferred_element_type=F32)
        k_t = k.astype(F32).T.astype(BF16)
        state_ref[...] = cd_ref[...] * (state + jnp.dot(k_t, v, preferred_element_type=F32))

        ms = jnp.mean(o * o, axis=-1, keepdims=True)
        on = o * lax.rsqrt(ms + RMS_EPS) * gn_ref[...]
        o_ref[sl, :] = (sg_ref[sl, :].astype(F32) * on).astype(o_ref.dtype)


def _retention_tables(heads, chunk, seq):
    dk, dv = RET_QK_DIM, RET_V_DIM
    half = dk // 2
    pos = np.arange(seq, dtype=np.float64)
    inv = ROPE_BASE ** (-np.arange(half, dtype=np.float64) / half)
    ang = pos[:, None] * inv[None, :]
    log_gamma = np.log1p(-np.exp2(-5.0 - np.arange(heads, dtype=np.float64)))
    chunk_decay = np.broadcast_to(np.exp(log_gamma * chunk)[:, None, None], (heads, 1, dv))
    return (jnp.asarray(np.cos(ang), dtype=F32), jnp.asarray(np.sin(ang), dtype=F32),
            log_gamma, jnp.asarray(chunk_decay, dtype=F32))


def _retention(proj, gn, cd, *, batch, heads, chunk, casts=(), rows=1024):
    T = proj.shape[0]
    S = T // batch
    dk, dv = RET_QK_DIM, RET_V_DIM
    rows = min(rows, S)
    ns = S // rows
    vblk0 = 2 * heads * dk // dv
    grid = (batch, heads, ns)
    c_in, c_args, c_out, c_shapes, c_widths = _side_cast_plan(casts, grid)
    return pl.pallas_call(
        functools.partial(_retention_kernel, chunk=chunk, nchunks=rows // chunk,
                          cast_widths=c_widths),
        grid=grid,
        in_specs=[
            pl.BlockSpec((rows, dk), lambda b, h, s: (b * ns + s, h)),
            pl.BlockSpec((rows, dk), lambda b, h, s: (b * ns + s, heads + h)),
            pl.BlockSpec((rows, dv), lambda b, h, s: (b * ns + s, vblk0 + h)),
            pl.BlockSpec((rows, dv), lambda b, h, s: (b * ns + s, vblk0 + heads + h)),
            pl.BlockSpec((None, 1, dv), lambda b, h, s: (h, 0, 0)),
            pl.BlockSpec((1, dv), lambda b, h, s: (0, h)),
        ] + c_in,
        out_specs=[pl.BlockSpec((rows, dv), lambda b, h, s: (b * ns + s, h))] + c_out,
        out_shape=[jax.ShapeDtypeStruct((T, heads * dv), BF16)] + c_shapes,
        scratch_shapes=[pltpu.VMEM((dk, dv), F32)],
        compiler_params=_params(("parallel", "parallel", "arbitrary")),
        name="retention",
    )(proj, proj, proj, proj, cd, gn.reshape(1, heads * dv), *c_args)


def kernel(x, fox_norm, fox_wq, fox_wk, fox_wv, fox_wf, fox_bf, fox_wo, ret_norm, ret_wq, ret_wk,
           ret_wv, ret_wg, ret_gn, ret_wo, mlp_norm, mlp_up, mlp_down, final_norm):
    B, S, D = x.shape
    T = B * S
    fox_heads = D // FOX_HEAD_DIM
    ret_heads = D // RET_QK_DIM
    h = x.reshape(T, D)

    w_qkv = _cast_concat([fox_wq, fox_wk, fox_wv], 0)
    cs = jnp.concatenate([jnp.full((D,), LOG2E * FOX_HEAD_DIM ** -0.5, F32),
                          jnp.ones((2 * D,), F32)])
    wf_pad = jnp.pad(fox_wf[0], ((0, 0), (0, LANES - fox_heads))).astype(BF16)
    bf_pad = jnp.pad(fox_bf[0], (0, LANES - fox_heads)).reshape(1, LANES)
    qkv, z = _norm_matmul(h, fox_norm[0], w_qkv, cs, wf_pad)
    cp = _gate_cumsum(z, bf_pad, B, fox_heads)
    attn, wo0, up0, down0, w_proj, wo1, up1, down1 = _fox_attention(
        qkv, cp, batch=B, heads=fox_heads,
        casts=[([fox_wo], 0), ([mlp_up], 0), ([mlp_down], 0),
               ([ret_wq, ret_wk, ret_wv, ret_wg], 0), ([ret_wo], 0),
               ([mlp_up], 1), ([mlp_down], 1)])
    h = _matmul_res(attn, wo0, h)
    h, = _mlp(h, mlp_norm[0], up0, down0)

    chunk = min(RET_CHUNK, S)
    cos, sin, log_gamma, cd = _retention_tables(ret_heads, chunk, S)
    rv = ret_heads * RET_V_DIM
    cs = jnp.concatenate([jnp.ones((D,), F32), jnp.full((D,), RET_QK_DIM ** -0.5, F32),
                          jnp.ones((2 * rv,), F32)])
    lg_cols = np.repeat(log_gamma, RET_QK_DIM)
    lgcol = jnp.asarray(np.concatenate([lg_cols, -lg_cols, np.zeros(2 * rv)]), dtype=F32)
    proj, = _norm_matmul(h, ret_norm[0], w_proj, cs, ret=(cos, sin, lgcol, 2 * D, rv, chunk))
    y, = _retention(proj, ret_gn[0], cd, batch=B, heads=ret_heads, chunk=chunk)
    h = _matmul_res(y, wo1, h)
    h, = _mlp(h, mlp_norm[1], up1, down1, final_norm)
    return h.reshape(B, S, D)
```

```python
import functools
import math

import numpy as np
import jax
import jax.numpy as jnp
from jax import lax
from jax.experimental import pallas as pl
from jax.experimental.pallas import tpu as pltpu

F32 = jnp.float32
BF16 = jnp.bfloat16

RMS_EPS = 1e-6
ROPE_BASE = 10000.0
FOX_HEAD_DIM = 128
RET_QK_DIM = 256
RET_V_DIM = 512
RET_CHUNK = 256
LANES = 128
BF16_ROWS = 16
NEG_BIG = -1e30
LOG2E = math.log2(math.e)
V7X_VMEM_LIMIT = 56 * 1024 * 1024

_NT = (((1,), (1,)), ((), ()))


def _params(sem, vmem=V7X_VMEM_LIMIT):
    return pltpu.CompilerParams(dimension_semantics=sem, vmem_limit_bytes=vmem)


def _tile(n, target, align=LANES):
    if n <= target:
        return n
    t = (target // align) * align
    while n % t:
        t -= align
    return t


def _rms_rows(x, g):
    ms = jnp.mean(x * x, axis=-1, keepdims=True)
    return x * lax.rsqrt(ms + RMS_EPS) * g


def _cast_concat_kernel(*refs, starts, nblks):
    o_ref = refs[-1]
    p = pl.program_id(0)
    for w_ref, s0, nb in zip(refs[:-1], starts, nblks):
        @pl.when((p >= s0) & (p < s0 + nb))
        def _(w_ref=w_ref):
            o_ref[...] = w_ref[...].astype(o_ref.dtype)


def _cast_concat(ws, layer, *, bw=2048, tr=512):
    R = ws[0].shape[1]
    bw = min(bw, min(w.shape[2] for w in ws))
    tr = _tile(R, tr)
    nr = R // tr
    nblks = [w.shape[2] // bw for w in ws]
    starts = [sum(nblks[:i]) for i in range(len(ws))]

    def in_map(s0, nb):
        def index(p, r):
            local = p - s0
            row = jnp.where(local < 0, 0, jnp.where(local >= nb, nr - 1, r))
            return layer, row, jnp.clip(local, 0, nb - 1)
        return index

    return pl.pallas_call(
        functools.partial(_cast_concat_kernel, starts=starts, nblks=nblks),
        grid=(sum(nblks), nr),
        in_specs=[pl.BlockSpec((None, tr, bw), in_map(s0, nb)) for s0, nb in zip(starts, nblks)],
        out_specs=pl.BlockSpec((tr, bw), lambda p, r: (r, p)),
        out_shape=jax.ShapeDtypeStruct((R, sum(nblks) * bw), BF16),
        compiler_params=_params(("arbitrary", "arbitrary")),
        name="cast_concat",
    )(*ws)


def _side_cast_plan(groups, grid):
    nsteps = math.prod(grid)

    def lin(*ids):
        l = ids[0]
        for n, i in zip(grid[1:], ids[1:]):
            l = l * n + i
        return l

    in_specs, args, out_specs, out_shapes, widths = [], [], [], [], []
    for ws, layer in groups:
        R = ws[0].shape[1]
        rows = BF16_ROWS
        while R // rows > nsteps:
            rows *= 2
        assert R % rows == 0
        nblk = R // rows
        rep = nsteps // nblk

        def blk(*ids, rep=rep, nblk=nblk):
            return jnp.minimum(lin(*ids) // rep, nblk - 1)

        for w in ws:
            in_specs.append(pl.BlockSpec((None, rows, w.shape[2]),
                                         lambda *ids, blk=blk, layer=layer: (layer, blk(*ids), 0)))
            args.append(w)
        cols = sum(w.shape[2] for w in ws)
        out_specs.append(pl.BlockSpec((rows, cols), lambda *ids, blk=blk: (blk(*ids), 0)))
        out_shapes.append(jax.ShapeDtypeStruct((R, cols), BF16))
        widths.append(tuple(w.shape[2] for w in ws))
    return in_specs, args, out_specs, out_shapes, tuple(widths)


def _side_cast_body(in_refs, out_refs, widths):
    k = 0
    for o_ref, ws in zip(out_refs, widths):
        c0 = 0
        for wd in ws:
            o_ref[:, c0:c0 + wd] = in_refs[k][...].astype(BF16)
            k += 1
            c0 += wd


def _n_cast_in(widths):
    return sum(len(ws) for ws in widths)


def _norm_matmul_kernel(x_ref, g_ref, w_ref, cs_ref, *rest, has_gate, ret, row_chunk, cast_widths):
    rest = list(rest)
    wz_ref = rest.pop(0) if has_gate else None
    if ret is not None:
        cos_ref, sin_ref, lg_ref = rest.pop(0), rest.pop(0), rest.pop(0)
    cast_in = [rest.pop(0) for _ in range(_n_cast_in(cast_widths))]
    o_ref = rest.pop(0)
    z_ref = rest.pop(0) if has_gate else None
    cast_out = [rest.pop(0) for _ in cast_widths]
    hn_ref, = rest
    tm = x_ref.shape[0]
    _side_cast_body(cast_in, cast_out, cast_widths)

    @pl.when(pl.program_id(1) == 0)
    def _():
        for r0 in range(0, tm, row_chunk):
            y = _rms_rows(x_ref[r0:r0 + row_chunk, :], g_ref[...])
            hn_ref[r0:r0 + row_chunk, :] = y.astype(BF16)
        if has_gate:
            z_ref[...] = jnp.dot(hn_ref[...], wz_ref[...], preferred_element_type=F32)

    tn = w_ref.shape[1]

    def heads(epilogue):
        for c0 in range(0, tn, RET_QK_DIM):
            acc = jnp.dot(hn_ref[...], w_ref[:, c0:c0 + RET_QK_DIM], preferred_element_type=F32)
            epilogue(c0, acc)

    def plain(c0, acc):
        cols = slice(c0, c0 + RET_QK_DIM)
        o_ref[:, cols] = (acc * cs_ref[:, cols]).astype(o_ref.dtype)

    if ret is None:
        heads(plain)
        return

    n_rot, n_plain, chunk = ret
    j = pl.program_id(1)
    half = RET_QK_DIM // 2

    @pl.when(j < n_rot)
    def _():
        t = (lax.broadcasted_iota(jnp.int32, (tm, half), 0) % chunk + 1).astype(F32)
        cos = cos_ref[...]
        sin = sin_ref[...]

        def rotary(c0, acc):
            h1 = slice(c0, c0 + half)
            h2 = slice(c0 + half, c0 + 2 * half)
            x1 = acc[:, :half]
            x2 = acc[:, half:]
            scale = jnp.exp(t * lg_ref[:, h1]) * cs_ref[:, h1]
            o_ref[:, h1] = ((x1 * cos - x2 * sin) * scale).astype(o_ref.dtype)
            o_ref[:, h2] = ((x1 * sin + x2 * cos) * scale).astype(o_ref.dtype)
        heads(rotary)

    @pl.when((j >= n_rot) & (j < n_rot + n_plain))
    def _():
        heads(plain)

    @pl.when(j >= n_rot + n_plain)
    def _():
        def swish(c0, acc):
            cols = slice(c0, c0 + RET_QK_DIM)
            y = acc * cs_ref[:, cols]
            sig = 0.5 * jnp.tanh(0.5 * y) + 0.5
            o_ref[:, cols] = (y * sig).astype(o_ref.dtype)
        heads(swish)


def _norm_matmul(x, g, w, colscale, wz=None, *, ret=None, casts=(), tm=1024, tn=2048):
    T, D = x.shape
    N = w.shape[1]
    tm = _tile(T, tm)
    tn = _tile(N if ret is None else math.gcd(ret[3], ret[4]), tn)
    has_gate = wz is not None
    grid = (T // tm, N // tn)
    c_in, c_args, c_out, c_shapes, c_widths = _side_cast_plan(casts, grid)
    in_specs = [
        pl.BlockSpec((tm, D), lambda i, j: (i, 0)),
        pl.BlockSpec((1, D), lambda i, j: (0, 0)),
        pl.BlockSpec((D, tn), lambda i, j: (0, j)),
        pl.BlockSpec((1, tn), lambda i, j: (0, j)),
    ]
    args = [x, g.reshape(1, D), w, colscale.reshape(1, N)]
    out_shape = [jax.ShapeDtypeStruct((T, N), BF16)]
    out_specs = [pl.BlockSpec((tm, tn), lambda i, j: (i, j))]
    if has_gate:
        in_specs.append(pl.BlockSpec((D, LANES), lambda i, j: (0, 0)))
        args.append(wz)
        out_shape.append(jax.ShapeDtypeStruct((T, LANES), F32))
        out_specs.append(pl.BlockSpec((tm, LANES), lambda i, j: (i, 0)))
    ret_static = None
    if ret is not None:
        cos, sin, lgcol, rot_cols, plain_cols, chunk = ret
        assert rot_cols % tn == 0 and plain_cols % tn == 0 and tm % chunk == 0
        nrt = cos.shape[0] // tm
        half = RET_QK_DIM // 2
        in_specs += [pl.BlockSpec((tm, half), lambda i, j: (i % nrt, 0)),
                     pl.BlockSpec((tm, half), lambda i, j: (i % nrt, 0)),
                     pl.BlockSpec((1, tn), lambda i, j: (0, j))]
        args += [cos, sin, lgcol.reshape(1, N)]
        ret_static = (rot_cols // tn, plain_cols // tn, chunk)
    res = pl.pallas_call(
        functools.partial(_norm_matmul_kernel, has_gate=has_gate, ret=ret_static,
                          row_chunk=min(256, tm), cast_widths=c_widths),
        grid=grid,
        in_specs=in_specs + c_in,
        out_specs=out_specs + c_out,
        out_shape=out_shape + c_shapes,
        scratch_shapes=[pltpu.VMEM((tm, D), BF16)],
        compiler_params=_params(("parallel", "arbitrary")),
        name="norm_matmul_gate" if has_gate else "norm_matmul",
    )(*args, *c_args)
    return res


def _gate_cumsum_kernel(z_ref, b_ref, cp_ref, *, heads):
    z = z_ref[...] + b_ref[...]
    c = jnp.minimum(z, 0.0) - jnp.log1p(jnp.exp(-jnp.abs(z)))
    S = c.shape[0]
    row = lax.broadcasted_iota(jnp.int32, c.shape, 0)
    shift = 1
    while shift < S:
        c = c + jnp.where(row >= shift, pltpu.roll(c, shift, axis=0), 0.0)
        shift *= 2
    c = c * LOG2E
    hi = c.astype(BF16).astype(F32)
    r1 = c - hi
    mid = r1.astype(BF16).astype(F32)
    lo = r1 - mid
    lane = lax.broadcasted_iota(jnp.int32, c.shape, 1)
    packed = jnp.where(lane < heads, hi,
                       jnp.where(lane < 2 * heads, pltpu.roll(mid, heads, axis=1),
                                 jnp.where(lane < 3 * heads, pltpu.roll(lo, 2 * heads, axis=1), 0.0)))
    cp_ref[...] = packed.astype(BF16)


def _gate_cumsum(z, b_pad, batch, heads):
    T = z.shape[0]
    S = T // batch
    assert 3 * heads <= LANES
    return pl.pallas_call(
        functools.partial(_gate_cumsum_kernel, heads=heads),
        grid=(batch,),
        in_specs=[pl.BlockSpec((S, LANES), lambda b: (b, 0)),
                  pl.BlockSpec((1, LANES), lambda b: (0, 0))],
        out_specs=pl.BlockSpec((S, LANES), lambda b: (b, 0)),
        out_shape=jax.ShapeDtypeStruct((T, LANES), BF16),
        compiler_params=_params(("parallel",)),
        name="gate_cumsum",
    )(z, b_pad)


ONES_ROWS = 16
N_PIECES = 3


def _piece_selector(head0, hpg, heads, sign, lane0):
    r = lax.broadcasted_iota(jnp.int32, (LANES, hpg * LANES), 0)
    col = lax.broadcasted_iota(jnp.int32, (LANES, hpg * LANES), 1)
    g = col // LANES
    p = col % LANES - lane0
    hit = (p >= 0) & (p < N_PIECES) & (r == head0 + g + heads * p)
    return jnp.where(hit, sign, 0.0).astype(BF16)


def _fox_attn_kernel(q_ref, k_ref, v_ref, cp_ref, *rest, tq, tk, hpg, heads, cast_widths):
    rest = list(rest)
    cast_in = [rest.pop(0) for _ in range(_n_cast_in(cast_widths))]
    o_ref = rest.pop(0)
    cast_out = [rest.pop(0) for _ in cast_widths]
    kx_ref, vt_ref, acc_ref, st_ref, p_ref = rest
    _side_cast_body(cast_in, cast_out, cast_widths)
    hg = pl.program_id(1)
    qi = pl.program_id(2)
    S = k_ref.shape[0]
    dh = FOX_HEAD_DIM
    lane = lax.broadcasted_iota(jnp.int32, (1, hpg * LANES), 1) % LANES
    cols = [slice(g * dh, (g + 1) * dh) for g in range(hpg)]

    @pl.when(qi == 0)
    def _():
        sel = _piece_selector(hg * hpg, hpg, heads, -1.0, 0)
        ones = jnp.where((lane >= N_PIECES) & (lane < 2 * N_PIECES), 1.0, 0.0)
        for g in range(hpg):
            vt_ref[g, dh:, :] = jnp.ones((ONES_ROWS, S), BF16)

        def body(i, carry):
            sl = pl.ds(pl.multiple_of(i * tq, tq), tq)
            ek = (jnp.dot(cp_ref[sl, :], sel, preferred_element_type=F32) + ones).astype(BF16)
            for g in range(hpg):
                kx_ref[g, sl, :dh] = k_ref[sl, cols[g]]
                kx_ref[g, sl, dh:] = ek[:, cols[g]]
                vt_ref[g, :dh, sl] = v_ref[sl, cols[g]].astype(F32).T.astype(BF16)
            return carry
        lax.fori_loop(0, S // tq, body, 0)

    q0 = pl.multiple_of(qi * tq, tq)
    sel_q = _piece_selector(hg * hpg, hpg, heads, 1.0, N_PIECES)
    cref = jnp.dot(cp_ref[pl.ds(q0, ONES_ROWS), :], sel_q, preferred_element_type=F32)[0:1, :]
    eq = (cref + jnp.where(lane < N_PIECES, 1.0, 0.0)).astype(BF16)
    qxt = [jnp.concatenate([q_ref[:, cols[g]].astype(F32),
                            jnp.broadcast_to(eq[:, cols[g]].astype(F32), (tq, dh))], axis=1
                           ).T.astype(BF16) for g in range(hpg)]

    def scores(g, j, diag):
        k0 = pl.multiple_of(j * tk, tk)
        st = jnp.dot(kx_ref[g, pl.ds(k0, tk), :], qxt[g], preferred_element_type=F32)
        if diag is not None:
            kk = lax.broadcasted_iota(jnp.int32, st.shape, 0) + diag * tk
            qq = lax.broadcasted_iota(jnp.int32, st.shape, 1)
            st = jnp.where(kk <= qq, st, NEG_BIG)
        return st, jnp.max(st, axis=0, keepdims=True)

    def pv_update(g, j, slot, alpha):
        k0 = pl.multiple_of(j * tk, tk)
        pv = jnp.dot(vt_ref[g, :, pl.ds(k0, tk)], p_ref[g, slot], preferred_element_type=F32)
        acc_ref[g] = alpha * acc_ref[g] + pv

    def stage(g, slot, carry, prev_blk, next_blk, next_diag=None):
        m, alpha_prev, cmax = carry
        if prev_blk is not None:
            pv_update(g, prev_blk, 1 - slot, alpha_prev)
        st_next, cmax_next = scores(g, next_blk, next_diag)
        st_ref[g, 1 - slot] = st_next
        m_new = jnp.maximum(m, cmax)
        p_ref[g, slot] = jnp.exp2(st_ref[g, slot] - m_new).astype(BF16)
        return m_new, jnp.exp2(m - m_new), cmax_next

    assert tq == 2 * tk
    d0 = 2 * qi
    carries = []
    for g in range(hpg):
        acc_ref[g] = jnp.zeros(acc_ref.shape[1:], F32)
        st, cmax = scores(g, d0, 0)
        st_ref[g, 0] = st
        c = (jnp.full((1, tq), NEG_BIG, F32), jnp.ones((1, tq), F32), cmax)
        c = stage(g, 0, c, None, d0 + 1, 1)
        carries.append(stage(g, 1, c, d0, 0))

    def pair(i, carries):
        out = []
        for g in range(hpg):
            c = stage(g, 0, carries[g], jnp.where(i == 0, d0 + 1, 2 * i - 1), 2 * i + 1)
            out.append(stage(g, 1, c, 2 * i, 2 * i + 2))
        return tuple(out)

    carries = lax.fori_loop(0, qi, pair, tuple(carries))
    last_blk = jnp.where(qi == 0, d0 + 1, d0 - 1)
    for g in range(hpg):
        pv_update(g, last_blk, 1, carries[g][1])
        ot = acc_ref[g, :dh, :] / acc_ref[g, dh:dh + 1, :]
        o_ref[:, cols[g]] = ot.T.astype(o_ref.dtype)


def _fox_attention(qkv, cp, *, batch, heads, casts=(), tq=512, hpg=4):
    T = qkv.shape[0]
    S = T // batch
    dh = FOX_HEAD_DIM
    tq = min(tq, S)
    tk = tq // 2
    nq = S // tq
    ng = heads // hpg
    w = hpg * dh
    grid = (batch, ng, nq)
    c_in, c_args, c_out, c_shapes, c_widths = _side_cast_plan(casts, grid)
    return pl.pallas_call(
        functools.partial(_fox_attn_kernel, tq=tq, tk=tk, hpg=hpg, heads=heads,
                          cast_widths=c_widths),
        grid=grid,
        in_specs=[
            pl.BlockSpec((tq, w), lambda b, h, i: (b * nq + i, h)),
            pl.BlockSpec((S, w), lambda b, h, i: (b, ng + h)),
            pl.BlockSpec((S, w), lambda b, h, i: (b, 2 * ng + h)),
            pl.BlockSpec((S, LANES), lambda b, h, i: (b, 0)),
        ] + c_in,
        out_specs=[pl.BlockSpec((tq, w), lambda b, h, i: (b * nq + i, h))] + c_out,
        out_shape=[jax.ShapeDtypeStruct((T, heads * dh), BF16)] + c_shapes,
        scratch_shapes=[pltpu.VMEM((hpg, S, 2 * dh), BF16),
                        pltpu.VMEM((hpg, dh + ONES_ROWS, S), BF16),
                        pltpu.VMEM((hpg, dh + ONES_ROWS, tq), F32),
                        pltpu.VMEM((hpg, 2, tk, tq), F32),
                        pltpu.VMEM((hpg, 2, tk, tq), BF16)],
        compiler_params=_params(("parallel", "parallel", "arbitrary")),
        name="fox_attention",
    )(qkv, qkv, qkv, cp, *c_args)


MXU_COLS = 256


def _matmul_res_kernel(a_ref, w_ref, r_ref, o_ref):
    for c0 in range(0, w_ref.shape[1], MXU_COLS):
        cols = slice(c0, c0 + MXU_COLS)
        o_ref[:, cols] = r_ref[:, cols] + jnp.dot(a_ref[...], w_ref[:, cols],
                                                  preferred_element_type=F32)


def _matmul_res(a, w, res, *, tm=512):
    T, K = a.shape
    N = w.shape[1]
    tm = _tile(T, tm)
    return pl.pallas_call(
        _matmul_res_kernel,
        grid=(T // tm,),
        in_specs=[pl.BlockSpec((tm, K), lambda i: (i, 0)),
                  pl.BlockSpec((K, N), lambda i: (0, 0), pipeline_mode=pl.Buffered(1)),
                  pl.BlockSpec((tm, N), lambda i: (i, 0))],
        out_specs=pl.BlockSpec((tm, N), lambda i: (i, 0)),
        out_shape=jax.ShapeDtypeStruct((T, N), F32),
        compiler_params=_params(("parallel",)),
        name="matmul_res",
    )(a, w, res)


def _mlp_kernel(x_ref, g_ref, wu_ref, wd_ref, *rest, has_final, row_chunk, cast_widths):
    rest = list(rest)
    gf_ref = rest.pop(0) if has_final else None
    cast_in = [rest.pop(0) for _ in range(_n_cast_in(cast_widths))]
    o_ref = rest.pop(0)
    cast_out = [rest.pop(0) for _ in cast_widths]
    hn_ref, = rest
    _side_cast_body(cast_in, cast_out, cast_widths)
    tm = x_ref.shape[0]
    f = pl.program_id(1)
    chunks = [slice(r0, r0 + row_chunk) for r0 in range(0, tm, row_chunk)]

    @pl.when(f == 0)
    def _():
        for sl in chunks:
            x = x_ref[sl, :]
            hn_ref[sl, :] = _rms_rows(x, g_ref[...]).astype(BF16)
            o_ref[sl, :] = x

    u = jnp.dot(hn_ref[...], wu_ref[...], preferred_element_type=F32)
    a = jnp.square(jnp.maximum(u, 0.0)).astype(BF16)
    o_ref[...] += jnp.dot(a, wd_ref[...], preferred_element_type=F32)

    if has_final:
        @pl.when(f == pl.num_programs(1) - 1)
        def _():
            for sl in chunks:
                o_ref[sl, :] = _rms_rows(o_ref[sl, :], gf_ref[...])


def _mlp(x, g, wu, wd, g_final=None, *, casts=(), tm=1024, tf=512):
    T, D = x.shape
    Fdim = wu.shape[1]
    tm = _tile(T, tm)
    tf = _tile(Fdim, tf)
    has_final = g_final is not None
    in_specs = [pl.BlockSpec((tm, D), lambda i, f: (i, 0)),
                pl.BlockSpec((1, D), lambda i, f: (0, 0)),
                pl.BlockSpec((D, tf), lambda i, f: (0, f)),
                pl.BlockSpec((tf, D), lambda i, f: (f, 0))]
    args = [x, g.reshape(1, D), wu, wd]
    if has_final:
        in_specs.append(pl.BlockSpec((1, D), lambda i, f: (0, 0)))
        args.append(g_final.reshape(1, D))
    grid = (T // tm, Fdim // tf)
    c_in, c_args, c_out, c_shapes, c_widths = _side_cast_plan(casts, grid)
    return pl.pallas_call(
        functools.partial(_mlp_kernel, has_final=has_final, row_chunk=min(256, tm),
                          cast_widths=c_widths),
        grid=grid,
        in_specs=in_specs + c_in,
        out_specs=[pl.BlockSpec((tm, D), lambda i, f: (i, 0))] + c_out,
        out_shape=[jax.ShapeDtypeStruct((T, D), F32)] + c_shapes,
        scratch_shapes=[pltpu.VMEM((tm, D), BF16)],
        compiler_params=_params(("parallel", "arbitrary")),
        name="mlp_final" if has_final else "mlp",
    )(*args, *c_args)


def _retention_kernel(q_ref, k_ref, v_ref, sg_ref, cd_ref, gn_ref, *rest, chunk, nchunks,
                      cast_widths):
    rest = list(rest)
    cast_in = [rest.pop(0) for _ in range(_n_cast_in(cast_widths))]
    o_ref = rest.pop(0)
    cast_out = [rest.pop(0) for _ in cast_widths]
    state_ref, = rest
    _side_cast_body(cast_in, cast_out, cast_widths)

    @pl.when(pl.program_id(2) == 0)
    def _():
        state_ref[...] = jnp.zeros_like(state_ref)

    causal = (lax.broadcasted_iota(jnp.int32, (chunk, chunk), 0)
              >= lax.broadcasted_iota(jnp.int32, (chunk, chunk), 1))
    for ci in range(nchunks):
        sl = slice(ci * chunk, (ci + 1) * chunk)
        q = q_ref[sl, :]
        k = k_ref[sl, :]
        v = v_ref[sl, :]
        state = state_ref[...]

        inner = lax.dot_general(q, k, _NT, preferred_element_type=F32)
        inner = jnp.where(causal, inner, 0.0).astype(BF16)
        o = jnp.dot(inner, v, preferred_element_type=F32)
        o = o + jnp.dot(q, state.astype(BF16), preferred_element_type=F32)
        k_t = k.astype(F32).T.astype(BF16)
        state_ref[...] = cd_ref[...] * (state + jnp.dot(k_t, v, preferred_element_type=F32))

        ms = jnp.mean(o * o, axis=-1, keepdims=True)
        on = o * lax.rsqrt(ms + RMS_EPS) * gn_ref[...]
        o_ref[sl, :] = (sg_ref[sl, :].astype(F32) * on).astype(o_ref.dtype)


def _retention_tables(heads, chunk, seq):
    dk, dv = RET_QK_DIM, RET_V_DIM
    half = dk // 2
    pos = np.arange(seq, dtype=np.float64)
    inv = ROPE_BASE ** (-np.arange(half, dtype=np.float64) / half)
    ang = pos[:, None] * inv[None, :]
    log_gamma = np.log1p(-np.exp2(-5.0 - np.arange(heads, dtype=np.float64)))
    chunk_decay = np.broadcast_to(np.exp(log_gamma * chunk)[:, None, None], (heads, 1, dv))
    return (jnp.asarray(np.cos(ang), dtype=F32), jnp.asarray(np.sin(ang), dtype=F32),
            log_gamma, jnp.asarray(chunk_decay, dtype=F32))


def _retention(proj, gn, cd, *, batch, heads, chunk, casts=(), rows=1024):
    T = proj.shape[0]
    S = T // batch
    dk, dv = RET_QK_DIM, RET_V_DIM
    rows = min(rows, S)
    ns = S // rows
    vblk0 = 2 * heads * dk // dv
    grid = (batch, heads, ns)
    c_in, c_args, c_out, c_shapes, c_widths = _side_cast_plan(casts, grid)
    return pl.pallas_call(
        functools.partial(_retention_kernel, chunk=chunk, nchunks=rows // chunk,
                          cast_widths=c_widths),
        grid=grid,
        in_specs=[
            pl.BlockSpec((rows, dk), lambda b, h, s: (b * ns + s, h)),
            pl.BlockSpec((rows, dk), lambda b, h, s: (b * ns + s, heads + h)),
            pl.BlockSpec((rows, dv), lambda b, h, s: (b * ns + s, vblk0 + h)),
            pl.BlockSpec((rows, dv), lambda b, h, s: (b * ns + s, vblk0 + heads + h)),
            pl.BlockSpec((None, 1, dv), lambda b, h, s: (h, 0, 0)),
            pl.BlockSpec((1, dv), lambda b, h, s: (0, h)),
        ] + c_in,
        out_specs=[pl.BlockSpec((rows, dv), lambda b, h, s: (b * ns + s, h))] + c_out,
        out_shape=[jax.ShapeDtypeStruct((T, heads * dv), BF16)] + c_shapes,
        scratch_shapes=[pltpu.VMEM((dk, dv), F32)],
        compiler_params=_params(("parallel", "parallel", "arbitrary")),
        name="retention",
    )(proj, proj, proj, proj, cd, gn.reshape(1, heads * dv), *c_args)


def kernel(x, fox_norm, fox_wq, fox_wk, fox_wv, fox_wf, fox_bf, fox_wo, ret_norm, ret_wq, ret_wk,
           ret_wv, ret_wg, ret_gn, ret_wo, mlp_norm, mlp_up, mlp_down, final_norm):
    B, S, D = x.shape
    T = B * S
    fox_heads = D // FOX_HEAD_DIM
    ret_heads = D // RET_QK_DIM
    h = x.reshape(T, D)

    w_qkv = _cast_concat([fox_wq, fox_wk, fox_wv], 0)
    cs = jnp.concatenate([jnp.full((D,), LOG2E * FOX_HEAD_DIM ** -0.5, F32),
                          jnp.ones((2 * D,), F32)])
    wf_pad = jnp.pad(fox_wf[0], ((0, 0), (0, LANES - fox_heads))).astype(BF16)
    bf_pad = jnp.pad(fox_bf[0], (0, LANES - fox_heads)).reshape(1, LANES)
    qkv, z = _norm_matmul(h, fox_norm[0], w_qkv, cs, wf_pad)
    cp = _gate_cumsum(z, bf_pad, B, fox_heads)
    attn, wo0, up0, down0, w_proj, wo1, up1, down1 = _fox_attention(
        qkv, cp, batch=B, heads=fox_heads,
        casts=[([fox_wo], 0), ([mlp_up], 0), ([mlp_down], 0),
               ([ret_wq, ret_wk, ret_wv, ret_wg], 0), ([ret_wo], 0),
               ([mlp_up], 1), ([mlp_down], 1)])
    h = _matmul_res(attn, wo0, h)
    h, = _mlp(h, mlp_norm[0], up0, down0)

    chunk = min(RET_CHUNK, S)
    cos, sin, log_gamma, cd = _retention_tables(ret_heads, chunk, S)
    rv = ret_heads * RET_V_DIM
    cs = jnp.concatenate([jnp.ones((D,), F32), jnp.full((D,), RET_QK_DIM ** -0.5, F32),
                          jnp.ones((2 * rv,), F32)])
    lg_cols = np.repeat(log_gamma, RET_QK_DIM)
    lgcol = jnp.asarray(np.concatenate([lg_cols, -lg_cols, np.zeros(2 * rv)]), dtype=F32)
    proj, = _norm_matmul(h, ret_norm[0], w_proj, cs, ret=(cos, sin, lgcol, 2 * D, rv, chunk))
    y, = _retention(proj, ret_gn[0], cd, batch=B, heads=ret_heads, chunk=chunk)
    h = _matmul_res(y, wo1, h)
    h, = _mlp(h, mlp_norm[1], up1, down1, final_norm)
    return h.reshape(B, S, D)
```

```python
import functools
import math

import numpy as np
import jax
import jax.numpy as jnp
from jax import lax
from jax.experimental import pallas as pl
from jax.experimental.pallas import tpu as pltpu

F32 = jnp.float32
BF16 = jnp.bfloat16

RMS_EPS = 1e-6
ROPE_BASE = 10000.0
FOX_HEAD_DIM = 128
RET_QK_DIM = 256
RET_V_DIM = 512
RET_CHUNK = 256
LANES = 128
BF16_ROWS = 16
NEG_BIG = -1e30
LOG2E = math.log2(math.e)
V7X_VMEM_LIMIT = 56 * 1024 * 1024

_NT = (((1,), (1,)), ((), ()))


def _params(sem, vmem=V7X_VMEM_LIMIT):
    return pltpu.CompilerParams(dimension_semantics=sem, vmem_limit_bytes=vmem)


def _tile(n, target, align=LANES):
    if n <= target:
        return n
    t = (target // align) * align
    while n % t:
        t -= align
    return t


def _rms_rows(x, g):
    ms = jnp.mean(x * x, axis=-1, keepdims=True)
    return x * lax.rsqrt(ms + RMS_EPS) * g


def _cast_concat_kernel(*refs, starts, nblks):
    o_ref = refs[-1]
    p = pl.program_id(0)
    for w_ref, s0, nb in zip(refs[:-1], starts, nblks):
        @pl.when((p >= s0) & (p < s0 + nb))
        def _(w_ref=w_ref):
            o_ref[...] = w_ref[...].astype(o_ref.dtype)


def _cast_concat(ws, layer, *, bw=2048, tr=512):
    R = ws[0].shape[1]
    bw = min(bw, min(w.shape[2] for w in ws))
    tr = _tile(R, tr)
    nr = R // tr
    nblks = [w.shape[2] // bw for w in ws]
    starts = [sum(nblks[:i]) for i in range(len(ws))]

    def in_map(s0, nb):
        def index(p, r):
            local = p - s0
            row = jnp.where(local < 0, 0, jnp.where(local >= nb, nr - 1, r))
            return layer, row, jnp.clip(local, 0, nb - 1)
        return index

    return pl.pallas_call(
        functools.partial(_cast_concat_kernel, starts=starts, nblks=nblks),
        grid=(sum(nblks), nr),
        in_specs=[pl.BlockSpec((None, tr, bw), in_map(s0, nb)) for s0, nb in zip(starts, nblks)],
        out_specs=pl.BlockSpec((tr, bw), lambda p, r: (r, p)),
        out_shape=jax.ShapeDtypeStruct((R, sum(nblks) * bw), BF16),
        compiler_params=_params(("arbitrary", "arbitrary")),
        name="cast_concat",
    )(*ws)


def _side_cast_plan(groups, grid):
    nsteps = math.prod(grid)

    def lin(*ids):
        l = ids[0]
        for n, i in zip(grid[1:], ids[1:]):
            l = l * n + i
        return l

    in_specs, args, out_specs, out_shapes, widths = [], [], [], [], []
    for ws, layer in groups:
        R = ws[0].shape[1]
        rows = BF16_ROWS
        while R // rows > nsteps:
            rows *= 2
        assert R % rows == 0
        nblk = R // rows
        rep = nsteps // nblk

        def blk(*ids, rep=rep, nblk=nblk):
            return jnp.minimum(lin(*ids) // rep, nblk - 1)

        for w in ws:
            in_specs.append(pl.BlockSpec((None, rows, w.shape[2]),
                                         lambda *ids, blk=blk, layer=layer: (layer, blk(*ids), 0)))
            args.append(w)
        cols = sum(w.shape[2] for w in ws)
        out_specs.append(pl.BlockSpec((rows, cols), lambda *ids, blk=blk: (blk(*ids), 0)))
        out_shapes.append(jax.ShapeDtypeStruct((R, cols), BF16))
        widths.append(tuple(w.shape[2] for w in ws))
    return in_specs, args, out_specs, out_shapes, tuple(widths)


def _side_cast_body(in_refs, out_refs, widths):
    k = 0
    for o_ref, ws in zip(out_refs, widths):
        c0 = 0
        for wd in ws:
            o_ref[:, c0:c0 + wd] = in_refs[k][...].astype(BF16)
            k += 1
            c0 += wd


def _n_cast_in(widths):
    return sum(len(ws) for ws in widths)


def _norm_matmul_kernel(x_ref, g_ref, w_ref, cs_ref, *rest, has_gate, ret, row_chunk, cast_widths):
    rest = list(rest)
    wz_ref = rest.pop(0) if has_gate else None
    if ret is not None:
        cos_ref, sin_ref, lg_ref = rest.pop(0), rest.pop(0), rest.pop(0)
    cast_in = [rest.pop(0) for _ in range(_n_cast_in(cast_widths))]
    o_ref = rest.pop(0)
    z_ref = rest.pop(0) if has_gate else None
    cast_out = [rest.pop(0) for _ in cast_widths]
    hn_ref, = rest
    tm = x_ref.shape[0]
    _side_cast_body(cast_in, cast_out, cast_widths)

    @pl.when(pl.program_id(1) == 0)
    def _():
        for r0 in range(0, tm, row_chunk):
            y = _rms_rows(x_ref[r0:r0 + row_chunk, :], g_ref[...])
            hn_ref[r0:r0 + row_chunk, :] = y.astype(BF16)
        if has_gate:
            z_ref[...] = jnp.dot(hn_ref[...], wz_ref[...], preferred_element_type=F32)

    tn = w_ref.shape[1]

    def heads(epilogue):
        for c0 in range(0, tn, RET_QK_DIM):
            acc = jnp.dot(hn_ref[...], w_ref[:, c0:c0 + RET_QK_DIM], preferred_element_type=F32)
            epilogue(c0, acc)

    def plain(c0, acc):
        cols = slice(c0, c0 + RET_QK_DIM)
        o_ref[:, cols] = (acc * cs_ref[:, cols]).astype(o_ref.dtype)

    if ret is None:
        heads(plain)
        return

    n_rot, n_plain, chunk = ret
    j = pl.program_id(1)
    half = RET_QK_DIM // 2

    @pl.when(j < n_rot)
    def _():
        t = (lax.broadcasted_iota(jnp.int32, (tm, half), 0) % chunk + 1).astype(F32)
        cos = cos_ref[...]
        sin = sin_ref[...]

        def rotary(c0, acc):
            h1 = slice(c0, c0 + half)
            h2 = slice(c0 + half, c0 + 2 * half)
            x1 = acc[:, :half]
            x2 = acc[:, half:]
            scale = jnp.exp(t * lg_ref[:, h1]) * cs_ref[:, h1]
            o_ref[:, h1] = ((x1 * cos - x2 * sin) * scale).astype(o_ref.dtype)
            o_ref[:, h2] = ((x1 * sin + x2 * cos) * scale).astype(o_ref.dtype)
        heads(rotary)

    @pl.when((j >= n_rot) & (j < n_rot + n_plain))
    def _():
        heads(plain)

    @pl.when(j >= n_rot + n_plain)
    def _():
        def swish(c0, acc):
            cols = slice(c0, c0 + RET_QK_DIM)
            y = acc * cs_ref[:, cols]
            sig = 0.5 * jnp.tanh(0.5 * y) + 0.5
            o_ref[:, cols] = (y * sig).astype(o_ref.dtype)
        heads(swish)


def _norm_matmul(x, g, w, colscale, wz=None, *, ret=None, casts=(), tm=1024, tn=2048):
    T, D = x.shape
    N = w.shape[1]
    tm = _tile(T, tm)
    tn = _tile(N if ret is None else math.gcd(ret[3], ret[4]), tn)
    has_gate = wz is not None
    grid = (T // tm, N // tn)
    c_in, c_args, c_out, c_shapes, c_widths = _side_cast_plan(casts, grid)
    in_specs = [
        pl.BlockSpec((tm, D), lambda i, j: (i, 0)),
        pl.BlockSpec((1, D), lambda i, j: (0, 0)),
        pl.BlockSpec((D, tn), lambda i, j: (0, j)),
        pl.BlockSpec((1, tn), lambda i, j: (0, j)),
    ]
    args = [x, g.reshape(1, D), w, colscale.reshape(1, N)]
    out_shape = [jax.ShapeDtypeStruct((T, N), BF16)]
    out_specs = [pl.BlockSpec((tm, tn), lambda i, j: (i, j))]
    if has_gate:
        in_specs.append(pl.BlockSpec((D, LANES), lambda i, j: (0, 0)))
        args.append(wz)
        out_shape.append(jax.ShapeDtypeStruct((T, LANES), F32))
        out_specs.append(pl.BlockSpec((tm, LANES), lambda i, j: (i, 0)))
    ret_static = None
    if ret is not None:
        cos, sin, lgcol, rot_cols, plain_cols, chunk = ret
        assert rot_cols % tn == 0 and plain_cols % tn == 0 and tm % chunk == 0
        nrt = cos.shape[0] // tm
        half = RET_QK_DIM // 2
        in_specs += [pl.BlockSpec((tm, half), lambda i, j: (i % nrt, 0)),
                     pl.BlockSpec((tm, half), lambda i, j: (i % nrt, 0)),
                     pl.BlockSpec((1, tn), lambda i, j: (0, j))]
        args += [cos, sin, lgcol.reshape(1, N)]
        ret_static = (rot_cols // tn, plain_cols // tn, chunk)
    res = pl.pallas_call(
        functools.partial(_norm_matmul_kernel, has_gate=has_gate, ret=ret_static,
                          row_chunk=min(256, tm), cast_widths=c_widths),
        grid=grid,
        in_specs=in_specs + c_in,
        out_specs=out_specs + c_out,
        out_shape=out_shape + c_shapes,
        scratch_shapes=[pltpu.VMEM((tm, D), BF16)],
        compiler_params=_params(("parallel", "arbitrary")),
        name="norm_matmul_gate" if has_gate else "norm_matmul",
    )(*args, *c_args)
    return res


def _gate_cumsum_kernel(z_ref, b_ref, cp_ref, *, heads):
    z = z_ref[...] + b_ref[...]
    c = jnp.minimum(z, 0.0) - jnp.log1p(jnp.exp(-jnp.abs(z)))
    S = c.shape[0]
    row = lax.broadcasted_iota(jnp.int32, c.shape, 0)
    shift = 1
    while shift < S:
        c = c + jnp.where(row >= shift, pltpu.roll(c, shift, axis=0), 0.0)
        shift *= 2
    c = c * LOG2E
    hi = c.astype(BF16).astype(F32)
    r1 = c - hi
    mid = r1.astype(BF16).astype(F32)
    lo = r1 - mid
    lane = lax.broadcasted_iota(jnp.int32, c.shape, 1)
    packed = jnp.where(lane < heads, hi,
                       jnp.where(lane < 2 * heads, pltpu.roll(mid, heads, axis=1),
                                 jnp.where(lane < 3 * heads, pltpu.roll(lo, 2 * heads, axis=1), 0.0)))
    cp_ref[...] = packed.astype(BF16)


def _gate_cumsum(z, b_pad, batch, heads):
    T = z.shape[0]
    S = T // batch
    assert 3 * heads <= LANES
    return pl.pallas_call(
        functools.partial(_gate_cumsum_kernel, heads=heads),
        grid=(batch,),
        in_specs=[pl.BlockSpec((S, LANES), lambda b: (b, 0)),
                  pl.BlockSpec((1, LANES), lambda b: (0, 0))],
        out_specs=pl.BlockSpec((S, LANES), lambda b: (b, 0)),
        out_shape=jax.ShapeDtypeStruct((T, LANES), BF16),
        compiler_params=_params(("parallel",)),
        name="gate_cumsum",
    )(z, b_pad)


ONES_ROWS = 16
N_PIECES = 3


def _piece_selector(head0, hpg, heads, sign, lane0):
    r = lax.broadcasted_iota(jnp.int32, (LANES, hpg * LANES), 0)
    col = lax.broadcasted_iota(jnp.int32, (LANES, hpg * LANES), 1)
    g = col // LANES
    p = col % LANES - lane0
    hit = (p >= 0) & (p < N_PIECES) & (r == head0 + g + heads * p)
    return jnp.where(hit, sign, 0.0).astype(BF16)


def _fox_attn_kernel(q_ref, k_ref, v_ref, cp_ref, *rest, tq, tk, hpg, heads, cast_widths):
    rest = list(rest)
    cast_in = [rest.pop(0) for _ in range(_n_cast_in(cast_widths))]
    o_ref = rest.pop(0)
    cast_out = [rest.pop(0) for _ in cast_widths]
    kx_ref, vt_ref, acc_ref, st_ref, p_ref = rest
    _side_cast_body(cast_in, cast_out, cast_widths)
    hg = pl.program_id(1)
    qi = pl.program_id(2)
    S = k_ref.shape[0]
    dh = FOX_HEAD_DIM
    lane = lax.broadcasted_iota(jnp.int32, (1, hpg * LANES), 1) % LANES
    cols = [slice(g * dh, (g + 1) * dh) for g in range(hpg)]

    @pl.when(qi == 0)
    def _():
        sel = _piece_selector(hg * hpg, hpg, heads, -1.0, 0)
        ones = jnp.where((lane >= N_PIECES) & (lane < 2 * N_PIECES), 1.0, 0.0)
        for g in range(hpg):
            vt_ref[g, dh:, :] = jnp.ones((ONES_ROWS, S), BF16)

        def body(i, carry):
            sl = pl.ds(pl.multiple_of(i * tq, tq), tq)
            ek = (jnp.dot(cp_ref[sl, :], sel, preferred_element_type=F32) + ones).astype(BF16)
            for g in range(hpg):
                kx_ref[g, sl, :dh] = k_ref[sl, cols[g]]
                kx_ref[g, sl, dh:] = ek[:, cols[g]]
                vt_ref[g, :dh, sl] = v_ref[sl, cols[g]].astype(F32).T.astype(BF16)
            return carry
        lax.fori_loop(0, S // tq, body, 0)

    q0 = pl.multiple_of(qi * tq, tq)
    sel_q = _piece_selector(hg * hpg, hpg, heads, 1.0, N_PIECES)
    cref = jnp.dot(cp_ref[pl.ds(q0, ONES_ROWS), :], sel_q, preferred_element_type=F32)[0:1, :]
    eq = (cref + jnp.where(lane < N_PIECES, 1.0, 0.0)).astype(BF16)
    qxt = [jnp.concatenate([q_ref[:, cols[g]].astype(F32),
                            jnp.broadcast_to(eq[:, cols[g]].astype(F32), (tq, dh))], axis=1
                           ).T.astype(BF16) for g in range(hpg)]

    def scores(g, j, diag):
        k0 = pl.multiple_of(j * tk, tk)
        st = jnp.dot(kx_ref[g, pl.ds(k0, tk), :], qxt[g], preferred_element_type=F32)
        if diag is not None:
            kk = lax.broadcasted_iota(jnp.int32, st.shape, 0) + diag * tk
            qq = lax.broadcasted_iota(jnp.int32, st.shape, 1)
            st = jnp.where(kk <= qq, st, NEG_BIG)
        return st, jnp.max(st, axis=0, keepdims=True)

    def pv_update(g, j, slot, alpha):
        k0 = pl.multiple_of(j * tk, tk)
        pv = jnp.dot(vt_ref[g, :, pl.ds(k0, tk)], p_ref[g, slot], preferred_element_type=F32)
        acc_ref[g] = alpha * acc_ref[g] + pv

    def stage(g, slot, carry, prev_blk, next_blk, next_diag=None):
        m, alpha_prev, cmax = carry
        if prev_blk is not None:
            pv_update(g, prev_blk, 1 - slot, alpha_prev)
        st_next, cmax_next = scores(g, next_blk, next_diag)
        st_ref[g, 1 - slot] = st_next
        m_new = jnp.maximum(m, cmax)
        p_ref[g, slot] = jnp.exp2(st_ref[g, slot] - m_new).astype(BF16)
        return m_new, jnp.exp2(m - m_new), cmax_next

    assert tq == 2 * tk
    d0 = 2 * qi
    carries = []
    for g in range(hpg):
        acc_ref[g] = jnp.zeros(acc_ref.shape[1:], F32)
        st, cmax = scores(g, d0, 0)
        st_ref[g, 0] = st
        carries.append((jnp.full((1, tq), NEG_BIG, F32), jnp.ones((1, tq), F32), cmax))
    carries = [stage(g, 0, carries[g], None, d0 + 1, 1) for g in range(hpg)]
    carries = [stage(g, 1, carries[g], d0, 0) for g in range(hpg)]

    def pair(i, carries):
        out = []
        for g in range(hpg):
            c = stage(g, 0, carries[g], jnp.where(i == 0, d0 + 1, 2 * i - 1), 2 * i + 1)
            out.append(stage(g, 1, c, 2 * i, 2 * i + 2))
        return tuple(out)

    carries = lax.fori_loop(0, qi, pair, tuple(carries))
    last_blk = jnp.where(qi == 0, d0 + 1, d0 - 1)
    for g in range(hpg):
        pv_update(g, last_blk, 1, carries[g][1])
    for g in range(hpg):
        ot = acc_ref[g, :dh, :] / acc_ref[g, dh:dh + 1, :]
        o_ref[:, cols[g]] = ot.T.astype(o_ref.dtype)


def _fox_attention(qkv, cp, *, batch, heads, casts=(), tq=512, hpg=4):
    T = qkv.shape[0]
    S = T // batch
    dh = FOX_HEAD_DIM
    tq = min(tq, S)
    tk = tq // 2
    nq = S // tq
    ng = heads // hpg
    w = hpg * dh
    grid = (batch, ng, nq)
    c_in, c_args, c_out, c_shapes, c_widths = _side_cast_plan(casts, grid)
    return pl.pallas_call(
        functools.partial(_fox_attn_kernel, tq=tq, tk=tk, hpg=hpg, heads=heads,
                          cast_widths=c_widths),
        grid=grid,
        in_specs=[
            pl.BlockSpec((tq, w), lambda b, h, i: (b * nq + i, h)),
            pl.BlockSpec((S, w), lambda b, h, i: (b, ng + h)),
            pl.BlockSpec((S, w), lambda b, h, i: (b, 2 * ng + h)),
            pl.BlockSpec((S, LANES), lambda b, h, i: (b, 0)),
        ] + c_in,
        out_specs=[pl.BlockSpec((tq, w), lambda b, h, i: (b * nq + i, h))] + c_out,
        out_shape=[jax.ShapeDtypeStruct((T, heads * dh), BF16)] + c_shapes,
        scratch_shapes=[pltpu.VMEM((hpg, S, 2 * dh), BF16),
                        pltpu.VMEM((hpg, dh + ONES_ROWS, S), BF16),
                        pltpu.VMEM((hpg, dh + ONES_ROWS, tq), F32),
                        pltpu.VMEM((hpg, 2, tk, tq), F32),
                        pltpu.VMEM((hpg, 2, tk, tq), BF16)],
        compiler_params=_params(("parallel", "parallel", "arbitrary")),
        name="fox_attention",
    )(qkv, qkv, qkv, cp, *c_args)


MXU_COLS = 256


def _matmul_res_kernel(a_ref, w_ref, r_ref, o_ref):
    for c0 in range(0, w_ref.shape[1], MXU_COLS):
        cols = slice(c0, c0 + MXU_COLS)
        o_ref[:, cols] = r_ref[:, cols] + jnp.dot(a_ref[...], w_ref[:, cols],
                                                  preferred_element_type=F32)


def _matmul_res(a, w, res, *, tm=512):
    T, K = a.shape
    N = w.shape[1]
    tm = _tile(T, tm)
    return pl.pallas_call(
        _matmul_res_kernel,
        grid=(T // tm,),
        in_specs=[pl.BlockSpec((tm, K), lambda i: (i, 0)),
                  pl.BlockSpec((K, N), lambda i: (0, 0), pipeline_mode=pl.Buffered(1)),
                  pl.BlockSpec((tm, N), lambda i: (i, 0))],
        out_specs=pl.BlockSpec((tm, N), lambda i: (i, 0)),
        out_shape=jax.ShapeDtypeStruct((T, N), F32),
        compiler_params=_params(("parallel",)),
        name="matmul_res",
    )(a, w, res)


def _mlp_kernel(x_ref, g_ref, wu_ref, wd_ref, *rest, has_final, row_chunk, cast_widths):
    rest = list(rest)
    gf_ref = rest.pop(0) if has_final else None
    cast_in = [rest.pop(0) for _ in range(_n_cast_in(cast_widths))]
    o_ref = rest.pop(0)
    cast_out = [rest.pop(0) for _ in cast_widths]
    hn_ref, = rest
    _side_cast_body(cast_in, cast_out, cast_widths)
    tm = x_ref.shape[0]
    f = pl.program_id(1)
    chunks = [slice(r0, r0 + row_chunk) for r0 in range(0, tm, row_chunk)]

    @pl.when(f == 0)
    def _():
        for sl in chunks:
            x = x_ref[sl, :]
            hn_ref[sl, :] = _rms_rows(x, g_ref[...]).astype(BF16)
            o_ref[sl, :] = x

    u = jnp.dot(hn_ref[...], wu_ref[...], preferred_element_type=F32)
    a = jnp.square(jnp.maximum(u, 0.0)).astype(BF16)
    o_ref[...] += jnp.dot(a, wd_ref[...], preferred_element_type=F32)

    if has_final:
        @pl.when(f == pl.num_programs(1) - 1)
        def _():
            for sl in chunks:
                o_ref[sl, :] = _rms_rows(o_ref[sl, :], gf_ref[...])


def _mlp(x, g, wu, wd, g_final=None, *, casts=(), tm=1024, tf=512):
    T, D = x.shape
    Fdim = wu.shape[1]
    tm = _tile(T, tm)
    tf = _tile(Fdim, tf)
    has_final = g_final is not None
    in_specs = [pl.BlockSpec((tm, D), lambda i, f: (i, 0)),
                pl.BlockSpec((1, D), lambda i, f: (0, 0)),
                pl.BlockSpec((D, tf), lambda i, f: (0, f)),
                pl.BlockSpec((tf, D), lambda i, f: (f, 0))]
    args = [x, g.reshape(1, D), wu, wd]
    if has_final:
        in_specs.append(pl.BlockSpec((1, D), lambda i, f: (0, 0)))
        args.append(g_final.reshape(1, D))
    grid = (T // tm, Fdim // tf)
    c_in, c_args, c_out, c_shapes, c_widths = _side_cast_plan(casts, grid)
    return pl.pallas_call(
        functools.partial(_mlp_kernel, has_final=has_final, row_chunk=min(256, tm),
                          cast_widths=c_widths),
        grid=grid,
        in_specs=in_specs + c_in,
        out_specs=[pl.BlockSpec((tm, D), lambda i, f: (i, 0))] + c_out,
        out_shape=[jax.ShapeDtypeStruct((T, D), F32)] + c_shapes,
        scratch_shapes=[pltpu.VMEM((tm, D), BF16)],
        compiler_params=_params(("parallel", "arbitrary")),
        name="mlp_final" if has_final else "mlp",
    )(*args, *c_args)


def _retention_kernel(q_ref, k_ref, v_ref, sg_ref, cd_ref, gn_ref, *rest, chunk, nchunks,
                      cast_widths):
    rest = list(rest)
    cast_in = [rest.pop(0) for _ in range(_n_cast_in(cast_widths))]
    o_ref = rest.pop(0)
    cast_out = [rest.pop(0) for _ in cast_widths]
    state_ref, = rest
    _side_cast_body(cast_in, cast_out, cast_widths)

    @pl.when(pl.program_id(2) == 0)
    def _():
        state_ref[...] = jnp.zeros_like(state_ref)

    causal = (lax.broadcasted_iota(jnp.int32, (chunk, chunk), 0)
              >= lax.broadcasted_iota(jnp.int32, (chunk, chunk), 1))
    for ci in range(nchunks):
        sl = slice(ci * chunk, (ci + 1) * chunk)
        q = q_ref[sl, :]
        k = k_ref[sl, :]
        v = v_ref[sl, :]
        state = state_ref[...]

        inner = lax.dot_general(q, k, _NT, preferred_element_type=F32)
        inner = jnp.where(causal, inner, 0.0).astype(BF16)
        o = jnp.dot(inner, v, preferred_element_type=F32)
        o = o + jnp.dot(q, state.astype(BF16), preferred_element_type=F32)
        k_t = k.astype(F32).T.astype(BF16)
        state_ref[...] = cd_ref[...] * (state + jnp.dot(k_t, v, preferred_element_type=F32))

        ms = jnp.mean(o * o, axis=-1, keepdims=True)
        on = o * lax.rsqrt(ms + RMS_EPS) * gn_ref[...]
        o_ref[sl, :] = (sg_ref[sl, :].astype(F32) * on).astype(o_ref.dtype)


def _retention_tables(heads, chunk, seq):
    dk, dv = RET_QK_DIM, RET_V_DIM
    half = dk // 2
    pos = np.arange(seq, dtype=np.float64)
    inv = ROPE_BASE ** (-np.arange(half, dtype=np.float64) / half)
    ang = pos[:, None] * inv[None, :]
    log_gamma = np.log1p(-np.exp2(-5.0 - np.arange(heads, dtype=np.float64)))
    chunk_decay = np.broadcast_to(np.exp(log_gamma * chunk)[:, None, None], (heads, 1, dv))
    return (jnp.asarray(np.cos(ang), dtype=F32), jnp.asarray(np.sin(ang), dtype=F32),
            log_gamma, jnp.asarray(chunk_decay, dtype=F32))


def _retention(proj, gn, cd, *, batch, heads, chunk, casts=(), rows=1024):
    T = proj.shape[0]
    S = T // batch
    dk, dv = RET_QK_DIM, RET_V_DIM
    rows = min(rows, S)
    ns = S // rows
    vblk0 = 2 * heads * dk // dv
    grid = (batch, heads, ns)
    c_in, c_args, c_out, c_shapes, c_widths = _side_cast_plan(casts, grid)
    return pl.pallas_call(
        functools.partial(_retention_kernel, chunk=chunk, nchunks=rows // chunk,
                          cast_widths=c_widths),
        grid=grid,
        in_specs=[
            pl.BlockSpec((rows, dk), lambda b, h, s: (b * ns + s, h)),
            pl.BlockSpec((rows, dk), lambda b, h, s: (b * ns + s, heads + h)),
            pl.BlockSpec((rows, dv), lambda b, h, s: (b * ns + s, vblk0 + h)),
            pl.BlockSpec((rows, dv), lambda b, h, s: (b * ns + s, vblk0 + heads + h)),
            pl.BlockSpec((None, 1, dv), lambda b, h, s: (h, 0, 0)),
            pl.BlockSpec((1, dv), lambda b, h, s: (0, h)),
        ] + c_in,
        out_specs=[pl.BlockSpec((rows, dv), lambda b, h, s: (b * ns + s, h))] + c_out,
        out_shape=[jax.ShapeDtypeStruct((T, heads * dv), BF16)] + c_shapes,
        scratch_shapes=[pltpu.VMEM((dk, dv), F32)],
        compiler_params=_params(("parallel", "parallel", "arbitrary")),
        name="retention",
    )(proj, proj, proj, proj, cd, gn.reshape(1, heads * dv), *c_args)


def kernel(x, fox_norm, fox_wq, fox_wk, fox_wv, fox_wf, fox_bf, fox_wo, ret_norm, ret_wq, ret_wk,
           ret_wv, ret_wg, ret_gn, ret_wo, mlp_norm, mlp_up, mlp_down, final_norm):
    B, S, D = x.shape
    T = B * S
    fox_heads = D // FOX_HEAD_DIM
    ret_heads = D // RET_QK_DIM
    h = x.reshape(T, D)

    w_qkv = _cast_concat([fox_wq, fox_wk, fox_wv], 0)
    cs = jnp.concatenate([jnp.full((D,), LOG2E * FOX_HEAD_DIM ** -0.5, F32),
                          jnp.ones((2 * D,), F32)])
    wf_pad = jnp.pad(fox_wf[0], ((0, 0), (0, LANES - fox_heads))).astype(BF16)
    bf_pad = jnp.pad(fox_bf[0], (0, LANES - fox_heads)).reshape(1, LANES)
    qkv, z = _norm_matmul(h, fox_norm[0], w_qkv, cs, wf_pad)
    cp = _gate_cumsum(z, bf_pad, B, fox_heads)
    attn, wo0, up0, down0, w_proj, wo1, up1, down1 = _fox_attention(
        qkv, cp, batch=B, heads=fox_heads,
        casts=[([fox_wo], 0), ([mlp_up], 0), ([mlp_down], 0),
               ([ret_wq, ret_wk, ret_wv, ret_wg], 0), ([ret_wo], 0),
               ([mlp_up], 1), ([mlp_down], 1)])
    h = _matmul_res(attn, wo0, h)
    h, = _mlp(h, mlp_norm[0], up0, down0)

    chunk = min(RET_CHUNK, S)
    cos, sin, log_gamma, cd = _retention_tables(ret_heads, chunk, S)
    rv = ret_heads * RET_V_DIM
    cs = jnp.concatenate([jnp.ones((D,), F32), jnp.full((D,), RET_QK_DIM ** -0.5, F32),
                          jnp.ones((2 * rv,), F32)])
    lg_cols = np.repeat(log_gamma, RET_QK_DIM)
    lgcol = jnp.asarray(np.concatenate([lg_cols, -lg_cols, np.zeros(2 * rv)]), dtype=F32)
    proj, = _norm_matmul(h, ret_norm[0], w_proj, cs, ret=(cos, sin, lgcol, 2 * D, rv, chunk))
    y, = _retention(proj, ret_gn[0], cd, batch=B, heads=ret_heads, chunk=chunk)
    h = _matmul_res(y, wo1, h)
    h, = _mlp(h, mlp_norm[1], up1, down1, final_norm)
    return h.reshape(B, S, D)
```

```python
import functools
import math

import numpy as np
import jax
import jax.numpy as jnp
from jax import lax
from jax.experimental import pallas as pl
from jax.experimental.pallas import tpu as pltpu

F32 = jnp.float32
BF16 = jnp.bfloat16

RMS_EPS = 1e-6
ROPE_BASE = 10000.0
FOX_HEAD_DIM = 128
RET_QK_DIM = 256
RET_V_DIM = 512
RET_CHUNK = 256
LANES = 128
BF16_ROWS = 16
NEG_BIG = -1e30
LOG2E = math.log2(math.e)
V7X_VMEM_LIMIT = 56 * 1024 * 1024

_NT = (((1,), (1,)), ((), ()))


def _params(sem, vmem=V7X_VMEM_LIMIT):
    return pltpu.CompilerParams(dimension_semantics=sem, vmem_limit_bytes=vmem)


def _tile(n, target, align=LANES):
    if n <= target:
        return n
    t = (target // align) * align
    while n % t:
        t -= align
    return t


def _rms_rows(x, g):
    ms = jnp.mean(x * x, axis=-1, keepdims=True)
    return x * lax.rsqrt(ms + RMS_EPS) * g


def _cast_concat_kernel(*refs, starts, nblks):
    o_ref = refs[-1]
    p = pl.program_id(0)
    for w_ref, s0, nb in zip(refs[:-1], starts, nblks):
        @pl.when((p >= s0) & (p < s0 + nb))
        def _(w_ref=w_ref):
            o_ref[...] = w_ref[...].astype(o_ref.dtype)


def _cast_concat(ws, layer, *, bw=2048, tr=512):
    R = ws[0].shape[1]
    bw = min(bw, min(w.shape[2] for w in ws))
    tr = _tile(R, tr)
    nr = R // tr
    nblks = [w.shape[2] // bw for w in ws]
    starts = [sum(nblks[:i]) for i in range(len(ws))]

    def in_map(s0, nb):
        def index(p, r):
            local = p - s0
            row = jnp.where(local < 0, 0, jnp.where(local >= nb, nr - 1, r))
            return layer, row, jnp.clip(local, 0, nb - 1)
        return index

    return pl.pallas_call(
        functools.partial(_cast_concat_kernel, starts=starts, nblks=nblks),
        grid=(sum(nblks), nr),
        in_specs=[pl.BlockSpec((None, tr, bw), in_map(s0, nb)) for s0, nb in zip(starts, nblks)],
        out_specs=pl.BlockSpec((tr, bw), lambda p, r: (r, p)),
        out_shape=jax.ShapeDtypeStruct((R, sum(nblks) * bw), BF16),
        compiler_params=_params(("arbitrary", "arbitrary")),
        name="cast_concat",
    )(*ws)


def _side_cast_plan(groups, grid):
    nsteps = math.prod(grid)

    def lin(*ids):
        l = ids[0]
        for n, i in zip(grid[1:], ids[1:]):
            l = l * n + i
        return l

    in_specs, args, out_specs, out_shapes, widths = [], [], [], [], []
    for ws, layer in groups:
        R = ws[0].shape[1]
        rows = BF16_ROWS
        while R // rows > nsteps:
            rows *= 2
        assert R % rows == 0
        nblk = R // rows
        rep = nsteps // nblk

        def blk(*ids, rep=rep, nblk=nblk):
            return jnp.minimum(lin(*ids) // rep, nblk - 1)

        for w in ws:
            in_specs.append(pl.BlockSpec((None, rows, w.shape[2]),
                                         lambda *ids, blk=blk, layer=layer: (layer, blk(*ids), 0)))
            args.append(w)
        cols = sum(w.shape[2] for w in ws)
        out_specs.append(pl.BlockSpec((rows, cols), lambda *ids, blk=blk: (blk(*ids), 0)))
        out_shapes.append(jax.ShapeDtypeStruct((R, cols), BF16))
        widths.append(tuple(w.shape[2] for w in ws))
    return in_specs, args, out_specs, out_shapes, tuple(widths)


def _side_cast_body(in_refs, out_refs, widths):
    k = 0
    for o_ref, ws in zip(out_refs, widths):
        c0 = 0
        for wd in ws:
            o_ref[:, c0:c0 + wd] = in_refs[k][...].astype(BF16)
            k += 1
            c0 += wd


def _n_cast_in(widths):
    return sum(len(ws) for ws in widths)


def _norm_matmul_kernel(x_ref, g_ref, w_ref, cs_ref, *rest, has_gate, ret, row_chunk, cast_widths):
    rest = list(rest)
    wz_ref = rest.pop(0) if has_gate else None
    if ret is not None:
        cos_ref, sin_ref, lg_ref = rest.pop(0), rest.pop(0), rest.pop(0)
    cast_in = [rest.pop(0) for _ in range(_n_cast_in(cast_widths))]
    o_ref = rest.pop(0)
    z_ref = rest.pop(0) if has_gate else None
    cast_out = [rest.pop(0) for _ in cast_widths]
    hn_ref, = rest
    tm = x_ref.shape[0]
    _side_cast_body(cast_in, cast_out, cast_widths)

    @pl.when(pl.program_id(1) == 0)
    def _():
        for r0 in range(0, tm, row_chunk):
            y = _rms_rows(x_ref[r0:r0 + row_chunk, :], g_ref[...])
            hn_ref[r0:r0 + row_chunk, :] = y.astype(BF16)
        if has_gate:
            z_ref[...] = jnp.dot(hn_ref[...], wz_ref[...], preferred_element_type=F32)

    tn = w_ref.shape[1]

    def heads(epilogue):
        for c0 in range(0, tn, RET_QK_DIM):
            acc = jnp.dot(hn_ref[...], w_ref[:, c0:c0 + RET_QK_DIM], preferred_element_type=F32)
            epilogue(c0, acc)

    def plain(c0, acc):
        cols = slice(c0, c0 + RET_QK_DIM)
        o_ref[:, cols] = (acc * cs_ref[:, cols]).astype(o_ref.dtype)

    if ret is None:
        heads(plain)
        return

    n_rot, n_plain, chunk = ret
    j = pl.program_id(1)
    half = RET_QK_DIM // 2

    @pl.when(j < n_rot)
    def _():
        t = (lax.broadcasted_iota(jnp.int32, (tm, half), 0) % chunk + 1).astype(F32)
        cos = cos_ref[...]
        sin = sin_ref[...]

        def rotary(c0, acc):
            h1 = slice(c0, c0 + half)
            h2 = slice(c0 + half, c0 + 2 * half)
            x1 = acc[:, :half]
            x2 = acc[:, half:]
            scale = jnp.exp(t * lg_ref[:, h1]) * cs_ref[:, h1]
            o_ref[:, h1] = ((x1 * cos - x2 * sin) * scale).astype(o_ref.dtype)
            o_ref[:, h2] = ((x1 * sin + x2 * cos) * scale).astype(o_ref.dtype)
        heads(rotary)

    @pl.when((j >= n_rot) & (j < n_rot + n_plain))
    def _():
        heads(plain)

    @pl.when(j >= n_rot + n_plain)
    def _():
        def swish(c0, acc):
            cols = slice(c0, c0 + RET_QK_DIM)
            y = acc * cs_ref[:, cols]
            sig = 0.5 * jnp.tanh(0.5 * y) + 0.5
            o_ref[:, cols] = (y * sig).astype(o_ref.dtype)
        heads(swish)


def _norm_matmul(x, g, w, colscale, wz=None, *, ret=None, casts=(), tm=1024, tn=2048):
    T, D = x.shape
    N = w.shape[1]
    tm = _tile(T, tm)
    tn = _tile(N if ret is None else math.gcd(ret[3], ret[4]), tn)
    has_gate = wz is not None
    grid = (T // tm, N // tn)
    c_in, c_args, c_out, c_shapes, c_widths = _side_cast_plan(casts, grid)
    in_specs = [
        pl.BlockSpec((tm, D), lambda i, j: (i, 0)),
        pl.BlockSpec((1, D), lambda i, j: (0, 0)),
        pl.BlockSpec((D, tn), lambda i, j: (0, j)),
        pl.BlockSpec((1, tn), lambda i, j: (0, j)),
    ]
    args = [x, g.reshape(1, D), w, colscale.reshape(1, N)]
    out_shape = [jax.ShapeDtypeStruct((T, N), BF16)]
    out_specs = [pl.BlockSpec((tm, tn), lambda i, j: (i, j))]
    if has_gate:
        in_specs.append(pl.BlockSpec((D, LANES), lambda i, j: (0, 0)))
        args.append(wz)
        out_shape.append(jax.ShapeDtypeStruct((T, LANES), F32))
        out_specs.append(pl.BlockSpec((tm, LANES), lambda i, j: (i, 0)))
    ret_static = None
    if ret is not None:
        cos, sin, lgcol, rot_cols, plain_cols, chunk = ret
        assert rot_cols % tn == 0 and plain_cols % tn == 0 and tm % chunk == 0
        nrt = cos.shape[0] // tm
        half = RET_QK_DIM // 2
        in_specs += [pl.BlockSpec((tm, half), lambda i, j: (i % nrt, 0)),
                     pl.BlockSpec((tm, half), lambda i, j: (i % nrt, 0)),
                     pl.BlockSpec((1, tn), lambda i, j: (0, j))]
        args += [cos, sin, lgcol.reshape(1, N)]
        ret_static = (rot_cols // tn, plain_cols // tn, chunk)
    res = pl.pallas_call(
        functools.partial(_norm_matmul_kernel, has_gate=has_gate, ret=ret_static,
                          row_chunk=min(256, tm), cast_widths=c_widths),
        grid=grid,
        in_specs=in_specs + c_in,
        out_specs=out_specs + c_out,
        out_shape=out_shape + c_shapes,
        scratch_shapes=[pltpu.VMEM((tm, D), BF16)],
        compiler_params=_params(("parallel", "arbitrary")),
        name="norm_matmul_gate" if has_gate else "norm_matmul",
    )(*args, *c_args)
    return res


def _gate_cumsum_kernel(z_ref, b_ref, cp_ref, *, heads):
    z = z_ref[...] + b_ref[...]
    c = jnp.minimum(z, 0.0) - jnp.log1p(jnp.exp(-jnp.abs(z)))
    S = c.shape[0]
    row = lax.broadcasted_iota(jnp.int32, c.shape, 0)
    shift = 1
    while shift < S:
        c = c + jnp.where(row >= shift, pltpu.roll(c, shift, axis=0), 0.0)
        shift *= 2
    c = c * LOG2E
    hi = c.astype(BF16).astype(F32)
    r1 = c - hi
    mid = r1.astype(BF16).astype(F32)
    lo = r1 - mid
    lane = lax.broadcasted_iota(jnp.int32, c.shape, 1)
    packed = jnp.where(lane < heads, hi,
                       jnp.where(lane < 2 * heads, pltpu.roll(mid, heads, axis=1),
                                 jnp.where(lane < 3 * heads, pltpu.roll(lo, 2 * heads, axis=1), 0.0)))
    cp_ref[...] = packed.astype(BF16)


def _gate_cumsum(z, b_pad, batch, heads):
    T = z.shape[0]
    S = T // batch
    assert 3 * heads <= LANES
    return pl.pallas_call(
        functools.partial(_gate_cumsum_kernel, heads=heads),
        grid=(batch,),
        in_specs=[pl.BlockSpec((S, LANES), lambda b: (b, 0)),
                  pl.BlockSpec((1, LANES), lambda b: (0, 0))],
        out_specs=pl.BlockSpec((S, LANES), lambda b: (b, 0)),
        out_shape=jax.ShapeDtypeStruct((T, LANES), BF16),
        compiler_params=_params(("parallel",)),
        name="gate_cumsum",
    )(z, b_pad)


ONES_ROWS = 16
N_PIECES = 3


def _piece_selector(head0, hpg, heads, sign, lane0):
    r = lax.broadcasted_iota(jnp.int32, (LANES, hpg * LANES), 0)
    col = lax.broadcasted_iota(jnp.int32, (LANES, hpg * LANES), 1)
    g = col // LANES
    p = col % LANES - lane0
    hit = (p >= 0) & (p < N_PIECES) & (r == head0 + g + heads * p)
    return jnp.where(hit, sign, 0.0).astype(BF16)


def _fox_attn_kernel(q_ref, k_ref, v_ref, cp_ref, *rest, tq, tk, hpg, heads, cast_widths):
    rest = list(rest)
    cast_in = [rest.pop(0) for _ in range(_n_cast_in(cast_widths))]
    o_ref = rest.pop(0)
    cast_out = [rest.pop(0) for _ in cast_widths]
    kx_ref, vt_ref, acc_ref, st_ref, p_ref = rest
    _side_cast_body(cast_in, cast_out, cast_widths)
    hg = pl.program_id(1)
    qi = pl.program_id(2)
    S = k_ref.shape[0]
    dh = FOX_HEAD_DIM
    lane = lax.broadcasted_iota(jnp.int32, (1, hpg * LANES), 1) % LANES
    cols = [slice(g * dh, (g + 1) * dh) for g in range(hpg)]

    @pl.when(qi == 0)
    def _():
        sel = _piece_selector(hg * hpg, hpg, heads, -1.0, 0)
        ones = jnp.where((lane >= N_PIECES) & (lane < 2 * N_PIECES), 1.0, 0.0)
        for g in range(hpg):
            vt_ref[g, dh:, :] = jnp.ones((ONES_ROWS, S), BF16)

        def body(i, carry):
            sl = pl.ds(pl.multiple_of(i * tq, tq), tq)
            ek = (jnp.dot(cp_ref[sl, :], sel, preferred_element_type=F32) + ones).astype(BF16)
            for g in range(hpg):
                kx_ref[g, sl, :dh] = k_ref[sl, cols[g]]
                kx_ref[g, sl, dh:] = ek[:, cols[g]]
                vt_ref[g, :dh, sl] = v_ref[sl, cols[g]].astype(F32).T.astype(BF16)
            return carry
        lax.fori_loop(0, S // tq, body, 0)

    q0 = pl.multiple_of(qi * tq, tq)
    sel_q = _piece_selector(hg * hpg, hpg, heads, 1.0, N_PIECES)
    cref = jnp.dot(cp_ref[pl.ds(q0, ONES_ROWS), :], sel_q, preferred_element_type=F32)[0:1, :]
    eq = (cref + jnp.where(lane < N_PIECES, 1.0, 0.0)).astype(BF16)
    qxt = [jnp.concatenate([q_ref[:, cols[g]].astype(F32),
                            jnp.broadcast_to(eq[:, cols[g]].astype(F32), (tq, dh))], axis=1
                           ).T.astype(BF16) for g in range(hpg)]

    def scores(g, j, diag):
        k0 = pl.multiple_of(j * tk, tk)
        st = jnp.dot(kx_ref[g, pl.ds(k0, tk), :], qxt[g], preferred_element_type=F32)
        if diag is not None:
            kk = lax.broadcasted_iota(jnp.int32, st.shape, 0) + diag * tk
            qq = lax.broadcasted_iota(jnp.int32, st.shape, 1)
            st = jnp.where(kk <= qq, st, NEG_BIG)
        return st, jnp.max(st, axis=0, keepdims=True)

    def pv_update(g, j, slot, alpha):
        k0 = pl.multiple_of(j * tk, tk)
        pv = jnp.dot(vt_ref[g, :, pl.ds(k0, tk)], p_ref[g, slot], preferred_element_type=F32)
        acc_ref[g] = alpha * acc_ref[g] + pv

    def stage(g, slot, carry, prev_blk, next_blk, next_diag=None):
        m, alpha_prev, cmax = carry
        if prev_blk is not None:
            pv_update(g, prev_blk, 1 - slot, alpha_prev)
        st_next, cmax_next = scores(g, next_blk, next_diag)
        st_ref[g, 1 - slot] = st_next
        m_new = jnp.maximum(m, cmax)
        p_ref[g, slot] = jnp.exp2(st_ref[g, slot] - m_new).astype(BF16)
        return m_new, jnp.exp2(m - m_new), cmax_next

    assert tq == 2 * tk
    d0 = 2 * qi
    carries = []
    for g in range(hpg):
        acc_ref[g] = jnp.zeros(acc_ref.shape[1:], F32)
        st, cmax = scores(g, d0, 0)
        st_ref[g, 0] = st
        carries.append((jnp.full((1, tq), NEG_BIG, F32), jnp.ones((1, tq), F32), cmax))
    carries = [stage(g, 0, carries[g], None, d0 + 1, 1) for g in range(hpg)]
    carries = [stage(g, 1, carries[g], d0, 0) for g in range(hpg)]

    def pair(i, carries):
        out = []
        for g in range(hpg):
            c = stage(g, 0, carries[g], jnp.where(i == 0, d0 + 1, 2 * i - 1), 2 * i + 1)
            out.append(stage(g, 1, c, 2 * i, 2 * i + 2))
        return tuple(out)

    carries = lax.fori_loop(0, qi, pair, tuple(carries))
    last_blk = jnp.where(qi == 0, d0 + 1, d0 - 1)
    for g in range(hpg):
        pv_update(g, last_blk, 1, carries[g][1])
    for g in range(hpg):
        ot = acc_ref[g, :dh, :] / acc_ref[g, dh:dh + 1, :]
        o_ref[:, cols[g]] = ot.T.astype(o_ref.dtype)


def _fox_attention(qkv, cp, *, batch, heads, casts=(), tq=512, hpg=4):
    T = qkv.shape[0]
    S = T // batch
    dh = FOX_HEAD_DIM
    tq = min(tq, S)
    tk = tq // 2
    nq = S // tq
    ng = heads // hpg
    w = hpg * dh
    grid = (batch, ng, nq)
    c_in, c_args, c_out, c_shapes, c_widths = _side_cast_plan(casts, grid)
    return pl.pallas_call(
        functools.partial(_fox_attn_kernel, tq=tq, tk=tk, hpg=hpg, heads=heads,
                          cast_widths=c_widths),
        grid=grid,
        in_specs=[
            pl.BlockSpec((tq, w), lambda b, h, i: (b * nq + i, h)),
            pl.BlockSpec((S, w), lambda b, h, i: (b, ng + h)),
            pl.BlockSpec((S, w), lambda b, h, i: (b, 2 * ng + h)),
            pl.BlockSpec((S, LANES), lambda b, h, i: (b, 0)),
        ] + c_in,
        out_specs=[pl.BlockSpec((tq, w), lambda b, h, i: (b * nq + i, h))] + c_out,
        out_shape=[jax.ShapeDtypeStruct((T, heads * dh), BF16)] + c_shapes,
        scratch_shapes=[pltpu.VMEM((hpg, S, 2 * dh), BF16),
                        pltpu.VMEM((hpg, dh + ONES_ROWS, S), BF16),
                        pltpu.VMEM((hpg, dh + ONES_ROWS, tq), F32),
                        pltpu.VMEM((hpg, 2, tk, tq), F32),
                        pltpu.VMEM((hpg, 2, tk, tq), BF16)],
        compiler_params=_params(("parallel", "parallel", "arbitrary")),
        name="fox_attention",
    )(qkv, qkv, qkv, cp, *c_args)


MXU_COLS = 256


def _matmul_res_kernel(a_ref, w_ref, r_ref, o_ref):
    for c0 in range(0, w_ref.shape[1], MXU_COLS):
        cols = slice(c0, c0 + MXU_COLS)
        o_ref[:, cols] = r_ref[:, cols] + jnp.dot(a_ref[...], w_ref[:, cols],
                                                  preferred_element_type=F32)


def _matmul_res(a, w, res, *, tm=512):
    T, K = a.shape
    N = w.shape[1]
    tm = _tile(T, tm)
    return pl.pallas_call(
        _matmul_res_kernel,
        grid=(T // tm,),
        in_specs=[pl.BlockSpec((tm, K), lambda i: (i, 0)),
                  pl.BlockSpec((K, N), lambda i: (0, 0), pipeline_mode=pl.Buffered(1)),
                  pl.BlockSpec((tm, N), lambda i: (i, 0))],
        out_specs=pl.BlockSpec((tm, N), lambda i: (i, 0)),
        out_shape=jax.ShapeDtypeStruct((T, N), F32),
        compiler_params=_params(("parallel",)),
        name="matmul_res",
    )(a, w, res)


def _mlp_kernel(x_ref, g_ref, wu_ref, wd_ref, *rest, has_final, row_chunk, cast_widths):
    rest = list(rest)
    gf_ref = rest.pop(0) if has_final else None
    cast_in = [rest.pop(0) for _ in range(_n_cast_in(cast_widths))]
    o_ref = rest.pop(0)
    cast_out = [rest.pop(0) for _ in cast_widths]
    hn_ref, = rest
    _side_cast_body(cast_in, cast_out, cast_widths)
    tm = x_ref.shape[0]
    f = pl.program_id(1)
    chunks = [slice(r0, r0 + row_chunk) for r0 in range(0, tm, row_chunk)]

    @pl.when(f == 0)
    def _():
        for sl in chunks:
            x = x_ref[sl, :]
            hn_ref[sl, :] = _rms_rows(x, g_ref[...]).astype(BF16)
            o_ref[sl, :] = x

    u = jnp.dot(hn_ref[...], wu_ref[...], preferred_element_type=F32)
    a = jnp.square(jnp.maximum(u, 0.0)).astype(BF16)
    o_ref[...] += jnp.dot(a, wd_ref[...], preferred_element_type=F32)

    if has_final:
        @pl.when(f == pl.num_programs(1) - 1)
        def _():
            for sl in chunks:
                o_ref[sl, :] = _rms_rows(o_ref[sl, :], gf_ref[...])


def _mlp(x, g, wu, wd, g_final=None, *, casts=(), tm=1024, tf=512):
    T, D = x.shape
    Fdim = wu.shape[1]
    tm = _tile(T, tm)
    tf = _tile(Fdim, tf)
    has_final = g_final is not None
    in_specs = [pl.BlockSpec((tm, D), lambda i, f: (i, 0)),
                pl.BlockSpec((1, D), lambda i, f: (0, 0)),
                pl.BlockSpec((D, tf), lambda i, f: (0, f)),
                pl.BlockSpec((tf, D), lambda i, f: (f, 0))]
    args = [x, g.reshape(1, D), wu, wd]
    if has_final:
        in_specs.append(pl.BlockSpec((1, D), lambda i, f: (0, 0)))
        args.append(g_final.reshape(1, D))
    grid = (T // tm, Fdim // tf)
    c_in, c_args, c_out, c_shapes, c_widths = _side_cast_plan(casts, grid)
    return pl.pallas_call(
        functools.partial(_mlp_kernel, has_final=has_final, row_chunk=min(256, tm),
                          cast_widths=c_widths),
        grid=grid,
        in_specs=in_specs + c_in,
        out_specs=[pl.BlockSpec((tm, D), lambda i, f: (i, 0))] + c_out,
        out_shape=[jax.ShapeDtypeStruct((T, D), F32)] + c_shapes,
        scratch_shapes=[pltpu.VMEM((tm, D), BF16)],
        compiler_params=_params(("parallel", "arbitrary")),
        name="mlp_final" if has_final else "mlp",
    )(*args, *c_args)


def _retention_kernel(q_ref, k_ref, v_ref, sg_ref, cd_ref, gn_ref, *rest, chunk, nchunks,
                      cast_widths):
    rest = list(rest)
    cast_in = [rest.pop(0) for _ in range(_n_cast_in(cast_widths))]
    o_ref = rest.pop(0)
    cast_out = [rest.pop(0) for _ in cast_widths]
    state_ref, = rest
    _side_cast_body(cast_in, cast_out, cast_widths)

    @pl.when(pl.program_id(2) == 0)
    def _():
        state_ref[...] = jnp.zeros_like(state_ref)

    causal = (lax.broadcasted_iota(jnp.int32, (chunk, chunk), 0)
              >= lax.broadcasted_iota(jnp.int32, (chunk, chunk), 1))
    slices = [slice(ci * chunk, (ci + 1) * chunk) for ci in range(nchunks)]
    intra, kv = [], []
    for sl in slices:
        q = q_ref[sl, :]
        k = k_ref[sl, :]
        v = v_ref[sl, :]
        inner = lax.dot_general(q, k, _NT, preferred_element_type=F32)
        inner = jnp.where(causal, inner, 0.0).astype(BF16)
        intra.append(jnp.dot(inner, v, preferred_element_type=F32))
        k_t = k.astype(F32).T.astype(BF16)
        kv.append(jnp.dot(k_t, v, preferred_element_type=F32))

    for ci, sl in enumerate(slices):
        state = state_ref[...]
        o = intra[ci] + jnp.dot(q_ref[sl, :], state.astype(BF16), preferred_element_type=F32)
        state_ref[...] = cd_ref[...] * (state + kv[ci])

        ms = jnp.mean(o * o, axis=-1, keepdims=True)
        on = o * lax.rsqrt(ms + RMS_EPS) * gn_ref[...]
        o_ref[sl, :] = (sg_ref[sl, :].astype(F32) * on).astype(o_ref.dtype)


def _retention_tables(heads, chunk, seq):
    dk, dv = RET_QK_DIM, RET_V_DIM
    half = dk // 2
    pos = np.arange(seq, dtype=np.float64)
    inv = ROPE_BASE ** (-np.arange(half, dtype=np.float64) / half)
    ang = pos[:, None] * inv[None, :]
    log_gamma = np.log1p(-np.exp2(-5.0 - np.arange(heads, dtype=np.float64)))
    chunk_decay = np.broadcast_to(np.exp(log_gamma * chunk)[:, None, None], (heads, 1, dv))
    return (jnp.asarray(np.cos(ang), dtype=F32), jnp.asarray(np.sin(ang), dtype=F32),
            log_gamma, jnp.asarray(chunk_decay, dtype=F32))


def _retention(proj, gn, cd, *, batch, heads, chunk, casts=(), rows=1024):
    T = proj.shape[0]
    S = T // batch
    dk, dv = RET_QK_DIM, RET_V_DIM
    rows = min(rows, S)
    ns = S // rows
    vblk0 = 2 * heads * dk // dv
    grid = (batch, heads, ns)
    c_in, c_args, c_out, c_shapes, c_widths = _side_cast_plan(casts, grid)
    return pl.pallas_call(
        functools.partial(_retention_kernel, chunk=chunk, nchunks=rows // chunk,
                          cast_widths=c_widths),
        grid=grid,
        in_specs=[
            pl.BlockSpec((rows, dk), lambda b, h, s: (b * ns + s, h)),
            pl.BlockSpec((rows, dk), lambda b, h, s: (b * ns + s, heads + h)),
            pl.BlockSpec((rows, dv), lambda b, h, s: (b * ns + s, vblk0 + h)),
            pl.BlockSpec((rows, dv), lambda b, h, s: (b * ns + s, vblk0 + heads + h)),
            pl.BlockSpec((None, 1, dv), lambda b, h, s: (h, 0, 0)),
            pl.BlockSpec((1, dv), lambda b, h, s: (0, h)),
        ] + c_in,
        out_specs=[pl.BlockSpec((rows, dv), lambda b, h, s: (b * ns + s, h))] + c_out,
        out_shape=[jax.ShapeDtypeStruct((T, heads * dv), BF16)] + c_shapes,
        scratch_shapes=[pltpu.VMEM((dk, dv), F32)],
        compiler_params=_params(("parallel", "parallel", "arbitrary")),
        name="retention",
    )(proj, proj, proj, proj, cd, gn.reshape(1, heads * dv), *c_args)


def kernel(x, fox_norm, fox_wq, fox_wk, fox_wv, fox_wf, fox_bf, fox_wo, ret_norm, ret_wq, ret_wk,
           ret_wv, ret_wg, ret_gn, ret_wo, mlp_norm, mlp_up, mlp_down, final_norm):
    B, S, D = x.shape
    T = B * S
    fox_heads = D // FOX_HEAD_DIM
    ret_heads = D // RET_QK_DIM
    h = x.reshape(T, D)

    w_qkv = _cast_concat([fox_wq, fox_wk, fox_wv], 0)
    cs = jnp.concatenate([jnp.full((D,), LOG2E * FOX_HEAD_DIM ** -0.5, F32),
                          jnp.ones((2 * D,), F32)])
    wf_pad = jnp.pad(fox_wf[0], ((0, 0), (0, LANES - fox_heads))).astype(BF16)
    bf_pad = jnp.pad(fox_bf[0], (0, LANES - fox_heads)).reshape(1, LANES)
    qkv, z = _norm_matmul(h, fox_norm[0], w_qkv, cs, wf_pad)
    cp = _gate_cumsum(z, bf_pad, B, fox_heads)
    attn, wo0, up0, down0, w_proj, wo1, up1, down1 = _fox_attention(
        qkv, cp, batch=B, heads=fox_heads,
        casts=[([fox_wo], 0), ([mlp_up], 0), ([mlp_down], 0),
               ([ret_wq, ret_wk, ret_wv, ret_wg], 0), ([ret_wo], 0),
               ([mlp_up], 1), ([mlp_down], 1)])
    h = _matmul_res(attn, wo0, h)
    h, = _mlp(h, mlp_norm[0], up0, down0)

    chunk = min(RET_CHUNK, S)
    cos, sin, log_gamma, cd = _retention_tables(ret_heads, chunk, S)
    rv = ret_heads * RET_V_DIM
    cs = jnp.concatenate([jnp.ones((D,), F32), jnp.full((D,), RET_QK_DIM ** -0.5, F32),
                          jnp.ones((2 * rv,), F32)])
    lg_cols = np.repeat(log_gamma, RET_QK_DIM)
    lgcol = jnp.asarray(np.concatenate([lg_cols, -lg_cols, np.zeros(2 * rv)]), dtype=F32)
    proj, = _norm_matmul(h, ret_norm[0], w_proj, cs, ret=(cos, sin, lgcol, 2 * D, rv, chunk))
    y, = _retention(proj, ret_gn[0], cd, batch=B, heads=ret_heads, chunk=chunk)
    h = _matmul_res(y, wo1, h)
    h, = _mlp(h, mlp_norm[1], up1, down1, final_norm)
    return h.reshape(B, S, D)
```

```python
import functools
import math

import numpy as np
import jax
import jax.numpy as jnp
from jax import lax
from jax.experimental import pallas as pl
from jax.experimental.pallas import tpu as pltpu

F32 = jnp.float32
BF16 = jnp.bfloat16

RMS_EPS = 1e-6
ROPE_BASE = 10000.0
FOX_HEAD_DIM = 128
RET_QK_DIM = 256
RET_V_DIM = 512
RET_CHUNK = 256
LANES = 128
BF16_ROWS = 16
NEG_BIG = -1e30
LOG2E = math.log2(math.e)
V7X_VMEM_LIMIT = 56 * 1024 * 1024

_NT = (((1,), (1,)), ((), ()))


def _params(sem, vmem=V7X_VMEM_LIMIT):
    return pltpu.CompilerParams(dimension_semantics=sem, vmem_limit_bytes=vmem)


def _tile(n, target, align=LANES):
    if n <= target:
        return n
    t = (target // align) * align
    while n % t:
        t -= align
    return t


def _rms_rows(x, g):
    ms = jnp.mean(x * x, axis=-1, keepdims=True)
    return x * lax.rsqrt(ms + RMS_EPS) * g


def _cast_concat_kernel(*refs, starts, nblks):
    o_ref = refs[-1]
    p = pl.program_id(0)
    for w_ref, s0, nb in zip(refs[:-1], starts, nblks):
        @pl.when((p >= s0) & (p < s0 + nb))
        def _(w_ref=w_ref):
            o_ref[...] = w_ref[...].astype(o_ref.dtype)


def _cast_concat(ws, layer, *, bw=2048, tr=512):
    R = ws[0].shape[1]
    bw = min(bw, min(w.shape[2] for w in ws))
    tr = _tile(R, tr)
    nr = R // tr
    nblks = [w.shape[2] // bw for w in ws]
    starts = [sum(nblks[:i]) for i in range(len(ws))]

    def in_map(s0, nb):
        def index(p, r):
            local = p - s0
            row = jnp.where(local < 0, 0, jnp.where(local >= nb, nr - 1, r))
            return layer, row, jnp.clip(local, 0, nb - 1)
        return index

    return pl.pallas_call(
        functools.partial(_cast_concat_kernel, starts=starts, nblks=nblks),
        grid=(sum(nblks), nr),
        in_specs=[pl.BlockSpec((None, tr, bw), in_map(s0, nb)) for s0, nb in zip(starts, nblks)],
        out_specs=pl.BlockSpec((tr, bw), lambda p, r: (r, p)),
        out_shape=jax.ShapeDtypeStruct((R, sum(nblks) * bw), BF16),
        compiler_params=_params(("arbitrary", "arbitrary")),
        name="cast_concat",
    )(*ws)


def _side_cast_plan(groups, grid):
    nsteps = math.prod(grid)

    def lin(*ids):
        l = ids[0]
        for n, i in zip(grid[1:], ids[1:]):
            l = l * n + i
        return l

    in_specs, args, out_specs, out_shapes, widths = [], [], [], [], []
    for ws, layer in groups:
        R = ws[0].shape[1]
        rows = BF16_ROWS
        while R // rows > nsteps:
            rows *= 2
        assert R % rows == 0
        nblk = R // rows
        rep = nsteps // nblk

        def blk(*ids, rep=rep, nblk=nblk):
            return jnp.minimum(lin(*ids) // rep, nblk - 1)

        for w in ws:
            in_specs.append(pl.BlockSpec((None, rows, w.shape[2]),
                                         lambda *ids, blk=blk, layer=layer: (layer, blk(*ids), 0)))
            args.append(w)
        cols = sum(w.shape[2] for w in ws)
        out_specs.append(pl.BlockSpec((rows, cols), lambda *ids, blk=blk: (blk(*ids), 0)))
        out_shapes.append(jax.ShapeDtypeStruct((R, cols), BF16))
        widths.append(tuple(w.shape[2] for w in ws))
    return in_specs, args, out_specs, out_shapes, tuple(widths)


def _side_cast_body(in_refs, out_refs, widths):
    k = 0
    for o_ref, ws in zip(out_refs, widths):
        c0 = 0
        for wd in ws:
            o_ref[:, c0:c0 + wd] = in_refs[k][...].astype(BF16)
            k += 1
            c0 += wd


def _n_cast_in(widths):
    return sum(len(ws) for ws in widths)


def _norm_matmul_kernel(x_ref, g_ref, w_ref, cs_ref, *rest, has_gate, ret, row_chunk, cast_widths):
    rest = list(rest)
    wz_ref = rest.pop(0) if has_gate else None
    if ret is not None:
        cos_ref, sin_ref, lg_ref = rest.pop(0), rest.pop(0), rest.pop(0)
    cast_in = [rest.pop(0) for _ in range(_n_cast_in(cast_widths))]
    o_ref = rest.pop(0)
    z_ref = rest.pop(0) if has_gate else None
    cast_out = [rest.pop(0) for _ in cast_widths]
    hn_ref, = rest
    tm = x_ref.shape[0]
    _side_cast_body(cast_in, cast_out, cast_widths)

    @pl.when(pl.program_id(1) == 0)
    def _():
        for r0 in range(0, tm, row_chunk):
            y = _rms_rows(x_ref[r0:r0 + row_chunk, :], g_ref[...])
            hn_ref[r0:r0 + row_chunk, :] = y.astype(BF16)
        if has_gate:
            z_ref[...] = jnp.dot(hn_ref[...], wz_ref[...], preferred_element_type=F32)

    tn = w_ref.shape[1]

    def heads(epilogue):
        for c0 in range(0, tn, RET_QK_DIM):
            acc = jnp.dot(hn_ref[...], w_ref[:, c0:c0 + RET_QK_DIM], preferred_element_type=F32)
            epilogue(c0, acc)

    def plain(c0, acc):
        cols = slice(c0, c0 + RET_QK_DIM)
        o_ref[:, cols] = (acc * cs_ref[:, cols]).astype(o_ref.dtype)

    if ret is None:
        heads(plain)
        return

    n_rot, n_plain, chunk = ret
    j = pl.program_id(1)
    half = RET_QK_DIM // 2

    @pl.when(j < n_rot)
    def _():
        t = (lax.broadcasted_iota(jnp.int32, (tm, half), 0) % chunk + 1).astype(F32)
        cos = cos_ref[...]
        sin = sin_ref[...]

        def rotary(c0, acc):
            h1 = slice(c0, c0 + half)
            h2 = slice(c0 + half, c0 + 2 * half)
            x1 = acc[:, :half]
            x2 = acc[:, half:]
            scale = jnp.exp(t * lg_ref[:, h1]) * cs_ref[:, h1]
            o_ref[:, h1] = ((x1 * cos - x2 * sin) * scale).astype(o_ref.dtype)
            o_ref[:, h2] = ((x1 * sin + x2 * cos) * scale).astype(o_ref.dtype)
        heads(rotary)

    @pl.when((j >= n_rot) & (j < n_rot + n_plain))
    def _():
        heads(plain)

    @pl.when(j >= n_rot + n_plain)
    def _():
        def swish(c0, acc):
            cols = slice(c0, c0 + RET_QK_DIM)
            y = acc * cs_ref[:, cols]
            sig = 0.5 * jnp.tanh(0.5 * y) + 0.5
            o_ref[:, cols] = (y * sig).astype(o_ref.dtype)
        heads(swish)


def _norm_matmul(x, g, w, colscale, wz=None, *, ret=None, casts=(), tm=1024, tn=2048):
    T, D = x.shape
    N = w.shape[1]
    tm = _tile(T, tm)
    tn = _tile(N if ret is None else math.gcd(ret[3], ret[4]), tn)
    has_gate = wz is not None
    grid = (T // tm, N // tn)
    c_in, c_args, c_out, c_shapes, c_widths = _side_cast_plan(casts, grid)
    in_specs = [
        pl.BlockSpec((tm, D), lambda i, j: (i, 0)),
        pl.BlockSpec((1, D), lambda i, j: (0, 0)),
        pl.BlockSpec((D, tn), lambda i, j: (0, j)),
        pl.BlockSpec((1, tn), lambda i, j: (0, j)),
    ]
    args = [x, g.reshape(1, D), w, colscale.reshape(1, N)]
    out_shape = [jax.ShapeDtypeStruct((T, N), BF16)]
    out_specs = [pl.BlockSpec((tm, tn), lambda i, j: (i, j))]
    if has_gate:
        in_specs.append(pl.BlockSpec((D, LANES), lambda i, j: (0, 0)))
        args.append(wz)
        out_shape.append(jax.ShapeDtypeStruct((T, LANES), F32))
        out_specs.append(pl.BlockSpec((tm, LANES), lambda i, j: (i, 0)))
    ret_static = None
    if ret is not None:
        cos, sin, lgcol, rot_cols, plain_cols, chunk = ret
        assert rot_cols % tn == 0 and plain_cols % tn == 0 and tm % chunk == 0
        nrt = cos.shape[0] // tm
        half = RET_QK_DIM // 2
        in_specs += [pl.BlockSpec((tm, half), lambda i, j: (i % nrt, 0)),
                     pl.BlockSpec((tm, half), lambda i, j: (i % nrt, 0)),
                     pl.BlockSpec((1, tn), lambda i, j: (0, j))]
        args += [cos, sin, lgcol.reshape(1, N)]
        ret_static = (rot_cols // tn, plain_cols // tn, chunk)
    res = pl.pallas_call(
        functools.partial(_norm_matmul_kernel, has_gate=has_gate, ret=ret_static,
                          row_chunk=min(256, tm), cast_widths=c_widths),
        grid=grid,
        in_specs=in_specs + c_in,
        out_specs=out_specs + c_out,
        out_shape=out_shape + c_shapes,
        scratch_shapes=[pltpu.VMEM((tm, D), BF16)],
        compiler_params=_params(("parallel", "arbitrary")),
        name="norm_matmul_gate" if has_gate else "norm_matmul",
    )(*args, *c_args)
    return res


def _gate_cumsum_kernel(z_ref, b_ref, cp_ref, *, heads):
    z = z_ref[...] + b_ref[...]
    c = jnp.minimum(z, 0.0) - jnp.log1p(jnp.exp(-jnp.abs(z)))
    S = c.shape[0]
    row = lax.broadcasted_iota(jnp.int32, c.shape, 0)
    shift = 1
    while shift < S:
        c = c + jnp.where(row >= shift, pltpu.roll(c, shift, axis=0), 0.0)
        shift *= 2
    c = c * LOG2E
    hi = c.astype(BF16).astype(F32)
    r1 = c - hi
    mid = r1.astype(BF16).astype(F32)
    lo = r1 - mid
    lane = lax.broadcasted_iota(jnp.int32, c.shape, 1)
    packed = jnp.where(lane < heads, hi,
                       jnp.where(lane < 2 * heads, pltpu.roll(mid, heads, axis=1),
                                 jnp.where(lane < 3 * heads, pltpu.roll(lo, 2 * heads, axis=1), 0.0)))
    cp_ref[...] = packed.astype(BF16)


def _gate_cumsum(z, b_pad, batch, heads):
    T = z.shape[0]
    S = T // batch
    assert 3 * heads <= LANES
    return pl.pallas_call(
        functools.partial(_gate_cumsum_kernel, heads=heads),
        grid=(batch,),
        in_specs=[pl.BlockSpec((S, LANES), lambda b: (b, 0)),
                  pl.BlockSpec((1, LANES), lambda b: (0, 0))],
        out_specs=pl.BlockSpec((S, LANES), lambda b: (b, 0)),
        out_shape=jax.ShapeDtypeStruct((T, LANES), BF16),
        compiler_params=_params(("parallel",)),
        name="gate_cumsum",
    )(z, b_pad)


ONES_ROWS = 16
N_PIECES = 3


def _piece_selector(head0, hpg, heads, sign, lane0):
    r = lax.broadcasted_iota(jnp.int32, (LANES, hpg * LANES), 0)
    col = lax.broadcasted_iota(jnp.int32, (LANES, hpg * LANES), 1)
    g = col // LANES
    p = col % LANES - lane0
    hit = (p >= 0) & (p < N_PIECES) & (r == head0 + g + heads * p)
    return jnp.where(hit, sign, 0.0).astype(BF16)


def _fox_attn_kernel(q_ref, k_ref, v_ref, cp_ref, *rest, tq, tk, hpg, heads, cast_widths):
    rest = list(rest)
    cast_in = [rest.pop(0) for _ in range(_n_cast_in(cast_widths))]
    o_ref = rest.pop(0)
    cast_out = [rest.pop(0) for _ in cast_widths]
    kx_ref, vt_ref, acc_ref, st_ref, p_ref = rest
    _side_cast_body(cast_in, cast_out, cast_widths)
    hg = pl.program_id(1)
    qi = pl.program_id(2)
    S = k_ref.shape[0]
    dh = FOX_HEAD_DIM
    lane = lax.broadcasted_iota(jnp.int32, (1, hpg * LANES), 1) % LANES
    cols = [slice(g * dh, (g + 1) * dh) for g in range(hpg)]

    @pl.when(qi == 0)
    def _():
        sel = _piece_selector(hg * hpg, hpg, heads, -1.0, 0)
        ones = jnp.where((lane >= N_PIECES) & (lane < 2 * N_PIECES), 1.0, 0.0)
        for g in range(hpg):
            vt_ref[g, dh:, :] = jnp.ones((ONES_ROWS, S), BF16)

        def body(i, carry):
            sl = pl.ds(pl.multiple_of(i * tq, tq), tq)
            ek = (jnp.dot(cp_ref[sl, :], sel, preferred_element_type=F32) + ones).astype(BF16)
            for g in range(hpg):
                kx_ref[g, sl, :dh] = k_ref[sl, cols[g]]
                kx_ref[g, sl, dh:] = ek[:, cols[g]]
                vt_ref[g, :dh, sl] = v_ref[sl, cols[g]].astype(F32).T.astype(BF16)
            return carry
        lax.fori_loop(0, S // tq, body, 0)

    q0 = pl.multiple_of(qi * tq, tq)
    sel_q = _piece_selector(hg * hpg, hpg, heads, 1.0, N_PIECES)
    cref = jnp.dot(cp_ref[pl.ds(q0, ONES_ROWS), :], sel_q, preferred_element_type=F32)[0:1, :]
    eq = (cref + jnp.where(lane < N_PIECES, 1.0, 0.0)).astype(BF16)
    qxt = [jnp.concatenate([q_ref[:, cols[g]].astype(F32),
                            jnp.broadcast_to(eq[:, cols[g]].astype(F32), (tq, dh))], axis=1
                           ).T.astype(BF16) for g in range(hpg)]

    def scores(g, j, diag):
        k0 = pl.multiple_of(j * tk, tk)
        st = jnp.dot(kx_ref[g, pl.ds(k0, tk), :], qxt[g], preferred_element_type=F32)
        if diag is not None:
            kk = lax.broadcasted_iota(jnp.int32, st.shape, 0) + diag * tk
            qq = lax.broadcasted_iota(jnp.int32, st.shape, 1)
            st = jnp.where(kk <= qq, st, NEG_BIG)
        return st, jnp.max(st, axis=0, keepdims=True)

    def pv_update(g, j, slot, alpha):
        k0 = pl.multiple_of(j * tk, tk)
        pv = jnp.dot(vt_ref[g, :, pl.ds(k0, tk)], p_ref[g, slot], preferred_element_type=F32)
        acc_ref[g] = alpha * acc_ref[g] + pv

    def stage(g, slot, carry, prev_blk, next_blk, next_diag=None):
        m, alpha_prev, cmax = carry
        if prev_blk is not None:
            pv_update(g, prev_blk, 1 - slot, alpha_prev)
        st_next, cmax_next = scores(g, next_blk, next_diag)
        st_ref[g, 1 - slot] = st_next
        m_new = jnp.maximum(m, cmax)
        p_ref[g, slot] = jnp.exp2(st_ref[g, slot] - m_new).astype(BF16)
        return m_new, jnp.exp2(m - m_new), cmax_next

    assert tq == 2 * tk
    d0 = 2 * qi
    carries = []
    for g in range(hpg):
        acc_ref[g] = jnp.zeros(acc_ref.shape[1:], F32)
        st, cmax = scores(g, d0, 0)
        st_ref[g, 0] = st
        carries.append((jnp.full((1, tq), NEG_BIG, F32), jnp.ones((1, tq), F32), cmax))
    carries = [stage(g, 0, carries[g], None, d0 + 1, 1) for g in range(hpg)]
    carries = [stage(g, 1, carries[g], d0, 0) for g in range(hpg)]

    def pair(i, carries):
        out = []
        for g in range(hpg):
            c = stage(g, 0, carries[g], jnp.where(i == 0, d0 + 1, 2 * i - 1), 2 * i + 1)
            out.append(stage(g, 1, c, 2 * i, 2 * i + 2))
        return tuple(out)

    carries = lax.fori_loop(0, qi, pair, tuple(carries))
    last_blk = jnp.where(qi == 0, d0 + 1, d0 - 1)
    for g in range(hpg):
        pv_update(g, last_blk, 1, carries[g][1])
    for g in range(hpg):
        ot = acc_ref[g, :dh, :] / acc_ref[g, dh:dh + 1, :]
        o_ref[:, cols[g]] = ot.T.astype(o_ref.dtype)


def _fox_attention(qkv, cp, *, batch, heads, casts=(), tq=512, hpg=4):
    T = qkv.shape[0]
    S = T // batch
    dh = FOX_HEAD_DIM
    tq = min(tq, S)
    tk = tq // 2
    nq = S // tq
    ng = heads // hpg
    w = hpg * dh
    grid = (batch, ng, nq)
    c_in, c_args, c_out, c_shapes, c_widths = _side_cast_plan(casts, grid)
    return pl.pallas_call(
        functools.partial(_fox_attn_kernel, tq=tq, tk=tk, hpg=hpg, heads=heads,
                          cast_widths=c_widths),
        grid=grid,
        in_specs=[
            pl.BlockSpec((tq, w), lambda b, h, i: (b * nq + i, h)),
            pl.BlockSpec((S, w), lambda b, h, i: (b, ng + h)),
            pl.BlockSpec((S, w), lambda b, h, i: (b, 2 * ng + h)),
            pl.BlockSpec((S, LANES), lambda b, h, i: (b, 0)),
        ] + c_in,
        out_specs=[pl.BlockSpec((tq, w), lambda b, h, i: (b * nq + i, h))] + c_out,
        out_shape=[jax.ShapeDtypeStruct((T, heads * dh), BF16)] + c_shapes,
        scratch_shapes=[pltpu.VMEM((hpg, S, 2 * dh), BF16),
                        pltpu.VMEM((hpg, dh + ONES_ROWS, S), BF16),
                        pltpu.VMEM((hpg, dh + ONES_ROWS, tq), F32),
                        pltpu.VMEM((hpg, 2, tk, tq), F32),
                        pltpu.VMEM((hpg, 2, tk, tq), BF16)],
        compiler_params=_params(("parallel", "parallel", "arbitrary")),
        name="fox_attention",
    )(qkv, qkv, qkv, cp, *c_args)


MXU_COLS = 256


def _matmul_res_kernel(a_ref, w_ref, r_ref, o_ref):
    for c0 in range(0, w_ref.shape[1], MXU_COLS):
        cols = slice(c0, c0 + MXU_COLS)
        o_ref[:, cols] = r_ref[:, cols] + jnp.dot(a_ref[...], w_ref[:, cols],
                                                  preferred_element_type=F32)


def _matmul_res(a, w, res, *, tm=512):
    T, K = a.shape
    N = w.shape[1]
    tm = _tile(T, tm)
    return pl.pallas_call(
        _matmul_res_kernel,
        grid=(T // tm,),
        in_specs=[pl.BlockSpec((tm, K), lambda i: (i, 0)),
                  pl.BlockSpec((K, N), lambda i: (0, 0), pipeline_mode=pl.Buffered(1)),
                  pl.BlockSpec((tm, N), lambda i: (i, 0))],
        out_specs=pl.BlockSpec((tm, N), lambda i: (i, 0)),
        out_shape=jax.ShapeDtypeStruct((T, N), F32),
        compiler_params=_params(("parallel",)),
        name="matmul_res",
    )(a, w, res)


def _mlp_kernel(x_ref, g_ref, wu_ref, wd_ref, *rest, has_final, row_chunk, cast_widths):
    rest = list(rest)
    gf_ref = rest.pop(0) if has_final else None
    cast_in = [rest.pop(0) for _ in range(_n_cast_in(cast_widths))]
    o_ref = rest.pop(0)
    cast_out = [rest.pop(0) for _ in cast_widths]
    hn_ref, = rest
    _side_cast_body(cast_in, cast_out, cast_widths)
    tm = x_ref.shape[0]
    f = pl.program_id(1)
    chunks = [slice(r0, r0 + row_chunk) for r0 in range(0, tm, row_chunk)]

    @pl.when(f == 0)
    def _():
        for sl in chunks:
            x = x_ref[sl, :]
            hn_ref[sl, :] = _rms_rows(x, g_ref[...]).astype(BF16)
            o_ref[sl, :] = x

    u = jnp.dot(hn_ref[...], wu_ref[...], preferred_element_type=F32)
    a = jnp.square(jnp.maximum(u, 0.0)).astype(BF16)
    o_ref[...] += jnp.dot(a, wd_ref[...], preferred_element_type=F32)

    if has_final:
        @pl.when(f == pl.num_programs(1) - 1)
        def _():
            for sl in chunks:
                o_ref[sl, :] = _rms_rows(o_ref[sl, :], gf_ref[...])


def _mlp(x, g, wu, wd, g_final=None, *, casts=(), tm=1024, tf=512):
    T, D = x.shape
    Fdim = wu.shape[1]
    tm = _tile(T, tm)
    tf = _tile(Fdim, tf)
    has_final = g_final is not None
    in_specs = [pl.BlockSpec((tm, D), lambda i, f: (i, 0)),
                pl.BlockSpec((1, D), lambda i, f: (0, 0)),
                pl.BlockSpec((D, tf), lambda i, f: (0, f)),
                pl.BlockSpec((tf, D), lambda i, f: (f, 0))]
    args = [x, g.reshape(1, D), wu, wd]
    if has_final:
        in_specs.append(pl.BlockSpec((1, D), lambda i, f: (0, 0)))
        args.append(g_final.reshape(1, D))
    grid = (T // tm, Fdim // tf)
    c_in, c_args, c_out, c_shapes, c_widths = _side_cast_plan(casts, grid)
    return pl.pallas_call(
        functools.partial(_mlp_kernel, has_final=has_final, row_chunk=min(256, tm),
                          cast_widths=c_widths),
        grid=grid,
        in_specs=in_specs + c_in,
        out_specs=[pl.BlockSpec((tm, D), lambda i, f: (i, 0))] + c_out,
        out_shape=[jax.ShapeDtypeStruct((T, D), F32)] + c_shapes,
        scratch_shapes=[pltpu.VMEM((tm, D), BF16)],
        compiler_params=_params(("parallel", "arbitrary")),
        name="mlp_final" if has_final else "mlp",
    )(*args, *c_args)


def _retention_kernel(q_ref, k_ref, v_ref, sg_ref, cd_ref, gn_ref, *rest, chunk, nchunks, hps,
                      cast_widths):
    rest = list(rest)
    cast_in = [rest.pop(0) for _ in range(_n_cast_in(cast_widths))]
    o_ref = rest.pop(0)
    cast_out = [rest.pop(0) for _ in cast_widths]
    state_ref, = rest
    _side_cast_body(cast_in, cast_out, cast_widths)

    @pl.when(pl.program_id(2) == 0)
    def _():
        state_ref[...] = jnp.zeros_like(state_ref)

    dk, dv = RET_QK_DIM, RET_V_DIM
    causal = (lax.broadcasted_iota(jnp.int32, (chunk, chunk), 0)
              >= lax.broadcasted_iota(jnp.int32, (chunk, chunk), 1))
    slices = [slice(ci * chunk, (ci + 1) * chunk) for ci in range(nchunks)]
    for h in range(hps):
        qk = slice(h * dk, (h + 1) * dk)
        vc = slice(h * dv, (h + 1) * dv)
        intra, kv = [], []
        for sl in slices:
            q = q_ref[sl, qk]
            k = k_ref[sl, qk]
            v = v_ref[sl, vc]
            inner = lax.dot_general(q, k, _NT, preferred_element_type=F32)
            inner = jnp.where(causal, inner, 0.0).astype(BF16)
            intra.append(jnp.dot(inner, v, preferred_element_type=F32))
            k_t = k.astype(F32).T.astype(BF16)
            kv.append(jnp.dot(k_t, v, preferred_element_type=F32))

        for ci, sl in enumerate(slices):
            state = state_ref[h]
            o = intra[ci] + jnp.dot(q_ref[sl, qk], state.astype(BF16),
                                    preferred_element_type=F32)
            state_ref[h] = cd_ref[h] * (state + kv[ci])

            ms = jnp.mean(o * o, axis=-1, keepdims=True)
            on = o * lax.rsqrt(ms + RMS_EPS) * gn_ref[:, vc]
            o_ref[sl, vc] = (sg_ref[sl, vc].astype(F32) * on).astype(o_ref.dtype)


def _retention_tables(heads, chunk, seq):
    dk, dv = RET_QK_DIM, RET_V_DIM
    half = dk // 2
    pos = np.arange(seq, dtype=np.float64)
    inv = ROPE_BASE ** (-np.arange(half, dtype=np.float64) / half)
    ang = pos[:, None] * inv[None, :]
    log_gamma = np.log1p(-np.exp2(-5.0 - np.arange(heads, dtype=np.float64)))
    chunk_decay = np.broadcast_to(np.exp(log_gamma * chunk)[:, None, None], (heads, 1, dv))
    return (jnp.asarray(np.cos(ang), dtype=F32), jnp.asarray(np.sin(ang), dtype=F32),
            log_gamma, jnp.asarray(chunk_decay, dtype=F32))


def _retention(proj, gn, cd, *, batch, heads, chunk, casts=(), rows=1024, hps=2):
    T = proj.shape[0]
    S = T // batch
    dk, dv = hps * RET_QK_DIM, hps * RET_V_DIM
    rows = min(rows, S)
    ns = S // rows
    heads = heads // hps
    vblk0 = 2 * heads * dk // dv
    grid = (batch, heads, ns)
    c_in, c_args, c_out, c_shapes, c_widths = _side_cast_plan(casts, grid)
    return pl.pallas_call(
        functools.partial(_retention_kernel, chunk=chunk, nchunks=rows // chunk, hps=hps,
                          cast_widths=c_widths),
        grid=grid,
        in_specs=[
            pl.BlockSpec((rows, dk), lambda b, h, s: (b * ns + s, h)),
            pl.BlockSpec((rows, dk), lambda b, h, s: (b * ns + s, heads + h)),
            pl.BlockSpec((rows, dv), lambda b, h, s: (b * ns + s, vblk0 + h)),
            pl.BlockSpec((rows, dv), lambda b, h, s: (b * ns + s, vblk0 + heads + h)),
            pl.BlockSpec((hps, 1, RET_V_DIM), lambda b, h, s: (h, 0, 0)),
            pl.BlockSpec((1, dv), lambda b, h, s: (0, h)),
        ] + c_in,
        out_specs=[pl.BlockSpec((rows, dv), lambda b, h, s: (b * ns + s, h))] + c_out,
        out_shape=[jax.ShapeDtypeStruct((T, heads * dv), BF16)] + c_shapes,
        scratch_shapes=[pltpu.VMEM((hps, RET_QK_DIM, RET_V_DIM), F32)],
        compiler_params=_params(("parallel", "parallel", "arbitrary")),
        name="retention",
    )(proj, proj, proj, proj, cd, gn.reshape(1, heads * dv), *c_args)


def kernel(x, fox_norm, fox_wq, fox_wk, fox_wv, fox_wf, fox_bf, fox_wo, ret_norm, ret_wq, ret_wk,
           ret_wv, ret_wg, ret_gn, ret_wo, mlp_norm, mlp_up, mlp_down, final_norm):
    B, S, D = x.shape
    T = B * S
    fox_heads = D // FOX_HEAD_DIM
    ret_heads = D // RET_QK_DIM
    h = x.reshape(T, D)

    w_qkv = _cast_concat([fox_wq, fox_wk, fox_wv], 0)
    cs = jnp.concatenate([jnp.full((D,), LOG2E * FOX_HEAD_DIM ** -0.5, F32),
                          jnp.ones((2 * D,), F32)])
    wf_pad = jnp.pad(fox_wf[0], ((0, 0), (0, LANES - fox_heads))).astype(BF16)
    bf_pad = jnp.pad(fox_bf[0], (0, LANES - fox_heads)).reshape(1, LANES)
    qkv, z = _norm_matmul(h, fox_norm[0], w_qkv, cs, wf_pad)
    cp = _gate_cumsum(z, bf_pad, B, fox_heads)
    attn, wo0, up0, down0, w_proj, wo1, up1, down1 = _fox_attention(
        qkv, cp, batch=B, heads=fox_heads,
        casts=[([fox_wo], 0), ([mlp_up], 0), ([mlp_down], 0),
               ([ret_wq, ret_wk, ret_wv, ret_wg], 0), ([ret_wo], 0),
               ([mlp_up], 1), ([mlp_down], 1)])
    h = _matmul_res(attn, wo0, h)
    h, = _mlp(h, mlp_norm[0], up0, down0)

    chunk = min(RET_CHUNK, S)
    cos, sin, log_gamma, cd = _retention_tables(ret_heads, chunk, S)
    rv = ret_heads * RET_V_DIM
    cs = jnp.concatenate([jnp.ones((D,), F32), jnp.full((D,), RET_QK_DIM ** -0.5, F32),
                          jnp.ones((2 * rv,), F32)])
    lg_cols = np.repeat(log_gamma, RET_QK_DIM)
    lgcol = jnp.asarray(np.concatenate([lg_cols, -lg_cols, np.zeros(2 * rv)]), dtype=F32)
    proj, = _norm_matmul(h, ret_norm[0], w_proj, cs, ret=(cos, sin, lgcol, 2 * D, rv, chunk))
    y, = _retention(proj, ret_gn[0], cd, batch=B, heads=ret_heads, chunk=chunk)
    h = _matmul_res(y, wo1, h)
    h, = _mlp(h, mlp_norm[1], up1, down1, final_norm)
    return h.reshape(B, S, D)
```

```python
import functools
import math

import numpy as np
import jax
import jax.numpy as jnp
from jax import lax
from jax.experimental import pallas as pl
from jax.experimental.pallas import tpu as pltpu

F32 = jnp.float32
BF16 = jnp.bfloat16

RMS_EPS = 1e-6
ROPE_BASE = 10000.0
FOX_HEAD_DIM = 128
RET_QK_DIM = 256
RET_V_DIM = 512
RET_CHUNK = 256
LANES = 128
BF16_ROWS = 16
NEG_BIG = -1e30
LOG2E = math.log2(math.e)
V7X_VMEM_LIMIT = 56 * 1024 * 1024

_NT = (((1,), (1,)), ((), ()))


def _params(sem, vmem=V7X_VMEM_LIMIT):
    return pltpu.CompilerParams(dimension_semantics=sem, vmem_limit_bytes=vmem)


def _tile(n, target, align=LANES):
    if n <= target:
        return n
    t = (target // align) * align
    while n % t:
        t -= align
    return t


def _rms_rows(x, g):
    ms = jnp.mean(x * x, axis=-1, keepdims=True)
    return x * lax.rsqrt(ms + RMS_EPS) * g


def _cast_concat_kernel(*refs, starts, nblks):
    o_ref = refs[-1]
    p = pl.program_id(0)
    for w_ref, s0, nb in zip(refs[:-1], starts, nblks):
        @pl.when((p >= s0) & (p < s0 + nb))
        def _(w_ref=w_ref):
            o_ref[...] = w_ref[...].astype(o_ref.dtype)


def _cast_concat(ws, layer, *, bw=2048, tr=512):
    R = ws[0].shape[1]
    bw = min(bw, min(w.shape[2] for w in ws))
    tr = _tile(R, tr)
    nr = R // tr
    nblks = [w.shape[2] // bw for w in ws]
    starts = [sum(nblks[:i]) for i in range(len(ws))]

    def in_map(s0, nb):
        def index(p, r):
            local = p - s0
            row = jnp.where(local < 0, 0, jnp.where(local >= nb, nr - 1, r))
            return layer, row, jnp.clip(local, 0, nb - 1)
        return index

    return pl.pallas_call(
        functools.partial(_cast_concat_kernel, starts=starts, nblks=nblks),
        grid=(sum(nblks), nr),
        in_specs=[pl.BlockSpec((None, tr, bw), in_map(s0, nb)) for s0, nb in zip(starts, nblks)],
        out_specs=pl.BlockSpec((tr, bw), lambda p, r: (r, p)),
        out_shape=jax.ShapeDtypeStruct((R, sum(nblks) * bw), BF16),
        compiler_params=_params(("arbitrary", "arbitrary")),
        name="cast_concat",
    )(*ws)


def _side_cast_plan(groups, grid):
    nsteps = math.prod(grid)

    def lin(*ids):
        l = ids[0]
        for n, i in zip(grid[1:], ids[1:]):
            l = l * n + i
        return l

    in_specs, args, out_specs, out_shapes, widths = [], [], [], [], []
    for ws, layer in groups:
        R = ws[0].shape[1]
        rows = BF16_ROWS
        while R // rows > nsteps:
            rows *= 2
        assert R % rows == 0
        nblk = R // rows
        rep = nsteps // nblk

        def blk(*ids, rep=rep, nblk=nblk):
            return jnp.minimum(lin(*ids) // rep, nblk - 1)

        for w in ws:
            in_specs.append(pl.BlockSpec((None, rows, w.shape[2]),
                                         lambda *ids, blk=blk, layer=layer: (layer, blk(*ids), 0)))
            args.append(w)
        cols = sum(w.shape[2] for w in ws)
        out_specs.append(pl.BlockSpec((rows, cols), lambda *ids, blk=blk: (blk(*ids), 0)))
        out_shapes.append(jax.ShapeDtypeStruct((R, cols), BF16))
        widths.append(tuple(w.shape[2] for w in ws))
    return in_specs, args, out_specs, out_shapes, tuple(widths)


def _side_cast_body(in_refs, out_refs, widths):
    k = 0
    for o_ref, ws in zip(out_refs, widths):
        c0 = 0
        for wd in ws:
            o_ref[:, c0:c0 + wd] = in_refs[k][...].astype(BF16)
            k += 1
            c0 += wd


def _n_cast_in(widths):
    return sum(len(ws) for ws in widths)


def _norm_matmul_kernel(x_ref, g_ref, w_ref, cs_ref, *rest, has_gate, ret, row_chunk, cast_widths):
    rest = list(rest)
    wz_ref = rest.pop(0) if has_gate else None
    if ret is not None:
        cos_ref, sin_ref, lg_ref = rest.pop(0), rest.pop(0), rest.pop(0)
    cast_in = [rest.pop(0) for _ in range(_n_cast_in(cast_widths))]
    o_ref = rest.pop(0)
    z_ref = rest.pop(0) if has_gate else None
    cast_out = [rest.pop(0) for _ in cast_widths]
    hn_ref, = rest
    tm = x_ref.shape[0]
    _side_cast_body(cast_in, cast_out, cast_widths)

    @pl.when(pl.program_id(1) == 0)
    def _():
        for r0 in range(0, tm, row_chunk):
            y = _rms_rows(x_ref[r0:r0 + row_chunk, :], g_ref[...])
            hn_ref[r0:r0 + row_chunk, :] = y.astype(BF16)
        if has_gate:
            z_ref[...] = jnp.dot(hn_ref[...], wz_ref[...], preferred_element_type=F32)

    tn = w_ref.shape[1]

    def heads(epilogue):
        for c0 in range(0, tn, RET_QK_DIM):
            acc = jnp.dot(hn_ref[...], w_ref[:, c0:c0 + RET_QK_DIM], preferred_element_type=F32)
            epilogue(c0, acc)

    def plain(c0, acc):
        cols = slice(c0, c0 + RET_QK_DIM)
        o_ref[:, cols] = (acc * cs_ref[:, cols]).astype(o_ref.dtype)

    if ret is None:
        heads(plain)
        return

    n_rot, n_plain, chunk = ret
    j = pl.program_id(1)
    half = RET_QK_DIM // 2

    @pl.when(j < n_rot)
    def _():
        t = (lax.broadcasted_iota(jnp.int32, (tm, half), 0) % chunk + 1).astype(F32)
        cos = cos_ref[...]
        sin = sin_ref[...]

        def rotary(c0, acc):
            h1 = slice(c0, c0 + half)
            h2 = slice(c0 + half, c0 + 2 * half)
            x1 = acc[:, :half]
            x2 = acc[:, half:]
            scale = jnp.exp(t * lg_ref[:, h1]) * cs_ref[:, h1]
            o_ref[:, h1] = ((x1 * cos - x2 * sin) * scale).astype(o_ref.dtype)
            o_ref[:, h2] = ((x1 * sin + x2 * cos) * scale).astype(o_ref.dtype)
        heads(rotary)

    @pl.when((j >= n_rot) & (j < n_rot + n_plain))
    def _():
        heads(plain)

    @pl.when(j >= n_rot + n_plain)
    def _():
        def swish(c0, acc):
            cols = slice(c0, c0 + RET_QK_DIM)
            y = acc * cs_ref[:, cols]
            sig = 0.5 * jnp.tanh(0.5 * y) + 0.5
            o_ref[:, cols] = (y * sig).astype(o_ref.dtype)
        heads(swish)


def _norm_matmul(x, g, w, colscale, wz=None, *, ret=None, casts=(), tm=1024, tn=2048):
    T, D = x.shape
    N = w.shape[1]
    tm = _tile(T, tm)
    tn = _tile(N if ret is None else math.gcd(ret[3], ret[4]), tn)
    has_gate = wz is not None
    grid = (T // tm, N // tn)
    c_in, c_args, c_out, c_shapes, c_widths = _side_cast_plan(casts, grid)
    in_specs = [
        pl.BlockSpec((tm, D), lambda i, j: (i, 0)),
        pl.BlockSpec((1, D), lambda i, j: (0, 0)),
        pl.BlockSpec((D, tn), lambda i, j: (0, j)),
        pl.BlockSpec((1, tn), lambda i, j: (0, j)),
    ]
    args = [x, g.reshape(1, D), w, colscale.reshape(1, N)]
    out_shape = [jax.ShapeDtypeStruct((T, N), BF16)]
    out_specs = [pl.BlockSpec((tm, tn), lambda i, j: (i, j))]
    if has_gate:
        in_specs.append(pl.BlockSpec((D, LANES), lambda i, j: (0, 0)))
        args.append(wz)
        out_shape.append(jax.ShapeDtypeStruct((T, LANES), F32))
        out_specs.append(pl.BlockSpec((tm, LANES), lambda i, j: (i, 0)))
    ret_static = None
    if ret is not None:
        cos, sin, lgcol, rot_cols, plain_cols, chunk = ret
        assert rot_cols % tn == 0 and plain_cols % tn == 0 and tm % chunk == 0
        nrt = cos.shape[0] // tm
        half = RET_QK_DIM // 2
        in_specs += [pl.BlockSpec((tm, half), lambda i, j: (i % nrt, 0)),
                     pl.BlockSpec((tm, half), lambda i, j: (i % nrt, 0)),
                     pl.BlockSpec((1, tn), lambda i, j: (0, j))]
        args += [cos, sin, lgcol.reshape(1, N)]
        ret_static = (rot_cols // tn, plain_cols // tn, chunk)
    res = pl.pallas_call(
        functools.partial(_norm_matmul_kernel, has_gate=has_gate, ret=ret_static,
                          row_chunk=min(256, tm), cast_widths=c_widths),
        grid=grid,
        in_specs=in_specs + c_in,
        out_specs=out_specs + c_out,
        out_shape=out_shape + c_shapes,
        scratch_shapes=[pltpu.VMEM((tm, D), BF16)],
        compiler_params=_params(("parallel", "arbitrary")),
        name="norm_matmul_gate" if has_gate else "norm_matmul",
    )(*args, *c_args)
    return res


def _gate_cumsum_kernel(z_ref, b_ref, cp_ref, *, heads):
    z = z_ref[...] + b_ref[...]
    c = jnp.minimum(z, 0.0) - jnp.log1p(jnp.exp(-jnp.abs(z)))
    S = c.shape[0]
    row = lax.broadcasted_iota(jnp.int32, c.shape, 0)
    shift = 1
    while shift < S:
        c = c + jnp.where(row >= shift, pltpu.roll(c, shift, axis=0), 0.0)
        shift *= 2
    c = c * LOG2E
    hi = c.astype(BF16).astype(F32)
    r1 = c - hi
    mid = r1.astype(BF16).astype(F32)
    lo = r1 - mid
    lane = lax.broadcasted_iota(jnp.int32, c.shape, 1)
    packed = jnp.where(lane < heads, hi,
                       jnp.where(lane < 2 * heads, pltpu.roll(mid, heads, axis=1),
                                 jnp.where(lane < 3 * heads, pltpu.roll(lo, 2 * heads, axis=1), 0.0)))
    cp_ref[...] = packed.astype(BF16)


def _gate_cumsum(z, b_pad, batch, heads):
    T = z.shape[0]
    S = T // batch
    assert 3 * heads <= LANES
    return pl.pallas_call(
        functools.partial(_gate_cumsum_kernel, heads=heads),
        grid=(batch,),
        in_specs=[pl.BlockSpec((S, LANES), lambda b: (b, 0)),
                  pl.BlockSpec((1, LANES), lambda b: (0, 0))],
        out_specs=pl.BlockSpec((S, LANES), lambda b: (b, 0)),
        out_shape=jax.ShapeDtypeStruct((T, LANES), BF16),
        compiler_params=_params(("parallel",)),
        name="gate_cumsum",
    )(z, b_pad)


ONES_ROWS = 16
N_PIECES = 3


def _piece_selector(head0, hpg, heads, sign, lane0):
    r = lax.broadcasted_iota(jnp.int32, (LANES, hpg * LANES), 0)
    col = lax.broadcasted_iota(jnp.int32, (LANES, hpg * LANES), 1)
    g = col // LANES
    p = col % LANES - lane0
    hit = (p >= 0) & (p < N_PIECES) & (r == head0 + g + heads * p)
    return jnp.where(hit, sign, 0.0).astype(BF16)


def _fox_attn_kernel(q_ref, k_ref, v_ref, cp_ref, *rest, tq, tk, hpg, heads, cast_widths):
    rest = list(rest)
    cast_in = [rest.pop(0) for _ in range(_n_cast_in(cast_widths))]
    o_ref = rest.pop(0)
    cast_out = [rest.pop(0) for _ in cast_widths]
    kx_ref, vt_ref, acc_ref, st_ref, p_ref = rest
    _side_cast_body(cast_in, cast_out, cast_widths)
    hg = pl.program_id(1)
    qi = pl.program_id(2)
    S = k_ref.shape[0]
    dh = FOX_HEAD_DIM
    lane = lax.broadcasted_iota(jnp.int32, (1, hpg * LANES), 1) % LANES
    cols = [slice(g * dh, (g + 1) * dh) for g in range(hpg)]

    @pl.when(qi == 0)
    def _():
        sel = _piece_selector(hg * hpg, hpg, heads, -1.0, 0)
        ones = jnp.where((lane >= N_PIECES) & (lane < 2 * N_PIECES), 1.0, 0.0)
        for g in range(hpg):
            vt_ref[g, dh:, :] = jnp.ones((ONES_ROWS, S), BF16)

        def body(i, carry):
            sl = pl.ds(pl.multiple_of(i * tq, tq), tq)
            ek = (jnp.dot(cp_ref[sl, :], sel, preferred_element_type=F32) + ones).astype(BF16)
            for g in range(hpg):
                kx_ref[g, sl, :dh] = k_ref[sl, cols[g]]
                kx_ref[g, sl, dh:] = ek[:, cols[g]]
                vt_ref[g, :dh, sl] = v_ref[sl, cols[g]].astype(F32).T.astype(BF16)
            return carry
        lax.fori_loop(0, S // tq, body, 0)

    q0 = pl.multiple_of(qi * tq, tq)
    sel_q = _piece_selector(hg * hpg, hpg, heads, 1.0, N_PIECES)
    cref = jnp.dot(cp_ref[pl.ds(q0, ONES_ROWS), :], sel_q, preferred_element_type=F32)[0:1, :]
    eq = (cref + jnp.where(lane < N_PIECES, 1.0, 0.0)).astype(BF16)
    qxt = [jnp.concatenate([q_ref[:, cols[g]].astype(F32),
                            jnp.broadcast_to(eq[:, cols[g]].astype(F32), (tq, dh))], axis=1
                           ).T.astype(BF16) for g in range(hpg)]

    def scores(g, j, diag):
        k0 = pl.multiple_of(j * tk, tk)
        st = jnp.dot(kx_ref[g, pl.ds(k0, tk), :], qxt[g], preferred_element_type=F32)
        if diag is not None:
            kk = lax.broadcasted_iota(jnp.int32, st.shape, 0) + diag * tk
            qq = lax.broadcasted_iota(jnp.int32, st.shape, 1)
            st = jnp.where(kk <= qq, st, NEG_BIG)
        return st, jnp.max(st, axis=0, keepdims=True)

    def pv_update(g, j, slot, alpha):
        k0 = pl.multiple_of(j * tk, tk)
        pv = jnp.dot(vt_ref[g, :, pl.ds(k0, tk)], p_ref[g, slot], preferred_element_type=F32)
        acc_ref[g] = alpha * acc_ref[g] + pv

    def stage(g, slot, carry, prev_blk, next_blk, next_diag=None):
        m, alpha_prev, cmax = carry
        if prev_blk is not None:
            pv_update(g, prev_blk, 1 - slot, alpha_prev)
        st_next, cmax_next = scores(g, next_blk, next_diag)
        st_ref[g, 1 - slot] = st_next
        m_new = jnp.maximum(m, cmax)
        p_ref[g, slot] = jnp.exp2(st_ref[g, slot] - m_new).astype(BF16)
        return m_new, jnp.exp2(m - m_new), cmax_next

    assert tq == 2 * tk
    d0 = 2 * qi
    carries = []
    for g in range(hpg):
        acc_ref[g] = jnp.zeros(acc_ref.shape[1:], F32)
        st, cmax = scores(g, d0, 0)
        st_ref[g, 0] = st
        carries.append((jnp.full((1, tq), NEG_BIG, F32), jnp.ones((1, tq), F32), cmax))
    carries = [stage(g, 0, carries[g], None, d0 + 1, 1) for g in range(hpg)]
    carries = [stage(g, 1, carries[g], d0, 0) for g in range(hpg)]

    def pair(i, carries):
        out = []
        for g in range(hpg):
            c = stage(g, 0, carries[g], jnp.where(i == 0, d0 + 1, 2 * i - 1), 2 * i + 1)
            out.append(stage(g, 1, c, 2 * i, 2 * i + 2))
        return tuple(out)

    carries = lax.fori_loop(0, qi, pair, tuple(carries))
    last_blk = jnp.where(qi == 0, d0 + 1, d0 - 1)
    for g in range(hpg):
        pv_update(g, last_blk, 1, carries[g][1])
    for g in range(hpg):
        ot = acc_ref[g, :dh, :] / acc_ref[g, dh:dh + 1, :]
        o_ref[:, cols[g]] = ot.T.astype(o_ref.dtype)


def _fox_attention(qkv, cp, *, batch, heads, casts=(), tq=512, hpg=4):
    T = qkv.shape[0]
    S = T // batch
    dh = FOX_HEAD_DIM
    tq = min(tq, S)
    tk = tq // 2
    nq = S // tq
    ng = heads // hpg
    w = hpg * dh
    grid = (batch, ng, nq)
    c_in, c_args, c_out, c_shapes, c_widths = _side_cast_plan(casts, grid)
    return pl.pallas_call(
        functools.partial(_fox_attn_kernel, tq=tq, tk=tk, hpg=hpg, heads=heads,
                          cast_widths=c_widths),
        grid=grid,
        in_specs=[
            pl.BlockSpec((tq, w), lambda b, h, i: (b * nq + i, h)),
            pl.BlockSpec((S, w), lambda b, h, i: (b, ng + h)),
            pl.BlockSpec((S, w), lambda b, h, i: (b, 2 * ng + h)),
            pl.BlockSpec((S, LANES), lambda b, h, i: (b, 0)),
        ] + c_in,
        out_specs=[pl.BlockSpec((tq, w), lambda b, h, i: (b * nq + i, h))] + c_out,
        out_shape=[jax.ShapeDtypeStruct((T, heads * dh), BF16)] + c_shapes,
        scratch_shapes=[pltpu.VMEM((hpg, S, 2 * dh), BF16),
                        pltpu.VMEM((hpg, dh + ONES_ROWS, S), BF16),
                        pltpu.VMEM((hpg, dh + ONES_ROWS, tq), F32),
                        pltpu.VMEM((hpg, 2, tk, tq), F32),
                        pltpu.VMEM((hpg, 2, tk, tq), BF16)],
        compiler_params=_params(("parallel", "parallel", "arbitrary")),
        name="fox_attention",
    )(qkv, qkv, qkv, cp, *c_args)


MXU_COLS = 256


def _matmul_res_kernel(a_ref, w_ref, r_ref, o_ref):
    for c0 in range(0, w_ref.shape[1], MXU_COLS):
        cols = slice(c0, c0 + MXU_COLS)
        o_ref[:, cols] = r_ref[:, cols] + jnp.dot(a_ref[...], w_ref[:, cols],
                                                  preferred_element_type=F32)


def _matmul_res(a, w, res, *, tm=512):
    T, K = a.shape
    N = w.shape[1]
    tm = _tile(T, tm)
    return pl.pallas_call(
        _matmul_res_kernel,
        grid=(T // tm,),
        in_specs=[pl.BlockSpec((tm, K), lambda i: (i, 0)),
                  pl.BlockSpec((K, N), lambda i: (0, 0), pipeline_mode=pl.Buffered(1)),
                  pl.BlockSpec((tm, N), lambda i: (i, 0))],
        out_specs=pl.BlockSpec((tm, N), lambda i: (i, 0)),
        out_shape=jax.ShapeDtypeStruct((T, N), F32),
        compiler_params=_params(("parallel",)),
        name="matmul_res",
    )(a, w, res)


def _mlp_kernel(x_ref, g_ref, wu_ref, wd_ref, *rest, has_final, row_chunk, cast_widths):
    rest = list(rest)
    gf_ref = rest.pop(0) if has_final else None
    cast_in = [rest.pop(0) for _ in range(_n_cast_in(cast_widths))]
    o_ref = rest.pop(0)
    cast_out = [rest.pop(0) for _ in cast_widths]
    hn_ref, = rest
    _side_cast_body(cast_in, cast_out, cast_widths)
    tm = x_ref.shape[0]
    f = pl.program_id(1)
    chunks = [slice(r0, r0 + row_chunk) for r0 in range(0, tm, row_chunk)]

    @pl.when(f == 0)
    def _():
        for sl in chunks:
            x = x_ref[sl, :]
            hn_ref[sl, :] = _rms_rows(x, g_ref[...]).astype(BF16)
            o_ref[sl, :] = x

    u = jnp.dot(hn_ref[...], wu_ref[...], preferred_element_type=F32)
    a = jnp.square(jnp.maximum(u, 0.0)).astype(BF16)
    o_ref[...] += jnp.dot(a, wd_ref[...], preferred_element_type=F32)

    if has_final:
        @pl.when(f == pl.num_programs(1) - 1)
        def _():
            for sl in chunks:
                o_ref[sl, :] = _rms_rows(o_ref[sl, :], gf_ref[...])


def _mlp(x, g, wu, wd, g_final=None, *, casts=(), tm=1024, tf=512):
    T, D = x.shape
    Fdim = wu.shape[1]
    tm = _tile(T, tm)
    tf = _tile(Fdim, tf)
    has_final = g_final is not None
    in_specs = [pl.BlockSpec((tm, D), lambda i, f: (i, 0)),
                pl.BlockSpec((1, D), lambda i, f: (0, 0)),
                pl.BlockSpec((D, tf), lambda i, f: (0, f)),
                pl.BlockSpec((tf, D), lambda i, f: (f, 0))]
    args = [x, g.reshape(1, D), wu, wd]
    if has_final:
        in_specs.append(pl.BlockSpec((1, D), lambda i, f: (0, 0)))
        args.append(g_final.reshape(1, D))
    grid = (T // tm, Fdim // tf)
    c_in, c_args, c_out, c_shapes, c_widths = _side_cast_plan(casts, grid)
    return pl.pallas_call(
        functools.partial(_mlp_kernel, has_final=has_final, row_chunk=min(256, tm),
                          cast_widths=c_widths),
        grid=grid,
        in_specs=in_specs + c_in,
        out_specs=[pl.BlockSpec((tm, D), lambda i, f: (i, 0))] + c_out,
        out_shape=[jax.ShapeDtypeStruct((T, D), F32)] + c_shapes,
        scratch_shapes=[pltpu.VMEM((tm, D), BF16)],
        compiler_params=_params(("parallel", "arbitrary")),
        name="mlp_final" if has_final else "mlp",
    )(*args, *c_args)


def _retention_kernel(q_ref, k_ref, v_ref, sg_ref, cd_ref, gn_ref, *rest, chunk, nchunks, hps,
                      cast_widths):
    rest = list(rest)
    cast_in = [rest.pop(0) for _ in range(_n_cast_in(cast_widths))]
    o_ref = rest.pop(0)
    cast_out = [rest.pop(0) for _ in cast_widths]
    state_ref, = rest
    _side_cast_body(cast_in, cast_out, cast_widths)

    @pl.when(pl.program_id(2) == 0)
    def _():
        state_ref[...] = jnp.zeros_like(state_ref)

    dk, dv = RET_QK_DIM, RET_V_DIM
    causal = (lax.broadcasted_iota(jnp.int32, (chunk, chunk), 0)
              >= lax.broadcasted_iota(jnp.int32, (chunk, chunk), 1))
    slices = [slice(ci * chunk, (ci + 1) * chunk) for ci in range(nchunks)]
    for h in range(hps):
        qk = slice(h * dk, (h + 1) * dk)
        vc = slice(h * dv, (h + 1) * dv)
        intra, kv = [], []
        for sl in slices:
            q = q_ref[sl, qk]
            k = k_ref[sl, qk]
            v = v_ref[sl, vc]
            inner = lax.dot_general(q, k, _NT, preferred_element_type=F32)
            inner = jnp.where(causal, inner, 0.0).astype(BF16)
            intra.append(jnp.dot(inner, v, preferred_element_type=F32))
            k_t = k.astype(F32).T.astype(BF16)
            kv.append(jnp.dot(k_t, v, preferred_element_type=F32))

        for ci, sl in enumerate(slices):
            state = state_ref[h]
            o = intra[ci] + jnp.dot(q_ref[sl, qk], state.astype(BF16),
                                    preferred_element_type=F32)
            state_ref[h] = cd_ref[h] * (state + kv[ci])

            ms = jnp.mean(o * o, axis=-1, keepdims=True)
            on = o * lax.rsqrt(ms + RMS_EPS) * gn_ref[:, vc]
            o_ref[sl, vc] = (sg_ref[sl, vc].astype(F32) * on).astype(o_ref.dtype)


def _retention_tables(heads, chunk, seq):
    dk, dv = RET_QK_DIM, RET_V_DIM
    half = dk // 2
    pos = np.arange(seq, dtype=np.float64)
    inv = ROPE_BASE ** (-np.arange(half, dtype=np.float64) / half)
    ang = pos[:, None] * inv[None, :]
    log_gamma = np.log1p(-np.exp2(-5.0 - np.arange(heads, dtype=np.float64)))
    chunk_decay = np.broadcast_to(np.exp(log_gamma * chunk)[:, None, None], (heads, 1, dv))
    return (jnp.asarray(np.cos(ang), dtype=F32), jnp.asarray(np.sin(ang), dtype=F32),
            log_gamma, jnp.asarray(chunk_decay, dtype=F32))


def _retention(proj, gn, cd, *, batch, heads, chunk, casts=(), rows=1024, hps=4):
    T = proj.shape[0]
    S = T // batch
    hps = math.gcd(hps, heads)
    dk, dv = hps * RET_QK_DIM, hps * RET_V_DIM
    rows = min(rows, S)
    ns = S // rows
    heads = heads // hps
    vblk0 = 2 * heads * dk // dv
    grid = (batch, heads, ns)
    c_in, c_args, c_out, c_shapes, c_widths = _side_cast_plan(casts, grid)
    return pl.pallas_call(
        functools.partial(_retention_kernel, chunk=chunk, nchunks=rows // chunk, hps=hps,
                          cast_widths=c_widths),
        grid=grid,
        in_specs=[
            pl.BlockSpec((rows, dk), lambda b, h, s: (b * ns + s, h)),
            pl.BlockSpec((rows, dk), lambda b, h, s: (b * ns + s, heads + h)),
            pl.BlockSpec((rows, dv), lambda b, h, s: (b * ns + s, vblk0 + h)),
            pl.BlockSpec((rows, dv), lambda b, h, s: (b * ns + s, vblk0 + heads + h)),
            pl.BlockSpec((hps, 1, RET_V_DIM), lambda b, h, s: (h, 0, 0)),
            pl.BlockSpec((1, dv), lambda b, h, s: (0, h)),
        ] + c_in,
        out_specs=[pl.BlockSpec((rows, dv), lambda b, h, s: (b * ns + s, h))] + c_out,
        out_shape=[jax.ShapeDtypeStruct((T, heads * dv), BF16)] + c_shapes,
        scratch_shapes=[pltpu.VMEM((hps, RET_QK_DIM, RET_V_DIM), F32)],
        compiler_params=_params(("parallel", "parallel", "arbitrary")),
        name="retention",
    )(proj, proj, proj, proj, cd, gn.reshape(1, heads * dv), *c_args)


def kernel(x, fox_norm, fox_wq, fox_wk, fox_wv, fox_wf, fox_bf, fox_wo, ret_norm, ret_wq, ret_wk,
           ret_wv, ret_wg, ret_gn, ret_wo, mlp_norm, mlp_up, mlp_down, final_norm):
    B, S, D = x.shape
    T = B * S
    fox_heads = D // FOX_HEAD_DIM
    ret_heads = D // RET_QK_DIM
    h = x.reshape(T, D)

    w_qkv = _cast_concat([fox_wq, fox_wk, fox_wv], 0)
    cs = jnp.concatenate([jnp.full((D,), LOG2E * FOX_HEAD_DIM ** -0.5, F32),
                          jnp.ones((2 * D,), F32)])
    wf_pad = jnp.pad(fox_wf[0], ((0, 0), (0, LANES - fox_heads))).astype(BF16)
    bf_pad = jnp.pad(fox_bf[0], (0, LANES - fox_heads)).reshape(1, LANES)
    qkv, z = _norm_matmul(h, fox_norm[0], w_qkv, cs, wf_pad)
    cp = _gate_cumsum(z, bf_pad, B, fox_heads)
    attn, wo0, up0, down0, w_proj, wo1, up1, down1 = _fox_attention(
        qkv, cp, batch=B, heads=fox_heads,
        casts=[([fox_wo], 0), ([mlp_up], 0), ([mlp_down], 0),
               ([ret_wq, ret_wk, ret_wv, ret_wg], 0), ([ret_wo], 0),
               ([mlp_up], 1), ([mlp_down], 1)])
    h = _matmul_res(attn, wo0, h)
    h, = _mlp(h, mlp_norm[0], up0, down0)

    chunk = min(RET_CHUNK, S)
    cos, sin, log_gamma, cd = _retention_tables(ret_heads, chunk, S)
    rv = ret_heads * RET_V_DIM
    cs = jnp.concatenate([jnp.ones((D,), F32), jnp.full((D,), RET_QK_DIM ** -0.5, F32),
                          jnp.ones((2 * rv,), F32)])
    lg_cols = np.repeat(log_gamma, RET_QK_DIM)
    lgcol = jnp.asarray(np.concatenate([lg_cols, -lg_cols, np.zeros(2 * rv)]), dtype=F32)
    proj, = _norm_matmul(h, ret_norm[0], w_proj, cs, ret=(cos, sin, lgcol, 2 * D, rv, chunk))
    y, = _retention(proj, ret_gn[0], cd, batch=B, heads=ret_heads, chunk=chunk)
    h = _matmul_res(y, wo1, h)
    h, = _mlp(h, mlp_norm[1], up1, down1, final_norm)
    return h.reshape(B, S, D)
```

```python
import functools
import math

import numpy as np
import jax
import jax.numpy as jnp
from jax import lax
from jax.experimental import pallas as pl
from jax.experimental.pallas import tpu as pltpu

F32 = jnp.float32
BF16 = jnp.bfloat16

RMS_EPS = 1e-6
ROPE_BASE = 10000.0
FOX_HEAD_DIM = 128
RET_QK_DIM = 256
RET_V_DIM = 512
RET_CHUNK = 256
LANES = 128
BF16_ROWS = 16
NEG_BIG = -1e30
LOG2E = math.log2(math.e)
V7X_VMEM_LIMIT = 56 * 1024 * 1024
V7X_VMEM_LIMIT_MLP = 62 * 1024 * 1024

_NT = (((1,), (1,)), ((), ()))


def _params(sem, vmem=V7X_VMEM_LIMIT):
    return pltpu.CompilerParams(dimension_semantics=sem, vmem_limit_bytes=vmem)


def _tile(n, target, align=LANES):
    if n <= target:
        return n
    t = (target // align) * align
    while n % t:
        t -= align
    return t


def _rms_rows(x, g):
    ms = jnp.mean(x * x, axis=-1, keepdims=True)
    return x * lax.rsqrt(ms + RMS_EPS) * g


def _cast_concat_kernel(*refs, starts, nblks):
    o_ref = refs[-1]
    p = pl.program_id(0)
    for w_ref, s0, nb in zip(refs[:-1], starts, nblks):
        @pl.when((p >= s0) & (p < s0 + nb))
        def _(w_ref=w_ref):
            o_ref[...] = w_ref[...].astype(o_ref.dtype)


def _cast_concat(ws, layer, *, bw=2048, tr=512):
    R = ws[0].shape[1]
    bw = min(bw, min(w.shape[2] for w in ws))
    tr = _tile(R, tr)
    nr = R // tr
    nblks = [w.shape[2] // bw for w in ws]
    starts = [sum(nblks[:i]) for i in range(len(ws))]

    def in_map(s0, nb):
        def index(p, r):
            local = p - s0
            row = jnp.where(local < 0, 0, jnp.where(local >= nb, nr - 1, r))
            return layer, row, jnp.clip(local, 0, nb - 1)
        return index

    return pl.pallas_call(
        functools.partial(_cast_concat_kernel, starts=starts, nblks=nblks),
        grid=(sum(nblks), nr),
        in_specs=[pl.BlockSpec((None, tr, bw), in_map(s0, nb)) for s0, nb in zip(starts, nblks)],
        out_specs=pl.BlockSpec((tr, bw), lambda p, r: (r, p)),
        out_shape=jax.ShapeDtypeStruct((R, sum(nblks) * bw), BF16),
        compiler_params=_params(("arbitrary", "arbitrary")),
        name="cast_concat",
    )(*ws)


def _side_cast_plan(groups, grid):
    nsteps = math.prod(grid)

    def lin(*ids):
        l = ids[0]
        for n, i in zip(grid[1:], ids[1:]):
            l = l * n + i
        return l

    in_specs, args, out_specs, out_shapes, widths = [], [], [], [], []
    for ws, layer in groups:
        R = ws[0].shape[1]
        rows = BF16_ROWS
        while R // rows > nsteps:
            rows *= 2
        assert R % rows == 0
        nblk = R // rows
        rep = nsteps // nblk

        def blk(*ids, rep=rep, nblk=nblk):
            return jnp.minimum(lin(*ids) // rep, nblk - 1)

        for w in ws:
            in_specs.append(pl.BlockSpec((None, rows, w.shape[2]),
                                         lambda *ids, blk=blk, layer=layer: (layer, blk(*ids), 0)))
            args.append(w)
        cols = sum(w.shape[2] for w in ws)
        out_specs.append(pl.BlockSpec((rows, cols), lambda *ids, blk=blk: (blk(*ids), 0)))
        out_shapes.append(jax.ShapeDtypeStruct((R, cols), BF16))
        widths.append(tuple(w.shape[2] for w in ws))
    return in_specs, args, out_specs, out_shapes, tuple(widths)


def _side_cast_body(in_refs, out_refs, widths):
    k = 0
    for o_ref, ws in zip(out_refs, widths):
        c0 = 0
        for wd in ws:
            o_ref[:, c0:c0 + wd] = in_refs[k][...].astype(BF16)
            k += 1
            c0 += wd


def _n_cast_in(widths):
    return sum(len(ws) for ws in widths)


def _norm_matmul_kernel(x_ref, g_ref, w_ref, cs_ref, *rest, has_gate, ret, row_chunk, cast_widths):
    rest = list(rest)
    wz_ref = rest.pop(0) if has_gate else None
    if ret is not None:
        cos_ref, sin_ref, lg_ref = rest.pop(0), rest.pop(0), rest.pop(0)
    cast_in = [rest.pop(0) for _ in range(_n_cast_in(cast_widths))]
    o_ref = rest.pop(0)
    z_ref = rest.pop(0) if has_gate else None
    cast_out = [rest.pop(0) for _ in cast_widths]
    hn_ref, = rest
    tm = x_ref.shape[0]
    _side_cast_body(cast_in, cast_out, cast_widths)

    @pl.when(pl.program_id(1) == 0)
    def _():
        for r0 in range(0, tm, row_chunk):
            y = _rms_rows(x_ref[r0:r0 + row_chunk, :], g_ref[...])
            hn_ref[r0:r0 + row_chunk, :] = y.astype(BF16)
        if has_gate:
            z_ref[...] = jnp.dot(hn_ref[...], wz_ref[...], preferred_element_type=F32)

    tn = w_ref.shape[1]

    def heads(epilogue):
        for c0 in range(0, tn, RET_QK_DIM):
            acc = jnp.dot(hn_ref[...], w_ref[:, c0:c0 + RET_QK_DIM], preferred_element_type=F32)
            epilogue(c0, acc)

    def plain(c0, acc):
        cols = slice(c0, c0 + RET_QK_DIM)
        o_ref[:, cols] = (acc * cs_ref[:, cols]).astype(o_ref.dtype)

    if ret is None:
        heads(plain)
        return

    n_rot, n_plain, chunk = ret
    j = pl.program_id(1)
    half = RET_QK_DIM // 2

    @pl.when(j < n_rot)
    def _():
        t = (lax.broadcasted_iota(jnp.int32, (tm, half), 0) % chunk + 1).astype(F32)
        cos = cos_ref[...]
        sin = sin_ref[...]

        def rotary(c0, acc):
            h1 = slice(c0, c0 + half)
            h2 = slice(c0 + half, c0 + 2 * half)
            x1 = acc[:, :half]
            x2 = acc[:, half:]
            scale = jnp.exp(t * lg_ref[:, h1]) * cs_ref[:, h1]
            o_ref[:, h1] = ((x1 * cos - x2 * sin) * scale).astype(o_ref.dtype)
            o_ref[:, h2] = ((x1 * sin + x2 * cos) * scale).astype(o_ref.dtype)
        heads(rotary)

    @pl.when((j >= n_rot) & (j < n_rot + n_plain))
    def _():
        heads(plain)

    @pl.when(j >= n_rot + n_plain)
    def _():
        def swish(c0, acc):
            cols = slice(c0, c0 + RET_QK_DIM)
            y = acc * cs_ref[:, cols]
            sig = 0.5 * jnp.tanh(0.5 * y) + 0.5
            o_ref[:, cols] = (y * sig).astype(o_ref.dtype)
        heads(swish)


def _norm_matmul(x, g, w, colscale, wz=None, *, ret=None, casts=(), tm=1024, tn=2048):
    T, D = x.shape
    N = w.shape[1]
    tm = _tile(T, tm)
    tn = _tile(N if ret is None else math.gcd(ret[3], ret[4]), tn)
    has_gate = wz is not None
    grid = (T // tm, N // tn)
    c_in, c_args, c_out, c_shapes, c_widths = _side_cast_plan(casts, grid)
    in_specs = [
        pl.BlockSpec((tm, D), lambda i, j: (i, 0)),
        pl.BlockSpec((1, D), lambda i, j: (0, 0)),
        pl.BlockSpec((D, tn), lambda i, j: (0, j)),
        pl.BlockSpec((1, tn), lambda i, j: (0, j)),
    ]
    args = [x, g.reshape(1, D), w, colscale.reshape(1, N)]
    out_shape = [jax.ShapeDtypeStruct((T, N), BF16)]
    out_specs = [pl.BlockSpec((tm, tn), lambda i, j: (i, j))]
    if has_gate:
        in_specs.append(pl.BlockSpec((D, LANES), lambda i, j: (0, 0)))
        args.append(wz)
        out_shape.append(jax.ShapeDtypeStruct((T, LANES), F32))
        out_specs.append(pl.BlockSpec((tm, LANES), lambda i, j: (i, 0)))
    ret_static = None
    if ret is not None:
        cos, sin, lgcol, rot_cols, plain_cols, chunk = ret
        assert rot_cols % tn == 0 and plain_cols % tn == 0 and tm % chunk == 0
        nrt = cos.shape[0] // tm
        half = RET_QK_DIM // 2
        in_specs += [pl.BlockSpec((tm, half), lambda i, j: (i % nrt, 0)),
                     pl.BlockSpec((tm, half), lambda i, j: (i % nrt, 0)),
                     pl.BlockSpec((1, tn), lambda i, j: (0, j))]
        args += [cos, sin, lgcol.reshape(1, N)]
        ret_static = (rot_cols // tn, plain_cols // tn, chunk)
    res = pl.pallas_call(
        functools.partial(_norm_matmul_kernel, has_gate=has_gate, ret=ret_static,
                          row_chunk=min(256, tm), cast_widths=c_widths),
        grid=grid,
        in_specs=in_specs + c_in,
        out_specs=out_specs + c_out,
        out_shape=out_shape + c_shapes,
        scratch_shapes=[pltpu.VMEM((tm, D), BF16)],
        compiler_params=_params(("parallel", "arbitrary")),
        name="norm_matmul_gate" if has_gate else "norm_matmul",
    )(*args, *c_args)
    return res


def _gate_cumsum_kernel(z_ref, b_ref, cp_ref, *, heads):
    z = z_ref[...] + b_ref[...]
    c = jnp.minimum(z, 0.0) - jnp.log1p(jnp.exp(-jnp.abs(z)))
    S = c.shape[0]
    row = lax.broadcasted_iota(jnp.int32, c.shape, 0)
    shift = 1
    while shift < S:
        c = c + jnp.where(row >= shift, pltpu.roll(c, shift, axis=0), 0.0)
        shift *= 2
    c = c * LOG2E
    hi = c.astype(BF16).astype(F32)
    r1 = c - hi
    mid = r1.astype(BF16).astype(F32)
    lo = r1 - mid
    lane = lax.broadcasted_iota(jnp.int32, c.shape, 1)
    packed = jnp.where(lane < heads, hi,
                       jnp.where(lane < 2 * heads, pltpu.roll(mid, heads, axis=1),
                                 jnp.where(lane < 3 * heads, pltpu.roll(lo, 2 * heads, axis=1), 0.0)))
    cp_ref[...] = packed.astype(BF16)


def _gate_cumsum(z, b_pad, batch, heads):
    T = z.shape[0]
    S = T // batch
    assert 3 * heads <= LANES
    return pl.pallas_call(
        functools.partial(_gate_cumsum_kernel, heads=heads),
        grid=(batch,),
        in_specs=[pl.BlockSpec((S, LANES), lambda b: (b, 0)),
                  pl.BlockSpec((1, LANES), lambda b: (0, 0))],
        out_specs=pl.BlockSpec((S, LANES), lambda b: (b, 0)),
        out_shape=jax.ShapeDtypeStruct((T, LANES), BF16),
        compiler_params=_params(("parallel",)),
        name="gate_cumsum",
    )(z, b_pad)


ONES_ROWS = 16
N_PIECES = 3


def _piece_selector(head0, hpg, heads, sign, lane0):
    r = lax.broadcasted_iota(jnp.int32, (LANES, hpg * LANES), 0)
    col = lax.broadcasted_iota(jnp.int32, (LANES, hpg * LANES), 1)
    g = col // LANES
    p = col % LANES - lane0
    hit = (p >= 0) & (p < N_PIECES) & (r == head0 + g + heads * p)
    return jnp.where(hit, sign, 0.0).astype(BF16)


def _fox_attn_kernel(q_ref, k_ref, v_ref, cp_ref, *rest, tq, tk, hpg, heads, cast_widths):
    rest = list(rest)
    cast_in = [rest.pop(0) for _ in range(_n_cast_in(cast_widths))]
    o_ref = rest.pop(0)
    cast_out = [rest.pop(0) for _ in cast_widths]
    kx_ref, vt_ref, acc_ref, st_ref, p_ref = rest
    _side_cast_body(cast_in, cast_out, cast_widths)
    hg = pl.program_id(1)
    qi = pl.program_id(2)
    S = k_ref.shape[0]
    dh = FOX_HEAD_DIM
    lane = lax.broadcasted_iota(jnp.int32, (1, hpg * LANES), 1) % LANES
    cols = [slice(g * dh, (g + 1) * dh) for g in range(hpg)]

    @pl.when(qi == 0)
    def _():
        sel = _piece_selector(hg * hpg, hpg, heads, -1.0, 0)
        ones = jnp.where((lane >= N_PIECES) & (lane < 2 * N_PIECES), 1.0, 0.0)
        for g in range(hpg):
            vt_ref[g, dh:, :] = jnp.ones((ONES_ROWS, S), BF16)

        def body(i, carry):
            sl = pl.ds(pl.multiple_of(i * tq, tq), tq)
            ek = (jnp.dot(cp_ref[sl, :], sel, preferred_element_type=F32) + ones).astype(BF16)
            for g in range(hpg):
                kx_ref[g, sl, :dh] = k_ref[sl, cols[g]]
                kx_ref[g, sl, dh:] = ek[:, cols[g]]
                vt_ref[g, :dh, sl] = v_ref[sl, cols[g]].astype(F32).T.astype(BF16)
            return carry
        lax.fori_loop(0, S // tq, body, 0)

    q0 = pl.multiple_of(qi * tq, tq)
    sel_q = _piece_selector(hg * hpg, hpg, heads, 1.0, N_PIECES)
    cref = jnp.dot(cp_ref[pl.ds(q0, ONES_ROWS), :], sel_q, preferred_element_type=F32)[0:1, :]
    eq = (cref + jnp.where(lane < N_PIECES, 1.0, 0.0)).astype(BF16)
    qxt = [jnp.concatenate([q_ref[:, cols[g]].astype(F32),
                            jnp.broadcast_to(eq[:, cols[g]].astype(F32), (tq, dh))], axis=1
                           ).T.astype(BF16) for g in range(hpg)]

    def scores(g, j, diag):
        k0 = pl.multiple_of(j * tk, tk)
        st = jnp.dot(kx_ref[g, pl.ds(k0, tk), :], qxt[g], preferred_element_type=F32)
        if diag is not None:
            kk = lax.broadcasted_iota(jnp.int32, st.shape, 0) + diag * tk
            qq = lax.broadcasted_iota(jnp.int32, st.shape, 1)
            st = jnp.where(kk <= qq, st, NEG_BIG)
        return st, jnp.max(st, axis=0, keepdims=True)

    def pv_update(g, j, slot, alpha):
        k0 = pl.multiple_of(j * tk, tk)
        pv = jnp.dot(vt_ref[g, :, pl.ds(k0, tk)], p_ref[g, slot], preferred_element_type=F32)
        acc_ref[g] = alpha * acc_ref[g] + pv

    def stage(g, slot, carry, prev_blk, next_blk, next_diag=None):
        m, alpha_prev, cmax = carry
        if prev_blk is not None:
            pv_update(g, prev_blk, 1 - slot, alpha_prev)
        st_next, cmax_next = scores(g, next_blk, next_diag)
        st_ref[g, 1 - slot] = st_next
        m_new = jnp.maximum(m, cmax)
        p_ref[g, slot] = jnp.exp2(st_ref[g, slot] - m_new).astype(BF16)
        return m_new, jnp.exp2(m - m_new), cmax_next

    assert tq == 2 * tk
    d0 = 2 * qi
    carries = []
    for g in range(hpg):
        acc_ref[g] = jnp.zeros(acc_ref.shape[1:], F32)
        st, cmax = scores(g, d0, 0)
        st_ref[g, 0] = st
        carries.append((jnp.full((1, tq), NEG_BIG, F32), jnp.ones((1, tq), F32), cmax))
    carries = [stage(g, 0, carries[g], None, d0 + 1, 1) for g in range(hpg)]
    carries = [stage(g, 1, carries[g], d0, 0) for g in range(hpg)]

    def pair(i, carries):
        out = []
        for g in range(hpg):
            c = stage(g, 0, carries[g], jnp.where(i == 0, d0 + 1, 2 * i - 1), 2 * i + 1)
            out.append(stage(g, 1, c, 2 * i, 2 * i + 2))
        return tuple(out)

    carries = lax.fori_loop(0, qi, pair, tuple(carries))
    last_blk = jnp.where(qi == 0, d0 + 1, d0 - 1)
    for g in range(hpg):
        pv_update(g, last_blk, 1, carries[g][1])
    for g in range(hpg):
        ot = acc_ref[g, :dh, :] / acc_ref[g, dh:dh + 1, :]
        o_ref[:, cols[g]] = ot.T.astype(o_ref.dtype)


def _fox_attention(qkv, cp, *, batch, heads, casts=(), tq=512, hpg=4):
    T = qkv.shape[0]
    S = T // batch
    dh = FOX_HEAD_DIM
    tq = min(tq, S)
    tk = tq // 2
    nq = S // tq
    ng = heads // hpg
    w = hpg * dh
    grid = (batch, ng, nq)
    c_in, c_args, c_out, c_shapes, c_widths = _side_cast_plan(casts, grid)
    return pl.pallas_call(
        functools.partial(_fox_attn_kernel, tq=tq, tk=tk, hpg=hpg, heads=heads,
                          cast_widths=c_widths),
        grid=grid,
        in_specs=[
            pl.BlockSpec((tq, w), lambda b, h, i: (b * nq + i, h)),
            pl.BlockSpec((S, w), lambda b, h, i: (b, ng + h)),
            pl.BlockSpec((S, w), lambda b, h, i: (b, 2 * ng + h)),
            pl.BlockSpec((S, LANES), lambda b, h, i: (b, 0)),
        ] + c_in,
        out_specs=[pl.BlockSpec((tq, w), lambda b, h, i: (b * nq + i, h))] + c_out,
        out_shape=[jax.ShapeDtypeStruct((T, heads * dh), BF16)] + c_shapes,
        scratch_shapes=[pltpu.VMEM((hpg, S, 2 * dh), BF16),
                        pltpu.VMEM((hpg, dh + ONES_ROWS, S), BF16),
                        pltpu.VMEM((hpg, dh + ONES_ROWS, tq), F32),
                        pltpu.VMEM((hpg, 2, tk, tq), F32),
                        pltpu.VMEM((hpg, 2, tk, tq), BF16)],
        compiler_params=_params(("parallel", "parallel", "arbitrary")),
        name="fox_attention",
    )(qkv, qkv, qkv, cp, *c_args)


MXU_COLS = 256


def _matmul_res_kernel(a_ref, w_ref, r_ref, o_ref):
    for c0 in range(0, w_ref.shape[1], MXU_COLS):
        cols = slice(c0, c0 + MXU_COLS)
        o_ref[:, cols] = r_ref[:, cols] + jnp.dot(a_ref[...], w_ref[:, cols],
                                                  preferred_element_type=F32)


def _matmul_res(a, w, res, *, tm=512):
    T, K = a.shape
    N = w.shape[1]
    tm = _tile(T, tm)
    return pl.pallas_call(
        _matmul_res_kernel,
        grid=(T // tm,),
        in_specs=[pl.BlockSpec((tm, K), lambda i: (i, 0)),
                  pl.BlockSpec((K, N), lambda i: (0, 0), pipeline_mode=pl.Buffered(1)),
                  pl.BlockSpec((tm, N), lambda i: (i, 0))],
        out_specs=pl.BlockSpec((tm, N), lambda i: (i, 0)),
        out_shape=jax.ShapeDtypeStruct((T, N), F32),
        compiler_params=_params(("parallel",)),
        name="matmul_res",
    )(a, w, res)


def _mlp_kernel(x_ref, g_ref, wu_ref, wd_ref, *rest, has_final, row_chunk, cast_widths):
    rest = list(rest)
    gf_ref = rest.pop(0) if has_final else None
    cast_in = [rest.pop(0) for _ in range(_n_cast_in(cast_widths))]
    o_ref = rest.pop(0)
    cast_out = [rest.pop(0) for _ in cast_widths]
    hn_ref, = rest
    _side_cast_body(cast_in, cast_out, cast_widths)
    tm = x_ref.shape[0]
    f = pl.program_id(1)
    chunks = [slice(r0, r0 + row_chunk) for r0 in range(0, tm, row_chunk)]

    @pl.when(f == 0)
    def _():
        for sl in chunks:
            x = x_ref[sl, :]
            hn_ref[sl, :] = _rms_rows(x, g_ref[...]).astype(BF16)
            o_ref[sl, :] = x

    u = jnp.dot(hn_ref[...], wu_ref[...], preferred_element_type=F32)
    a = jnp.square(jnp.maximum(u, 0.0)).astype(BF16)
    o_ref[...] += jnp.dot(a, wd_ref[...], preferred_element_type=F32)

    if has_final:
        @pl.when(f == pl.num_programs(1) - 1)
        def _():
            for sl in chunks:
                o_ref[sl, :] = _rms_rows(o_ref[sl, :], gf_ref[...])


def _mlp(x, g, wu, wd, g_final=None, *, casts=(), tm=1024, tf=1024):
    T, D = x.shape
    Fdim = wu.shape[1]
    tm = _tile(T, tm)
    tf = _tile(Fdim, tf)
    has_final = g_final is not None
    in_specs = [pl.BlockSpec((tm, D), lambda i, f: (i, 0)),
                pl.BlockSpec((1, D), lambda i, f: (0, 0)),
                pl.BlockSpec((D, tf), lambda i, f: (0, f)),
                pl.BlockSpec((tf, D), lambda i, f: (f, 0))]
    args = [x, g.reshape(1, D), wu, wd]
    if has_final:
        in_specs.append(pl.BlockSpec((1, D), lambda i, f: (0, 0)))
        args.append(g_final.reshape(1, D))
    grid = (T // tm, Fdim // tf)
    c_in, c_args, c_out, c_shapes, c_widths = _side_cast_plan(casts, grid)
    return pl.pallas_call(
        functools.partial(_mlp_kernel, has_final=has_final, row_chunk=min(256, tm),
                          cast_widths=c_widths),
        grid=grid,
        in_specs=in_specs + c_in,
        out_specs=[pl.BlockSpec((tm, D), lambda i, f: (i, 0))] + c_out,
        out_shape=[jax.ShapeDtypeStruct((T, D), F32)] + c_shapes,
        scratch_shapes=[pltpu.VMEM((tm, D), BF16)],
        compiler_params=_params(("parallel", "arbitrary"), V7X_VMEM_LIMIT_MLP),
        name="mlp_final" if has_final else "mlp",
    )(*args, *c_args)


def _retention_kernel(q_ref, k_ref, v_ref, sg_ref, cd_ref, gn_ref, *rest, chunk, nchunks, hps,
                      cast_widths):
    rest = list(rest)
    cast_in = [rest.pop(0) for _ in range(_n_cast_in(cast_widths))]
    o_ref = rest.pop(0)
    cast_out = [rest.pop(0) for _ in cast_widths]
    state_ref, = rest
    _side_cast_body(cast_in, cast_out, cast_widths)

    @pl.when(pl.program_id(2) == 0)
    def _():
        state_ref[...] = jnp.zeros_like(state_ref)

    dk, dv = RET_QK_DIM, RET_V_DIM
    causal = (lax.broadcasted_iota(jnp.int32, (chunk, chunk), 0)
              >= lax.broadcasted_iota(jnp.int32, (chunk, chunk), 1))
    slices = [slice(ci * chunk, (ci + 1) * chunk) for ci in range(nchunks)]
    for h in range(hps):
        qk = slice(h * dk, (h + 1) * dk)
        vc = slice(h * dv, (h + 1) * dv)
        intra, kv = [], []
        for sl in slices:
            q = q_ref[sl, qk]
            k = k_ref[sl, qk]
            v = v_ref[sl, vc]
            inner = lax.dot_general(q, k, _NT, preferred_element_type=F32)
            inner = jnp.where(causal, inner, 0.0).astype(BF16)
            intra.append(jnp.dot(inner, v, preferred_element_type=F32))
            k_t = k.astype(F32).T.astype(BF16)
            kv.append(jnp.dot(k_t, v, preferred_element_type=F32))

        for ci, sl in enumerate(slices):
            state = state_ref[h]
            o = intra[ci] + jnp.dot(q_ref[sl, qk], state.astype(BF16),
                                    preferred_element_type=F32)
            state_ref[h] = cd_ref[h] * (state + kv[ci])

            ms = jnp.mean(o * o, axis=-1, keepdims=True)
            on = o * lax.rsqrt(ms + RMS_EPS) * gn_ref[:, vc]
            o_ref[sl, vc] = (sg_ref[sl, vc].astype(F32) * on).astype(o_ref.dtype)


def _retention_tables(heads, chunk, seq):
    dk, dv = RET_QK_DIM, RET_V_DIM
    half = dk // 2
    pos = np.arange(seq, dtype=np.float64)
    inv = ROPE_BASE ** (-np.arange(half, dtype=np.float64) / half)
    ang = pos[:, None] * inv[None, :]
    log_gamma = np.log1p(-np.exp2(-5.0 - np.arange(heads, dtype=np.float64)))
    chunk_decay = np.broadcast_to(np.exp(log_gamma * chunk)[:, None, None], (heads, 1, dv))
    return (jnp.asarray(np.cos(ang), dtype=F32), jnp.asarray(np.sin(ang), dtype=F32),
            log_gamma, jnp.asarray(chunk_decay, dtype=F32))


def _retention(proj, gn, cd, *, batch, heads, chunk, casts=(), rows=1024, hps=4):
    T = proj.shape[0]
    S = T // batch
    hps = math.gcd(hps, heads)
    dk, dv = hps * RET_QK_DIM, hps * RET_V_DIM
    rows = min(rows, S)
    ns = S // rows
    heads = heads // hps
    vblk0 = 2 * heads * dk // dv
    grid = (batch, heads, ns)
    c_in, c_args, c_out, c_shapes, c_widths = _side_cast_plan(casts, grid)
    return pl.pallas_call(
        functools.partial(_retention_kernel, chunk=chunk, nchunks=rows // chunk, hps=hps,
                          cast_widths=c_widths),
        grid=grid,
        in_specs=[
            pl.BlockSpec((rows, dk), lambda b, h, s: (b * ns + s, h)),
            pl.BlockSpec((rows, dk), lambda b, h, s: (b * ns + s, heads + h)),
            pl.BlockSpec((rows, dv), lambda b, h, s: (b * ns + s, vblk0 + h)),
            pl.BlockSpec((rows, dv), lambda b, h, s: (b * ns + s, vblk0 + heads + h)),
            pl.BlockSpec((hps, 1, RET_V_DIM), lambda b, h, s: (h, 0, 0)),
            pl.BlockSpec((1, dv), lambda b, h, s: (0, h)),
        ] + c_in,
        out_specs=[pl.BlockSpec((rows, dv), lambda b, h, s: (b * ns + s, h))] + c_out,
        out_shape=[jax.ShapeDtypeStruct((T, heads * dv), BF16)] + c_shapes,
        scratch_shapes=[pltpu.VMEM((hps, RET_QK_DIM, RET_V_DIM), F32)],
        compiler_params=_params(("parallel", "parallel", "arbitrary")),
        name="retention",
    )(proj, proj, proj, proj, cd, gn.reshape(1, heads * dv), *c_args)


def kernel(x, fox_norm, fox_wq, fox_wk, fox_wv, fox_wf, fox_bf, fox_wo, ret_norm, ret_wq, ret_wk,
           ret_wv, ret_wg, ret_gn, ret_wo, mlp_norm, mlp_up, mlp_down, final_norm):
    B, S, D = x.shape
    T = B * S
    fox_heads = D // FOX_HEAD_DIM
    ret_heads = D // RET_QK_DIM
    h = x.reshape(T, D)

    w_qkv = _cast_concat([fox_wq, fox_wk, fox_wv], 0)
    cs = jnp.concatenate([jnp.full((D,), LOG2E * FOX_HEAD_DIM ** -0.5, F32),
                          jnp.ones((2 * D,), F32)])
    wf_pad = jnp.pad(fox_wf[0], ((0, 0), (0, LANES - fox_heads))).astype(BF16)
    bf_pad = jnp.pad(fox_bf[0], (0, LANES - fox_heads)).reshape(1, LANES)
    qkv, z = _norm_matmul(h, fox_norm[0], w_qkv, cs, wf_pad)
    cp = _gate_cumsum(z, bf_pad, B, fox_heads)
    attn, wo0, up0, down0, w_proj, wo1, up1, down1 = _fox_attention(
        qkv, cp, batch=B, heads=fox_heads,
        casts=[([fox_wo], 0), ([mlp_up], 0), ([mlp_down], 0),
               ([ret_wq, ret_wk, ret_wv, ret_wg], 0), ([ret_wo], 0),
               ([mlp_up], 1), ([mlp_down], 1)])
    h = _matmul_res(attn, wo0, h)
    h, = _mlp(h, mlp_norm[0], up0, down0)

    chunk = min(RET_CHUNK, S)
    cos, sin, log_gamma, cd = _retention_tables(ret_heads, chunk, S)
    rv = ret_heads * RET_V_DIM
    cs = jnp.concatenate([jnp.ones((D,), F32), jnp.full((D,), RET_QK_DIM ** -0.5, F32),
                          jnp.ones((2 * rv,), F32)])
    lg_cols = np.repeat(log_gamma, RET_QK_DIM)
    lgcol = jnp.asarray(np.concatenate([lg_cols, -lg_cols, np.zeros(2 * rv)]), dtype=F32)
    proj, = _norm_matmul(h, ret_norm[0], w_proj, cs, ret=(cos, sin, lgcol, 2 * D, rv, chunk))
    y, = _retention(proj, ret_gn[0], cd, batch=B, heads=ret_heads, chunk=chunk)
    h = _matmul_res(y, wo1, h)
    h, = _mlp(h, mlp_norm[1], up1, down1, final_norm)
    return h.reshape(B, S, D)
```
